```python
import math
import jax, jax.numpy as jnp
from jax import lax
import numpy as np

D_MODEL = 1024
BATCH = 8
SEQ = 4096
DEPTH = 1

HEAD_DIM = 64
N_ATTN_HEADS = 12
ATTN_WIDTH = N_ATTN_HEADS * HEAD_DIM
N_FOURIER_GROUPS = 4
FOURIER_GROUP_DIM = 64
FOURIER_WIDTH = N_FOURIER_GROUPS * FOURIER_GROUP_DIM
MIX_WIDTH = ATTN_WIDTH + FOURIER_WIDTH
IN_PROJ_WIDTH = 3 * ATTN_WIDTH + FOURIER_WIDTH
DILATED_PATTERNS = ((128, 1), (512, 4), (2048, 16))
N_REL_BUCKETS = 32
REL_MAX_DISTANCE = 1024
D_FF = 2816
CONV_WIDTH = 3
EPS = 1e-6
NEG_INF = -1e30

kernel_name = "hymba_dilated_fnet_convglu_block"


def _rms_norm(x, g):
    xf = x.astype(jnp.float32)
    y = xf * lax.rsqrt(jnp.mean(xf * xf, axis=-1, keepdims=True) + EPS)
    return (y * g.astype(jnp.float32)).astype(x.dtype)


def _t5_bucket(rel):
    nb = N_REL_BUCKETS // 2
    max_exact = nb // 2
    ret = jnp.where(rel > 0, nb, 0)
    n = jnp.abs(rel)
    nf = jnp.maximum(n, 1).astype(jnp.float32)
    large = max_exact + (jnp.log(nf / max_exact) / math.log(REL_MAX_DISTANCE / max_exact)
                         * (nb - max_exact)).astype(jnp.int32)
    large = jnp.minimum(large, nb - 1)
    return ret + jnp.where(n < max_exact, n, large)


def _dilated_branch(q, k, v, rel_table, window, dilation):
    B, S, H, hd = q.shape
    half = window // (2 * dilation)
    L = S // dilation
    nb = -(-L // half)
    Lp = nb * half

    def to_sub(t, extra):
        t = t.reshape(B, L, dilation, H, hd).transpose(0, 2, 1, 3, 4)
        return jnp.pad(t, ((0, 0), (0, 0), (extra, Lp - L + extra), (0, 0), (0, 0)))

    def windows(t):
        blk = to_sub(t, half).reshape(B, dilation, nb + 2, half, H, hd)
        return jnp.concatenate([blk[:, :, :-2], blk[:, :, 1:-1], blk[:, :, 2:]], axis=3)

    qb = to_sub(q, 0).reshape(B, dilation, nb, half, H, hd)
    kw, vw = windows(k), windows(v)

    qi = jnp.arange(half)
    kj = jnp.arange(3 * half)
    rel = kj[None, :] - half - qi[:, None]
    kidx = (jnp.arange(nb)[:, None] - 1) * half + kj[None, :]
    mask = (jnp.abs(rel) <= half)[None] & ((kidx >= 0) & (kidx < L))[:, None, :]
    bias = rel_table[_t5_bucket(rel * dilation)].astype(jnp.float32).transpose(2, 0, 1)

    s = jnp.einsum('brnqhd,brnkhd->brnhqk', qb, kw) * (hd ** -0.5) + bias
    s = jnp.where(mask[:, None], s, NEG_INF)
    m = jnp.max(s, axis=-1, keepdims=True)
    p = jnp.exp(s - m)
    l = jnp.sum(p, axis=-1)
    o = jnp.einsum('brnhqk,brnkhd->brnqhd', p, vw) / l.transpose(0, 1, 2, 4, 3)[..., None]
    lse = (m[..., 0] + jnp.log(l)).transpose(0, 1, 2, 4, 3)

    o = o.reshape(B, dilation, Lp, H, hd)[:, :, :L].transpose(0, 2, 1, 3, 4).reshape(B, S, H, hd)
    lse = lse.reshape(B, dilation, Lp, H)[:, :, :L].transpose(0, 2, 1, 3).reshape(B, S, H)
    return o, lse


def _dilated_attention(q, k, v, rel_table):
    outs, lses = [], []
    for window, dilation in DILATED_PATTERNS:
        o, lse = _dilated_branch(q, k, v, rel_table, window, dilation)
        outs.append(o)
        lses.append(lse)
    w = jax.nn.softmax(jnp.stack(lses, axis=0), axis=0)
    return jnp.sum(w[..., None] * jnp.stack(outs, axis=0), axis=0)


def _fourier_mix(u, w, b):
    f = jnp.fft.fft2(u.astype(jnp.float32), axes=(1, 3), norm="ortho").real
    return jnp.einsum('bsgc,gcd->bsgd', f, w.astype(jnp.float32)) + b.astype(jnp.float32)


def _conv_glu_ffn(h, w_gate, w_val, conv_w, conv_b, w_down):
    g = h @ w_gate
    val = h @ w_val
    pad = CONV_WIDTH // 2
    g = lax.conv_general_dilated(
        g, conv_w.astype(g.dtype)[:, None, :], window_strides=(1,), padding=((pad, pad),),
        dimension_numbers=('NWC', 'WIO', 'NWC'), feature_group_count=D_FF) + conv_b
    return (jax.nn.silu(g) * val) @ w_down


def setup_inputs(seed: int = 0) -> dict:
    key = jax.random.key(seed)
    ks = jax.random.split(key, 20)
    nrm = lambda k, shape, scale: jax.random.normal(k, shape, jnp.float32) * scale
    gain = lambda k, shape: 1.0 + 0.01 * jax.random.normal(k, shape, jnp.float32)
    return {
        "x": nrm(ks[0], (BATCH, SEQ, D_MODEL), 1.0),
        "norm_mix_gain": gain(ks[1], (DEPTH, D_MODEL)),
        "w_in": nrm(ks[2], (DEPTH, D_MODEL, IN_PROJ_WIDTH), D_MODEL ** -0.5),
        "attn_out_gain": gain(ks[3], (DEPTH, ATTN_WIDTH)),
        "rel_bias_table": nrm(ks[4], (N_REL_BUCKETS, N_ATTN_HEADS), 0.5),
        "fourier_w": nrm(ks[5], (DEPTH, N_FOURIER_GROUPS, FOURIER_GROUP_DIM, FOURIER_GROUP_DIM), FOURIER_GROUP_DIM ** -0.5),
        "fourier_b": nrm(ks[6], (DEPTH, N_FOURIER_GROUPS, FOURIER_GROUP_DIM), 0.01),
        "fourier_out_gain": gain(ks[7], (DEPTH, FOURIER_WIDTH)),
        "w_out": nrm(ks[8], (DEPTH, MIX_WIDTH, D_MODEL), MIX_WIDTH ** -0.5),
        "norm_ffn_gain": gain(ks[9], (DEPTH, D_MODEL)),
        "w_gate": nrm(ks[10], (DEPTH, D_MODEL, D_FF), D_MODEL ** -0.5),
        "w_val": nrm(ks[11], (DEPTH, D_MODEL, D_FF), D_MODEL ** -0.5),
        "conv_w": nrm(ks[12], (DEPTH, CONV_WIDTH, D_FF), CONV_WIDTH ** -0.5),
        "conv_b": nrm(ks[13], (DEPTH, D_FF), 0.01),
        "w_down": nrm(ks[14], (DEPTH, D_FF, D_MODEL), D_FF ** -0.5),
        "final_norm_gain": gain(ks[15], (D_MODEL,)),
    }


def reference(x, norm_mix_gain, w_in, attn_out_gain, rel_bias_table, fourier_w, fourier_b,
              fourier_out_gain, w_out, norm_ffn_gain, w_gate, w_val, conv_w, conv_b, w_down,
              final_norm_gain):
    B, S, _ = x.shape
    for layer in range(DEPTH):
        h = _rms_norm(x, norm_mix_gain[layer])
        proj = h @ w_in[layer]
        q = proj[..., :ATTN_WIDTH].reshape(B, S, N_ATTN_HEADS, HEAD_DIM).astype(jnp.float32)
        k = proj[..., ATTN_WIDTH:2 * ATTN_WIDTH].reshape(B, S, N_ATTN_HEADS, HEAD_DIM).astype(jnp.float32)
        v = proj[..., 2 * ATTN_WIDTH:3 * ATTN_WIDTH].reshape(B, S, N_ATTN_HEADS, HEAD_DIM).astype(jnp.float32)
        u = proj[..., 3 * ATTN_WIDTH:].reshape(B, S, N_FOURIER_GROUPS, FOURIER_GROUP_DIM)

        attn = _dilated_attention(q, k, v, rel_bias_table).reshape(B, S, ATTN_WIDTH).astype(x.dtype)
        four = _fourier_mix(u, fourier_w[layer], fourier_b[layer]).reshape(B, S, FOURIER_WIDTH).astype(x.dtype)
        mixed = jnp.concatenate([_rms_norm(attn, attn_out_gain[layer]),
                                 _rms_norm(four, fourier_out_gain[layer])], axis=-1)
        x = x + mixed @ w_out[layer]

        h = _rms_norm(x, norm_ffn_gain[layer])
        x = x + _conv_glu_ffn(h, w_gate[layer], w_val[layer], conv_w[layer], conv_b[layer], w_down[layer])
    return _rms_norm(x, final_norm_gain)
```

```python
import functools
import math

import numpy as np
import jax
import jax.numpy as jnp
from jax import lax
from jax.experimental import pallas as pl
from jax.experimental.pallas import tpu as pltpu

EPS = 1e-6
NEG_INF = -1e30
HEAD_DIM = 64
DILATED_PATTERNS = ((128, 1), (512, 4), (2048, 16))
N_REL_BUCKETS = 32
REL_MAX_DISTANCE = 1024

LANES = 128
HALF_WINDOW = 64
TQ = 2 * HALF_WINDOW
TK = 4 * HALF_WINDOW
FFT_RADIX = 64
HALO = 8
VMEM_LIMIT = 56 * 1024 * 1024

BF16 = jnp.bfloat16
F32 = jnp.float32


def _cparams(n_axes):
    return pltpu.CompilerParams(dimension_semantics=("arbitrary",) * n_axes,
                                vmem_limit_bytes=VMEM_LIMIT)


def _rms(x, gain):
    ms = jnp.mean(x * x, axis=-1, keepdims=True)
    return x * lax.rsqrt(ms + EPS) * gain


def _prep_kernel(wu_ref, fw_ref, cc_ref, sc_ref, wp_ref, wq_ref):
    hi = lax.Precision.HIGHEST
    fw = fw_ref[...]
    a = jnp.dot(cc_ref[...], fw, precision=hi, preferred_element_type=F32)
    b = jnp.dot(sc_ref[...], fw, precision=hi, preferred_element_type=F32)
    wu = wu_ref[...]
    wp_ref[...] = jnp.dot(wu, a, precision=hi, preferred_element_type=F32)
    wq_ref[...] = jnp.dot(wu, b, precision=hi, preferred_element_type=F32)


def _fold_fourier_weights(w_u, fourier_w, seq):
    d_model = w_u.shape[0]
    groups, gd, _ = fourier_w.shape
    ang = 2.0 * np.pi * np.outer(np.arange(gd), np.arange(gd)) / gd
    scale = 1.0 / math.sqrt(seq * gd)
    cc = jnp.asarray(np.cos(ang) * scale, F32)
    sc = jnp.asarray(np.sin(ang) * scale, F32)
    wu_g = w_u.reshape(d_model, groups, gd).transpose(1, 0, 2)
    wp, wq = pl.pallas_call(
        _prep_kernel,
        grid=(groups,),
        in_specs=[pl.BlockSpec((None, d_model, gd), lambda g: (g, 0, 0)),
                  pl.BlockSpec((None, gd, gd), lambda g: (g, 0, 0)),
                  pl.BlockSpec((gd, gd), lambda g: (0, 0)),
                  pl.BlockSpec((gd, gd), lambda g: (0, 0))],
        out_specs=[pl.BlockSpec((None, d_model, gd), lambda g: (g, 0, 0))] * 2,
        out_shape=[jax.ShapeDtypeStruct((groups, d_model, gd), F32)] * 2,
        compiler_params=_cparams(1),
        name="fourier_weight_fold",
    )(wu_g, fourier_w, cc, sc)
    unfold = lambda w: w.transpose(1, 0, 2).reshape(d_model, groups * gd)
    return unfold(wp), unfold(wq)


def _inproj_kernel(x_ref, g_ref, w_ref, q_ref, k_ref, v_ref, pq_ref, *, n_hp, fw):
    h = _rms(x_ref[...], g_ref[...]).astype(BF16)

    def proj(c0, n):
        return jnp.dot(h, w_ref[:, c0:c0 + n], preferred_element_type=F32)

    aw = n_hp * LANES
    for t, ref in enumerate((q_ref, k_ref, v_ref)):
        for c in range(n_hp // 2):
            res = proj(t * aw + c * 2 * LANES, 2 * LANES)
            ref[2 * c] = res[:, :LANES].astype(BF16)
            ref[2 * c + 1] = res[:, LANES:].astype(BF16)
    for t in range(2):
        pq_ref[t] = proj(3 * aw + t * fw, fw).astype(BF16)


def _inproj(x2d, gain, w_all, batch, seq, n_hp, fw, tm):
    n_tok, d_model = x2d.shape
    nt = seq // tm
    tok_map = lambda i: (i // nt, 0, i % nt, 0)
    hp_shape = jax.ShapeDtypeStruct((batch, n_hp, seq, LANES), BF16)
    return pl.pallas_call(
        functools.partial(_inproj_kernel, n_hp=n_hp, fw=fw),
        grid=(n_tok // tm,),
        in_specs=[pl.BlockSpec((tm, d_model), lambda i: (i, 0)),
                  pl.BlockSpec((1, d_model), lambda i: (0, 0)),
                  pl.BlockSpec(w_all.shape, lambda i: (0, 0))],
        out_specs=[pl.BlockSpec((None, n_hp, tm, LANES), tok_map)] * 3
        + [pl.BlockSpec((None, 2, tm, fw), tok_map)],
        out_shape=[hp_shape] * 3 + [jax.ShapeDtypeStruct((batch, 2, seq, fw), BF16)],
        compiler_params=_cparams(1),
        name="rmsnorm_inproj",
    )(x2d, gain, w_all)


def _t5_bucket(rel):
    nb = N_REL_BUCKETS // 2
    max_exact = nb // 2
    ret = jnp.where(rel > 0, nb, 0)
    n = jnp.abs(rel)
    nf = jnp.maximum(n, 1).astype(F32)
    large = max_exact + (jnp.log(nf / max_exact) / math.log(REL_MAX_DISTANCE / max_exact)
                         * (nb - max_exact)).astype(jnp.int32)
    large = jnp.minimum(large, nb - 1)
    return ret + jnp.where(n < max_exact, n, large)


def _bias_tiles(rel_table, dilation):
    qi = np.arange(TQ)[:, None]
    kc = np.arange(TK)[None, :]
    offsets = np.array([0, HALF_WINDOW, 2 * HALF_WINDOW])
    rel = kc[None] - offsets[:, None, None] - qi[None]
    in_band = np.abs(rel) <= HALF_WINDOW
    steps = np.arange(-HALF_WINDOW, HALF_WINDOW + 1)
    band_bias = rel_table[_t5_bucket(jnp.asarray(steps * dilation, jnp.int32))].astype(F32)
    idx = np.clip(rel + HALF_WINDOW, 0, 2 * HALF_WINDOW)
    tiles = jnp.where(jnp.asarray(in_band)[..., None], band_bias[idx], NEG_INF)
    return tiles.transpose(0, 3, 1, 2)


def _attn_kernel(*refs, dilation, sub_len, n_prev, write_lse):
    q_ref, k_ref, v_ref, bias_ref = refs[:4]
    prev = refs[4:4 + 2 * n_prev]
    o_ref = refs[4 + 2 * n_prev]
    lse_ref = refs[5 + 2 * n_prev] if write_lse else None
    n_blk = sub_len // TQ
    lane = lax.broadcasted_iota(jnp.int32, (TQ, LANES), 1)
    first_head = lane < HEAD_DIM
    nt_dims = (((1,), (1,)), ((), ()))

    def block(n, carry):
        q_row = pl.multiple_of(n * TQ, TQ)
        k_row = pl.multiple_of(jnp.clip(n * TQ - HALF_WINDOW, 0, sub_len - TK), HALF_WINDOW)
        edge = (n > 0).astype(jnp.int32) + (n == n_blk - 1).astype(jnp.int32)
        for r in range(dilation):
            cols = slice(r * LANES, (r + 1) * LANES)
            qb = q_ref[pl.ds(q_row, TQ), cols]
            kb = k_ref[pl.ds(k_row, TK), cols]
            vb = v_ref[pl.ds(k_row, TK), cols]
            zero = jnp.zeros_like(qb)
            outs, lses = [], []
            for hd in range(2):
                qh = jnp.where(first_head if hd == 0 else ~first_head, qb, zero)
                s = lax.dot_general(qh, kb, nt_dims, preferred_element_type=F32) + bias_ref[edge, hd]
                m = jnp.max(s, axis=-1, keepdims=True)
                p = jnp.exp(s - m)
                l = jnp.sum(p, axis=-1, keepdims=True)
                pv = jnp.dot(p.astype(BF16), vb, preferred_element_type=F32)
                outs.append(pv * (1.0 / l))
                lses.append(m + jnp.log(l))
            o = jnp.where(first_head, outs[0], outs[1])
            lse = jnp.where(first_head, lses[0], lses[1])
            if n_prev:
                os_, ls_ = [o], [lse]
                for j in range(n_prev):
                    os_.append(prev[2 * j][pl.ds(q_row, TQ), cols].astype(F32))
                    ls_.append(prev[2 * j + 1][pl.ds(q_row, TQ), cols])
                top = functools.reduce(jnp.maximum, ls_)
                ws = [jnp.exp(x - top) for x in ls_]
                den = functools.reduce(lambda a, b: a + b, ws)
                num = functools.reduce(lambda a, b: a + b, [w * x for w, x in zip(ws, os_)])
                o = num * (1.0 / den)
            o_ref[pl.ds(q_row, TQ), cols] = o.astype(BF16)
            if write_lse:
                lse_ref[pl.ds(q_row, TQ), cols] = lse
        return carry

    lax.fori_loop(0, n_blk, block, 0)


def _attn_branch(q, k, v, rel_table, dilation, prev, write_lse):
    batch, n_hp, seq, _ = q.shape
    sub_len = seq // dilation
    width = dilation * LANES
    view = lambda a: a.reshape(batch, n_hp, sub_len, width)
    slab = pl.BlockSpec((None, None, sub_len, width), lambda b, h: (b, h, 0, 0))
    bias = _bias_tiles(rel_table, dilation)
    args = [view(q), view(k), view(v), bias]
    in_specs = [slab, slab, slab, pl.BlockSpec((3, 2, TQ, TK), lambda b, h: (0, h, 0, 0))]
    for o_p, lse_p in prev:
        args += [view(o_p), view(lse_p)]
        in_specs += [slab, slab]
    out_shape = [jax.ShapeDtypeStruct((batch, n_hp, sub_len, width), BF16)]
    if write_lse:
        out_shape.append(jax.ShapeDtypeStruct((batch, n_hp, sub_len, width), F32))
    res = pl.pallas_call(
        functools.partial(_attn_kernel, dilation=dilation, sub_len=sub_len,
                          n_prev=len(prev), write_lse=write_lse),
        grid=(batch, n_hp),
        in_specs=in_specs,
        out_specs=[slab] * len(out_shape),
        out_shape=out_shape,
        compiler_params=_cparams(2),
        name=f"dilated_attn_d{dilation}",
    )(*args)
    return [a.reshape(batch, n_hp, seq, LANES) for a in res]


def _fft_stage1_kernel(w_ref, v_ref, z_ref, *, chunk):
    w = w_ref[...]
    for c0 in range(0, v_ref.shape[-1], chunk):
        z = jnp.dot(w, v_ref[:, c0:c0 + chunk], preferred_element_type=F32)
        z_ref[:, c0:c0 + chunk] = z.astype(BF16)


def _fft_stage2_kernel(m_ref, z_ref, b_ref, o_ref, *, group, fw):
    for j in range(group):
        zc = jnp.concatenate([z_ref[0, j], z_ref[1, j]], axis=0)
        x = jnp.dot(m_ref[j], zc, preferred_element_type=F32) + b_ref[...]
        o_ref[:, j * fw:(j + 1) * fw] = x.astype(BF16)


def _fourier(pq, fourier_b):
    batch, _, seq, fw = pq.shape
    R = FFT_RADIX
    assert seq == R * R
    i = np.arange(R)
    ang1 = 2.0 * np.pi * np.outer(i, i) / R
    c1, s1 = np.cos(ang1), np.sin(ang1)
    w_cat = jnp.asarray(np.block([[c1, -s1], [-s1, -c1]]), BF16)
    k_all = i[:, None, None] + R * i[None, :, None]
    ang2 = 2.0 * np.pi * ((k_all * i[None, None, :]) % seq) / seq
    m_cat = jnp.asarray(np.concatenate([np.cos(ang2), np.sin(ang2)], axis=-1), BF16)

    v = pq.reshape(batch, 2 * R, R * fw)
    z = pl.pallas_call(
        functools.partial(_fft_stage1_kernel, chunk=2048),
        grid=(batch,),
        in_specs=[pl.BlockSpec((2 * R, 2 * R), lambda b: (0, 0)),
                  pl.BlockSpec((None, 2 * R, R * fw), lambda b: (b, 0, 0))],
        out_specs=pl.BlockSpec((None, 2 * R, R * fw), lambda b: (b, 0, 0)),
        out_shape=jax.ShapeDtypeStruct((batch, 2 * R, R * fw), BF16),
        compiler_params=_cparams(1),
        name="fft_stage1",
    )(w_cat, v)

    group = 16
    z5 = z.reshape(batch, 2, R, R, fw)
    out = pl.pallas_call(
        functools.partial(_fft_stage2_kernel, group=group, fw=fw),
        grid=(batch, R // group),
        in_specs=[pl.BlockSpec((group, R, 2 * R), lambda b, g: (g, 0, 0)),
                  pl.BlockSpec((None, 2, group, R, fw), lambda b, g: (b, 0, g, 0, 0)),
                  pl.BlockSpec((1, fw), lambda b, g: (0, 0))],
        out_specs=pl.BlockSpec((None, R, group * fw), lambda b, g: (b, 0, g)),
        out_shape=jax.ShapeDtypeStruct((batch, R, R * fw), BF16),
        compiler_params=_cparams(2),
        name="fft_stage2",
    )(m_cat, z5, fourier_b.reshape(1, fw).astype(F32))
    return out.reshape(batch, seq, fw)


def _outproj_kernel(a_ref, f_ref, x_ref, ga_ref, gf_ref, w_ref, o_ref, *, n_hp):
    parts = [a_ref[j].astype(F32) for j in range(n_hp)]
    ssq = functools.reduce(lambda a, b: a + b,
                           [jnp.sum(p * p, axis=-1, keepdims=True) for p in parts])
    inv = lax.rsqrt(ssq / (n_hp * LANES) + EPS)
    ga = ga_ref[...]
    cols = [(p * inv * ga[:, j * LANES:(j + 1) * LANES]).astype(BF16) for j, p in enumerate(parts)]
    cols.append(_rms(f_ref[...].astype(F32), gf_ref[...]).astype(BF16))
    mixed = jnp.concatenate(cols, axis=1)
    o_ref[...] = x_ref[...] + jnp.dot(mixed, w_ref[...], preferred_element_type=F32)


def _outproj(attn, four, x2d, ga, gf, w_out, tm):
    batch, n_hp, seq, _ = attn.shape
    fw = four.shape[-1]
    n_tok, d_model = x2d.shape
    nt = seq // tm
    return pl.pallas_call(
        functools.partial(_outproj_kernel, n_hp=n_hp),
        grid=(n_tok // tm,),
        in_specs=[pl.BlockSpec((None, n_hp, tm, LANES), lambda i: (i // nt, 0, i % nt, 0)),
                  pl.BlockSpec((None, tm, fw), lambda i: (i // nt, i % nt, 0)),
                  pl.BlockSpec((tm, d_model), lambda i: (i, 0)),
                  pl.BlockSpec((1, n_hp * LANES), lambda i: (0, 0)),
                  pl.BlockSpec((1, fw), lambda i: (0, 0)),
                  pl.BlockSpec(w_out.shape, lambda i: (0, 0))],
        out_specs=pl.BlockSpec((tm, d_model), lambda i: (i, 0)),
        out_shape=jax.ShapeDtypeStruct((n_tok, d_model), F32),
        compiler_params=_cparams(1),
        name="mix_outproj",
    )(attn, four, x2d, ga, gf, w_out)


def _ffn_kernel(xp_ref, x_ref, xn_ref, g_ref, wg_ref, wv_ref, cw_ref, cb_ref, wd_ref, gfin_ref,
                o_ref, acc_ref, *, tiles_per_seq, fc):
    i = pl.program_id(0)
    tm = x_ref.shape[0]
    gain = g_ref[...]
    x = x_ref[...]
    keep_prev = (i % tiles_per_seq != 0).astype(F32)
    keep_next = (i % tiles_per_seq != tiles_per_seq - 1).astype(F32)
    x_ext = jnp.concatenate([xp_ref[...], x, xn_ref[...]], axis=0)
    h_ext = _rms(x_ext, gain)
    row = lax.broadcasted_iota(jnp.int32, (tm + 2 * HALO, 1), 0)
    keep = jnp.where(row < HALO, keep_prev, jnp.where(row >= HALO + tm, keep_next, 1.0))
    h_ext = (h_ext * keep).astype(BF16)
    h = _rms(x, gain).astype(BF16)
    n_ext = tm + 2 * HALO
    d_ff = wg_ref.shape[1]
    for c0 in range(0, d_ff, fc):
        cs = slice(c0, c0 + fc)
        g = jnp.dot(h_ext, wg_ref[:, cs], preferred_element_type=F32)
        g_prev = pltpu.roll(g, 1, axis=0)[HALO:HALO + tm]
        g_next = pltpu.roll(g, n_ext - 1, axis=0)[HALO:HALO + tm]
        cw = cw_ref[:, cs]
        conv = cw[0:1] * g_prev + cw[1:2] * g[HALO:HALO + tm] + cw[2:3] * g_next + cb_ref[:, cs]
        val = jnp.dot(h, wv_ref[:, cs], preferred_element_type=F32)
        act = (conv * (1.0 / (1.0 + jnp.exp(-conv))) * val).astype(BF16)
        down = jnp.dot(act, wd_ref[cs, :], preferred_element_type=F32)
        if c0 == 0:
            acc_ref[...] = down
        else:
            acc_ref[...] += down
    o_ref[...] = _rms(x + acc_ref[...], gfin_ref[...])


def _ffn(x1, gain, wg, wv, conv_w, conv_b, wd, gfin, seq, tm, fc):
    n_tok, d_model = x1.shape
    d_ff = wg.shape[1]
    blocks_per_tile = tm // HALO
    n_halo_blocks = n_tok // HALO
    const = lambda shape: pl.BlockSpec(shape, lambda i: (0,) * len(shape))
    return pl.pallas_call(
        functools.partial(_ffn_kernel, tiles_per_seq=seq // tm, fc=fc),
        grid=(n_tok // tm,),
        in_specs=[pl.BlockSpec((HALO, d_model), lambda i: (jnp.maximum(i * blocks_per_tile - 1, 0), 0)),
                  pl.BlockSpec((tm, d_model), lambda i: (i, 0)),
                  pl.BlockSpec((HALO, d_model),
                               lambda i: (jnp.minimum((i + 1) * blocks_per_tile, n_halo_blocks - 1), 0)),
                  const((1, d_model)), const(wg.shape), const(wv.shape),
                  const(conv_w.shape), const((1, d_ff)), const(wd.shape), const((1, d_model))],
        out_specs=pl.BlockSpec((tm, d_model), lambda i: (i, 0)),
        out_shape=jax.ShapeDtypeStruct((n_tok, d_model), F32),
        scratch_shapes=[pltpu.VMEM((tm, d_model), F32)],
        compiler_params=_cparams(1),
        name="convglu_ffn",
    )(x1, x1, x1, gain, wg, wv, conv_w, conv_b, wd, gfin)


def kernel(x, norm_mix_gain, w_in, attn_out_gain, rel_bias_table, fourier_w, fourier_b, fourier_out_gain, w_out, norm_ffn_gain, w_gate, w_val, conv_w, conv_b, w_down, final_norm_gain):
    batch, seq, d_model = x.shape
    depth = w_in.shape[0]
    n_heads = rel_bias_table.shape[1]
    attn_w = n_heads * HEAD_DIM
    n_hp = attn_w // LANES
    fw = fourier_w.shape[1] * fourier_w.shape[2]
    d_ff = w_gate.shape[-1]
    assert all(w // (2 * d) == HALF_WINDOW for w, d in DILATED_PATTERNS)
    tm = 512
    row = lambda g: g.reshape(1, -1).astype(F32)

    x2d = x.reshape(batch * seq, d_model)
    for layer in range(depth):
        w_l = w_in[layer]
        wp, wq = _fold_fourier_weights(w_l[:, 3 * attn_w:], fourier_w[layer], seq)
        w_all = jnp.concatenate([w_l[:, :attn_w] * (HEAD_DIM ** -0.5), w_l[:, attn_w:3 * attn_w], wp, wq],
                                axis=1).astype(BF16)
        q, k, v, pq = _inproj(x2d, row(norm_mix_gain[layer]), w_all, batch, seq, n_hp, fw, tm)

        branches = sorted(DILATED_PATTERNS, key=lambda wd: -wd[1])
        prev = []
        for idx, (_, dilation) in enumerate(branches):
            last = idx == len(branches) - 1
            res = _attn_branch(q, k, v, rel_bias_table, dilation, prev if last else [], not last)
            if last:
                attn = res[0]
            else:
                prev.append((res[0], res[1]))

        four = _fourier(pq, fourier_b[layer].reshape(-1))
        x2d = _outproj(attn, four, x2d, row(attn_out_gain[layer]), row(fourier_out_gain[layer]),
                       w_out[layer].astype(BF16), tm)
        last_layer = layer == depth - 1
        gfin = row(final_norm_gain) if last_layer else None
        assert last_layer, "final norm is fused into the last layer's FFN"
        x2d = _ffn(x2d, row(norm_ffn_gain[layer]), w_gate[layer].astype(BF16), w_val[layer].astype(BF16),
                   conv_w[layer].astype(F32), row(conv_b[layer]), w_down[layer].astype(BF16), gfin,
                   seq, tm, 256)
    return x2d.reshape(batch, seq, d_model)
```

```python
import functools
import math

import numpy as np
import jax
import jax.numpy as jnp
from jax import lax
from jax.experimental import pallas as pl
from jax.experimental.pallas import tpu as pltpu

EPS = 1e-6
NEG_INF = -1e30
HEAD_DIM = 64
DILATED_PATTERNS = ((128, 1), (512, 4), (2048, 16))
N_REL_BUCKETS = 32
REL_MAX_DISTANCE = 1024

LANES = 128
HALF_WINDOW = 64
TQ = 2 * HALF_WINDOW
TK = 4 * HALF_WINDOW
BLOCKS_IN_FLIGHT = 8
FFT_RADIX = 64
HALO = 8
VMEM_LIMIT = 56 * 1024 * 1024

BF16 = jnp.bfloat16
F32 = jnp.float32


def _cparams(n_axes):
    return pltpu.CompilerParams(dimension_semantics=("arbitrary",) * n_axes,
                                vmem_limit_bytes=VMEM_LIMIT)


def _rms(x, gain):
    ms = jnp.mean(x * x, axis=-1, keepdims=True)
    return x * lax.rsqrt(ms + EPS) * gain


def _prep_kernel(wu_ref, fw_ref, cc_ref, sc_ref, wp_ref, wq_ref):
    hi = lax.Precision.HIGHEST
    fw = fw_ref[...]
    a = jnp.dot(cc_ref[...], fw, precision=hi, preferred_element_type=F32)
    b = jnp.dot(sc_ref[...], fw, precision=hi, preferred_element_type=F32)
    wu = wu_ref[...]
    wp_ref[...] = jnp.dot(wu, a, precision=hi, preferred_element_type=F32)
    wq_ref[...] = jnp.dot(wu, b, precision=hi, preferred_element_type=F32)


def _fold_fourier_weights(w_u, fourier_w, seq):
    d_model = w_u.shape[0]
    groups, gd, _ = fourier_w.shape
    ang = 2.0 * np.pi * np.outer(np.arange(gd), np.arange(gd)) / gd
    scale = 1.0 / math.sqrt(seq * gd)
    cc = jnp.asarray(np.cos(ang) * scale, F32)
    sc = jnp.asarray(np.sin(ang) * scale, F32)
    wu_g = w_u.reshape(d_model, groups, gd).transpose(1, 0, 2)
    wp, wq = pl.pallas_call(
        _prep_kernel,
        grid=(groups,),
        in_specs=[pl.BlockSpec((None, d_model, gd), lambda g: (g, 0, 0)),
                  pl.BlockSpec((None, gd, gd), lambda g: (g, 0, 0)),
                  pl.BlockSpec((gd, gd), lambda g: (0, 0)),
                  pl.BlockSpec((gd, gd), lambda g: (0, 0))],
        out_specs=[pl.BlockSpec((None, d_model, gd), lambda g: (g, 0, 0))] * 2,
        out_shape=[jax.ShapeDtypeStruct((groups, d_model, gd), F32)] * 2,
        compiler_params=_cparams(1),
        name="fourier_weight_fold",
    )(wu_g, fourier_w, cc, sc)
    unfold = lambda w: w.transpose(1, 0, 2).reshape(d_model, groups * gd)
    return unfold(wp), unfold(wq)


def _inproj_kernel(x_ref, g_ref, w_ref, q_ref, k_ref, v_ref, pq_ref, *, n_hp, fw):
    h = _rms(x_ref[...], g_ref[...]).astype(BF16)

    def proj(c0, n):
        return jnp.dot(h, w_ref[:, c0:c0 + n], preferred_element_type=F32)

    aw = n_hp * LANES
    for t, ref in enumerate((q_ref, k_ref, v_ref)):
        for c in range(n_hp // 2):
            res = proj(t * aw + c * 2 * LANES, 2 * LANES)
            ref[2 * c] = res[:, :LANES].astype(BF16)
            ref[2 * c + 1] = res[:, LANES:].astype(BF16)
    for t in range(2):
        pq_ref[t] = proj(3 * aw + t * fw, fw).astype(BF16)


def _inproj(x2d, gain, w_all, batch, seq, n_hp, fw, tm):
    n_tok, d_model = x2d.shape
    nt = seq // tm
    tok_map = lambda i: (i // nt, 0, i % nt, 0)
    hp_shape = jax.ShapeDtypeStruct((batch, n_hp, seq, LANES), BF16)
    return pl.pallas_call(
        functools.partial(_inproj_kernel, n_hp=n_hp, fw=fw),
        grid=(n_tok // tm,),
        in_specs=[pl.BlockSpec((tm, d_model), lambda i: (i, 0)),
                  pl.BlockSpec((1, d_model), lambda i: (0, 0)),
                  pl.BlockSpec(w_all.shape, lambda i: (0, 0))],
        out_specs=[pl.BlockSpec((None, n_hp, tm, LANES), tok_map)] * 3
        + [pl.BlockSpec((None, 2, tm, fw), tok_map)],
        out_shape=[hp_shape] * 3 + [jax.ShapeDtypeStruct((batch, 2, seq, fw), BF16)],
        compiler_params=_cparams(1),
        name="rmsnorm_inproj",
    )(x2d, gain, w_all)


def _t5_bucket_static(rel, dtype):
    nb = N_REL_BUCKETS // 2
    max_exact = nb // 2
    n = np.abs(rel)
    nf = np.maximum(n, 1).astype(dtype)
    large = max_exact + (np.log(nf / dtype(max_exact)) / dtype(math.log(REL_MAX_DISTANCE / max_exact))
                         * dtype(nb - max_exact)).astype(np.int32)
    large = np.minimum(large, nb - 1)
    return np.where(rel > 0, nb, 0) + np.where(n < max_exact, n, large)


def _bucket_tiles(dilations):
    qi = np.arange(TQ)[:, None]
    kc = np.arange(TK)[None, :]
    offsets = np.array([0, HALF_WINDOW, 2 * HALF_WINDOW])
    rel = kc[None] - offsets[:, None, None] - qi[None]
    tiles = []
    for d in dilations:
        bkt = _t5_bucket_static(rel * d, np.float32)
        assert np.array_equal(bkt, _t5_bucket_static(rel * d, np.float64))
        tiles.append(np.where(np.abs(rel) <= HALF_WINDOW, bkt, -1))
    return np.concatenate(tiles, axis=0).astype(np.int32)


def _bias_kernel(table_ref, bkt_ref, out_ref, *, n_heads):
    bkt = bkt_ref[...]
    for h in range(n_heads):
        acc = jnp.full(bkt.shape, NEG_INF, F32)
        for b in range(N_REL_BUCKETS):
            acc = jnp.where(bkt == b, table_ref[b, h], acc)
        out_ref[h] = acc


def _bias_tiles(rel_table, dilations):
    n_heads = rel_table.shape[1]
    bkt = jnp.asarray(_bucket_tiles(dilations))
    return pl.pallas_call(
        functools.partial(_bias_kernel, n_heads=n_heads),
        grid=(bkt.shape[0],),
        in_specs=[pl.BlockSpec(memory_space=pltpu.SMEM),
                  pl.BlockSpec((None, TQ, TK), lambda t: (t, 0, 0))],
        out_specs=pl.BlockSpec((None, n_heads, TQ, TK), lambda t: (t, 0, 0, 0)),
        out_shape=jax.ShapeDtypeStruct((bkt.shape[0], n_heads, TQ, TK), F32),
        compiler_params=_cparams(1),
        name="rel_bias_tiles",
    )(rel_table.astype(F32), bkt)


def _attn_kernel(*refs, dilation, sub_len, n_prev, write_lse):
    q_ref, k_ref, v_ref, bias_ref = refs[:4]
    prev = refs[4:4 + 2 * n_prev]
    o_ref = refs[4 + 2 * n_prev]
    lse_ref = refs[5 + 2 * n_prev] if write_lse else None
    n_blk = sub_len // TQ
    lane = lax.broadcasted_iota(jnp.int32, (TQ, LANES), 1)
    first_head = lane < HEAD_DIM
    nt_dims = (((1,), (1,)), ((), ()))

    def block(n, carry):
        q_row = pl.multiple_of(n * TQ, TQ)
        k_row = pl.multiple_of(jnp.clip(n * TQ - HALF_WINDOW, 0, sub_len - TK), HALF_WINDOW)
        edge = (n > 0).astype(jnp.int32) + (n == n_blk - 1).astype(jnp.int32)
        for r in range(dilation):
            cols = slice(r * LANES, (r + 1) * LANES)
            qb = q_ref[pl.ds(q_row, TQ), cols]
            kb = k_ref[pl.ds(k_row, TK), cols]
            vb = v_ref[pl.ds(k_row, TK), cols]
            zero = jnp.zeros_like(qb)
            outs, lses = [], []
            for hd in range(2):
                qh = jnp.where(first_head if hd == 0 else ~first_head, qb, zero)
                s = lax.dot_general(qh, kb, nt_dims, preferred_element_type=F32) + bias_ref[edge, hd]
                m = jnp.max(s, axis=-1, keepdims=True)
                p = jnp.exp(s - m)
                l = jnp.sum(p, axis=-1, keepdims=True)
                pv = jnp.dot(p.astype(BF16), vb, preferred_element_type=F32)
                outs.append(pv * (1.0 / l))
                lses.append(m + jnp.log(l))
            o = jnp.where(first_head, outs[0], outs[1])
            lse = jnp.where(first_head, lses[0], lses[1])
            if n_prev:
                os_, ls_ = [o], [lse]
                for j in range(n_prev):
                    os_.append(prev[2 * j][pl.ds(q_row, TQ), cols].astype(F32))
                    ls_.append(prev[2 * j + 1][pl.ds(q_row, TQ), cols])
                top = functools.reduce(jnp.maximum, ls_)
                ws = [jnp.exp(x - top) for x in ls_]
                den = functools.reduce(lambda a, b: a + b, ws)
                num = functools.reduce(lambda a, b: a + b, [w * x for w, x in zip(ws, os_)])
                o = num * (1.0 / den)
            o_ref[pl.ds(q_row, TQ), cols] = o.astype(BF16)
            if write_lse:
                lse_ref[pl.ds(q_row, TQ), cols] = lse
        return carry

    lax.fori_loop(0, n_blk, block, 0, unroll=max(1, BLOCKS_IN_FLIGHT // dilation))


def _attn_branch(q, k, v, bias, branch, dilation, prev, write_lse):
    batch, n_hp, seq, _ = q.shape
    sub_len = seq // dilation
    width = dilation * LANES
    view = lambda a: a.reshape(batch, n_hp, sub_len, width)
    slab = pl.BlockSpec((None, None, sub_len, width), lambda b, h: (b, h, 0, 0))
    args = [view(q), view(k), view(v), bias]
    in_specs = [slab, slab, slab, pl.BlockSpec((3, 2, TQ, TK), lambda b, h: (branch, h, 0, 0))]
    for o_p, lse_p in prev:
        args += [view(o_p), view(lse_p)]
        in_specs += [slab, slab]
    out_shape = [jax.ShapeDtypeStruct((batch, n_hp, sub_len, width), BF16)]
    if write_lse:
        out_shape.append(jax.ShapeDtypeStruct((batch, n_hp, sub_len, width), F32))
    res = pl.pallas_call(
        functools.partial(_attn_kernel, dilation=dilation, sub_len=sub_len,
                          n_prev=len(prev), write_lse=write_lse),
        grid=(batch, n_hp),
        in_specs=in_specs,
        out_specs=[slab] * len(out_shape),
        out_shape=out_shape,
        compiler_params=_cparams(2),
        name=f"dilated_attn_d{dilation}",
    )(*args)
    return [a.reshape(batch, n_hp, seq, LANES) for a in res]


def _fft_stage1_kernel(w_ref, v_ref, z_ref, *, chunk):
    w = w_ref[...]
    for c0 in range(0, v_ref.shape[-1], chunk):
        z = jnp.dot(w, v_ref[:, c0:c0 + chunk], preferred_element_type=F32)
        z_ref[:, c0:c0 + chunk] = z.astype(BF16)


def _fft_stage2_kernel(m_ref, z_ref, b_ref, o_ref, *, group, fw):
    for j in range(group):
        zc = jnp.concatenate([z_ref[0, j], z_ref[1, j]], axis=0)
        x = jnp.dot(m_ref[j], zc, preferred_element_type=F32) + b_ref[...]
        o_ref[:, j * fw:(j + 1) * fw] = x.astype(BF16)


def _fourier(pq, fourier_b):
    batch, _, seq, fw = pq.shape
    R = FFT_RADIX
    assert seq == R * R
    i = np.arange(R)
    ang1 = 2.0 * np.pi * np.outer(i, i) / R
    c1, s1 = np.cos(ang1), np.sin(ang1)
    w_cat = jnp.asarray(np.block([[c1, -s1], [-s1, -c1]]), BF16)
    k_all = i[:, None, None] + R * i[None, :, None]
    ang2 = 2.0 * np.pi * ((k_all * i[None, None, :]) % seq) / seq
    m_cat = jnp.asarray(np.concatenate([np.cos(ang2), np.sin(ang2)], axis=-1), BF16)

    v = pq.reshape(batch, 2 * R, R * fw)
    z = pl.pallas_call(
        functools.partial(_fft_stage1_kernel, chunk=2048),
        grid=(batch,),
        in_specs=[pl.BlockSpec((2 * R, 2 * R), lambda b: (0, 0)),
                  pl.BlockSpec((None, 2 * R, R * fw), lambda b: (b, 0, 0))],
        out_specs=pl.BlockSpec((None, 2 * R, R * fw), lambda b: (b, 0, 0)),
        out_shape=jax.ShapeDtypeStruct((batch, 2 * R, R * fw), BF16),
        compiler_params=_cparams(1),
        name="fft_stage1",
    )(w_cat, v)

    group = 16
    z5 = z.reshape(batch, 2, R, R, fw)
    out = pl.pallas_call(
        functools.partial(_fft_stage2_kernel, group=group, fw=fw),
        grid=(batch, R // group),
        in_specs=[pl.BlockSpec((group, R, 2 * R), lambda b, g: (g, 0, 0)),
                  pl.BlockSpec((None, 2, group, R, fw), lambda b, g: (b, 0, g, 0, 0)),
                  pl.BlockSpec((1, fw), lambda b, g: (0, 0))],
        out_specs=pl.BlockSpec((None, R, group * fw), lambda b, g: (b, 0, g)),
        out_shape=jax.ShapeDtypeStruct((batch, R, R * fw), BF16),
        compiler_params=_cparams(2),
        name="fft_stage2",
    )(m_cat, z5, fourier_b.reshape(1, fw).astype(F32))
    return out.reshape(batch, seq, fw)


def _outproj_kernel(a_ref, f_ref, x_ref, ga_ref, gf_ref, w_ref, o_ref, *, n_hp):
    parts = [a_ref[j].astype(F32) for j in range(n_hp)]
    ssq = functools.reduce(lambda a, b: a + b,
                           [jnp.sum(p * p, axis=-1, keepdims=True) for p in parts])
    inv = lax.rsqrt(ssq / (n_hp * LANES) + EPS)
    ga = ga_ref[...]
    cols = [(p * inv * ga[:, j * LANES:(j + 1) * LANES]).astype(BF16) for j, p in enumerate(parts)]
    cols.append(_rms(f_ref[...].astype(F32), gf_ref[...]).astype(BF16))
    mixed = jnp.concatenate(cols, axis=1)
    o_ref[...] = x_ref[...] + jnp.dot(mixed, w_ref[...], preferred_element_type=F32)


def _outproj(attn, four, x2d, ga, gf, w_out, tm):
    batch, n_hp, seq, _ = attn.shape
    fw = four.shape[-1]
    n_tok, d_model = x2d.shape
    nt = seq // tm
    return pl.pallas_call(
        functools.partial(_outproj_kernel, n_hp=n_hp),
        grid=(n_tok // tm,),
        in_specs=[pl.BlockSpec((None, n_hp, tm, LANES), lambda i: (i // nt, 0, i % nt, 0)),
                  pl.BlockSpec((None, tm, fw), lambda i: (i // nt, i % nt, 0)),
                  pl.BlockSpec((tm, d_model), lambda i: (i, 0)),
                  pl.BlockSpec((1, n_hp * LANES), lambda i: (0, 0)),
                  pl.BlockSpec((1, fw), lambda i: (0, 0)),
                  pl.BlockSpec(w_out.shape, lambda i: (0, 0))],
        out_specs=pl.BlockSpec((tm, d_model), lambda i: (i, 0)),
        out_shape=jax.ShapeDtypeStruct((n_tok, d_model), F32),
        compiler_params=_cparams(1),
        name="mix_outproj",
    )(attn, four, x2d, ga, gf, w_out)


def _ffn_kernel(xp_ref, x_ref, xn_ref, g_ref, wg_ref, wv_ref, cw_ref, cb_ref, wd_ref, gfin_ref,
                o_ref, acc_ref, *, tiles_per_seq, fc):
    i = pl.program_id(0)
    tm = x_ref.shape[0]
    gain = g_ref[...]
    x = x_ref[...]
    keep_prev = (i % tiles_per_seq != 0).astype(F32)
    keep_next = (i % tiles_per_seq != tiles_per_seq - 1).astype(F32)
    x_ext = jnp.concatenate([xp_ref[...], x, xn_ref[...]], axis=0)
    h_ext = _rms(x_ext, gain)
    row = lax.broadcasted_iota(jnp.int32, (tm + 2 * HALO, 1), 0)
    keep = jnp.where(row < HALO, keep_prev, jnp.where(row >= HALO + tm, keep_next, 1.0))
    h_ext = (h_ext * keep).astype(BF16)
    h = _rms(x, gain).astype(BF16)
    n_ext = tm + 2 * HALO
    d_ff = wg_ref.shape[1]
    for c0 in range(0, d_ff, fc):
        cs = slice(c0, c0 + fc)
        g = jnp.dot(h_ext, wg_ref[:, cs], preferred_element_type=F32)
        g_prev = pltpu.roll(g, 1, axis=0)[HALO:HALO + tm]
        g_next = pltpu.roll(g, n_ext - 1, axis=0)[HALO:HALO + tm]
        cw = cw_ref[:, cs]
        conv = cw[0:1] * g_prev + cw[1:2] * g[HALO:HALO + tm] + cw[2:3] * g_next + cb_ref[:, cs]
        val = jnp.dot(h, wv_ref[:, cs], preferred_element_type=F32)
        act = (conv * (1.0 / (1.0 + jnp.exp(-conv))) * val).astype(BF16)
        down = jnp.dot(act, wd_ref[cs, :], preferred_element_type=F32)
        if c0 == 0:
            acc_ref[...] = down
        else:
            acc_ref[...] += down
    o_ref[...] = _rms(x + acc_ref[...], gfin_ref[...])


def _ffn(x1, gain, wg, wv, conv_w, conv_b, wd, gfin, seq, tm, fc):
    n_tok, d_model = x1.shape
    d_ff = wg.shape[1]
    blocks_per_tile = tm // HALO
    n_halo_blocks = n_tok // HALO
    const = lambda shape: pl.BlockSpec(shape, lambda i: (0,) * len(shape))
    return pl.pallas_call(
        functools.partial(_ffn_kernel, tiles_per_seq=seq // tm, fc=fc),
        grid=(n_tok // tm,),
        in_specs=[pl.BlockSpec((HALO, d_model), lambda i: (jnp.maximum(i * blocks_per_tile - 1, 0), 0)),
                  pl.BlockSpec((tm, d_model), lambda i: (i, 0)),
                  pl.BlockSpec((HALO, d_model),
                               lambda i: (jnp.minimum((i + 1) * blocks_per_tile, n_halo_blocks - 1), 0)),
                  const((1, d_model)), const(wg.shape), const(wv.shape),
                  const(conv_w.shape), const((1, d_ff)), const(wd.shape), const((1, d_model))],
        out_specs=pl.BlockSpec((tm, d_model), lambda i: (i, 0)),
        out_shape=jax.ShapeDtypeStruct((n_tok, d_model), F32),
        scratch_shapes=[pltpu.VMEM((tm, d_model), F32)],
        compiler_params=_cparams(1),
        name="convglu_ffn",
    )(x1, x1, x1, gain, wg, wv, conv_w, conv_b, wd, gfin)


def kernel(x, norm_mix_gain, w_in, attn_out_gain, rel_bias_table, fourier_w, fourier_b, fourier_out_gain, w_out, norm_ffn_gain, w_gate, w_val, conv_w, conv_b, w_down, final_norm_gain):
    batch, seq, d_model = x.shape
    depth = w_in.shape[0]
    n_heads = rel_bias_table.shape[1]
    attn_w = n_heads * HEAD_DIM
    n_hp = attn_w // LANES
    fw = fourier_w.shape[1] * fourier_w.shape[2]
    d_ff = w_gate.shape[-1]
    assert all(w // (2 * d) == HALF_WINDOW for w, d in DILATED_PATTERNS)
    tm = 512
    row = lambda g: g.reshape(1, -1).astype(F32)

    x2d = x.reshape(batch * seq, d_model)
    for layer in range(depth):
        w_l = w_in[layer]
        wp, wq = _fold_fourier_weights(w_l[:, 3 * attn_w:], fourier_w[layer], seq)
        w_all = jnp.concatenate([w_l[:, :attn_w] * (HEAD_DIM ** -0.5), w_l[:, attn_w:3 * attn_w], wp, wq],
                                axis=1).astype(BF16)
        q, k, v, pq = _inproj(x2d, row(norm_mix_gain[layer]), w_all, batch, seq, n_hp, fw, tm)

        branches = sorted(DILATED_PATTERNS, key=lambda wd: -wd[1])
        bias = _bias_tiles(rel_bias_table, [d for _, d in branches])
        prev = []
        for idx, (_, dilation) in enumerate(branches):
            last = idx == len(branches) - 1
            res = _attn_branch(q, k, v, bias, idx, dilation, prev if last else [], not last)
            if last:
                attn = res[0]
            else:
                prev.append((res[0], res[1]))

        four = _fourier(pq, fourier_b[layer].reshape(-1))
        x2d = _outproj(attn, four, x2d, row(attn_out_gain[layer]), row(fourier_out_gain[layer]),
                       w_out[layer].astype(BF16), tm)
        last_layer = layer == depth - 1
        gfin = row(final_norm_gain) if last_layer else None
        assert last_layer, "final norm is fused into the last layer's FFN"
        x2d = _ffn(x2d, row(norm_ffn_gain[layer]), w_gate[layer].astype(BF16), w_val[layer].astype(BF16),
                   conv_w[layer].astype(F32), row(conv_b[layer]), w_down[layer].astype(BF16), gfin,
                   seq, tm, 256)
    return x2d.reshape(batch, seq, d_model)
```

```python
import functools
import math

import numpy as np
import jax
import jax.numpy as jnp
from jax import lax
from jax.experimental import pallas as pl
from jax.experimental.pallas import tpu as pltpu

EPS = 1e-6
NEG_INF = -1e30
HEAD_DIM = 64
DILATED_PATTERNS = ((128, 1), (512, 4), (2048, 16))
N_REL_BUCKETS = 32
REL_MAX_DISTANCE = 1024

LANES = 128
BF16_ROWS = 16
HALF_WINDOW = 64
TQ = 2 * HALF_WINDOW
TK = 4 * HALF_WINDOW
MAX_DIL = max(d for _, d in DILATED_PATTERNS)
GROUP = MAX_DIL * BF16_ROWS
BLOCKS_IN_FLIGHT = 8
FFT_RADIX = 64
HALO = 8
VMEM_LIMIT = 56 * 1024 * 1024

BF16 = jnp.bfloat16
F32 = jnp.float32


def _cparams(n_axes):
    return pltpu.CompilerParams(dimension_semantics=("arbitrary",) * n_axes,
                                vmem_limit_bytes=VMEM_LIMIT)


def _rms(x, gain):
    ms = jnp.mean(x * x, axis=-1, keepdims=True)
    return x * lax.rsqrt(ms + EPS) * gain


def _group_permutation():
    t = np.arange(GROUP)
    swapped = (t % MAX_DIL) * BF16_ROWS + t // MAX_DIL
    perm = np.zeros((GROUP, GROUP), np.float32)
    perm[swapped, t] = 1.0
    assert MAX_DIL == BF16_ROWS and np.array_equal(perm, perm.T)
    return jnp.asarray(perm, BF16)


def _prep_kernel(wu_ref, fw_ref, cc_ref, sc_ref, wp_ref, wq_ref):
    hi = lax.Precision.HIGHEST
    fw = fw_ref[...]
    a = jnp.dot(cc_ref[...], fw, precision=hi, preferred_element_type=F32)
    b = jnp.dot(sc_ref[...], fw, precision=hi, preferred_element_type=F32)
    wu = wu_ref[...]
    wp_ref[...] = jnp.dot(wu, a, precision=hi, preferred_element_type=F32)
    wq_ref[...] = jnp.dot(wu, b, precision=hi, preferred_element_type=F32)


def _fold_fourier_weights(w_u, fourier_w, seq):
    d_model = w_u.shape[0]
    groups, gd, _ = fourier_w.shape
    ang = 2.0 * np.pi * np.outer(np.arange(gd), np.arange(gd)) / gd
    scale = 1.0 / math.sqrt(seq * gd)
    cc = jnp.asarray(np.cos(ang) * scale, F32)
    sc = jnp.asarray(np.sin(ang) * scale, F32)
    wu_g = w_u.reshape(d_model, groups, gd).transpose(1, 0, 2)
    wp, wq = pl.pallas_call(
        _prep_kernel,
        grid=(groups,),
        in_specs=[pl.BlockSpec((None, d_model, gd), lambda g: (g, 0, 0)),
                  pl.BlockSpec((None, gd, gd), lambda g: (g, 0, 0)),
                  pl.BlockSpec((gd, gd), lambda g: (0, 0)),
                  pl.BlockSpec((gd, gd), lambda g: (0, 0))],
        out_specs=[pl.BlockSpec((None, d_model, gd), lambda g: (g, 0, 0))] * 2,
        out_shape=[jax.ShapeDtypeStruct((groups, d_model, gd), F32)] * 2,
        compiler_params=_cparams(1),
        name="fourier_weight_fold",
    )(wu_g, fourier_w, cc, sc)
    unfold = lambda w: w.transpose(1, 0, 2).reshape(d_model, groups * gd)
    return unfold(wp), unfold(wq)


def _inproj_kernel(x_ref, g_ref, w_ref, perm_ref, q_ref, k_ref, v_ref, qr_ref, kr_ref, vr_ref, pq_ref,
                   *, n_hp, fw):
    h = _rms(x_ref[...], g_ref[...]).astype(BF16)
    tm = h.shape[0]
    perm = perm_ref[...]

    def proj(c0, n):
        return jnp.dot(h, w_ref[:, c0:c0 + n], preferred_element_type=F32)

    aw = n_hp * LANES
    for t, (tok_ref, res_ref) in enumerate(((q_ref, qr_ref), (k_ref, kr_ref), (v_ref, vr_ref))):
        for c in range(n_hp // 2):
            res = proj(t * aw + c * 2 * LANES, 2 * LANES).astype(BF16)
            tok_ref[2 * c] = res[:, :LANES]
            tok_ref[2 * c + 1] = res[:, LANES:]
            for g in range(tm // GROUP):
                rows = jnp.dot(perm, res[g * GROUP:(g + 1) * GROUP], preferred_element_type=F32).astype(BF16)
                for half in range(2):
                    blk = rows[:, half * LANES:(half + 1) * LANES].reshape(MAX_DIL, BF16_ROWS, LANES)
                    res_ref[2 * c + half, :, g * BF16_ROWS:(g + 1) * BF16_ROWS, :] = blk
    for t in range(2):
        pq_ref[t] = proj(3 * aw + t * fw, fw).astype(BF16)


def _inproj(x2d, gain, w_all, perm, batch, seq, n_hp, fw, tm):
    n_tok, d_model = x2d.shape
    nt = seq // tm
    tok_map = lambda i: (i // nt, 0, i % nt, 0)
    res_map = lambda i: (i // nt, 0, 0, i % nt, 0)
    tok_shape = jax.ShapeDtypeStruct((batch, n_hp, seq, LANES), BF16)
    res_shape = jax.ShapeDtypeStruct((batch, n_hp, MAX_DIL, seq // MAX_DIL, LANES), BF16)
    return pl.pallas_call(
        functools.partial(_inproj_kernel, n_hp=n_hp, fw=fw),
        grid=(n_tok // tm,),
        in_specs=[pl.BlockSpec((tm, d_model), lambda i: (i, 0)),
                  pl.BlockSpec((1, d_model), lambda i: (0, 0)),
                  pl.BlockSpec(w_all.shape, lambda i: (0, 0)),
                  pl.BlockSpec(perm.shape, lambda i: (0, 0))],
        out_specs=[pl.BlockSpec((None, n_hp, tm, LANES), tok_map)] * 3
        + [pl.BlockSpec((None, n_hp, MAX_DIL, tm // MAX_DIL, LANES), res_map)] * 3
        + [pl.BlockSpec((None, 2, tm, fw), tok_map)],
        out_shape=[tok_shape] * 3 + [res_shape] * 3 + [jax.ShapeDtypeStruct((batch, 2, seq, fw), BF16)],
        compiler_params=_cparams(1),
        name="rmsnorm_inproj",
    )(x2d, gain, w_all, perm)


def _t5_bucket_static(rel, dtype):
    nb = N_REL_BUCKETS // 2
    max_exact = nb // 2
    n = np.abs(rel)
    nf = np.maximum(n, 1).astype(dtype)
    large = max_exact + (np.log(nf / dtype(max_exact)) / dtype(math.log(REL_MAX_DISTANCE / max_exact))
                         * dtype(nb - max_exact)).astype(np.int32)
    large = np.minimum(large, nb - 1)
    return np.where(rel > 0, nb, 0) + np.where(n < max_exact, n, large)


def _bucket_tiles(dilations):
    qi = np.arange(TQ)[:, None]
    kc = np.arange(TK)[None, :]
    offsets = np.array([0, HALF_WINDOW, 2 * HALF_WINDOW])
    rel = kc[None] - offsets[:, None, None] - qi[None]
    tiles = []
    for d in dilations:
        bkt = _t5_bucket_static(rel * d, np.float32)
        assert np.array_equal(bkt, _t5_bucket_static(rel * d, np.float64))
        tile = np.where(np.abs(rel) <= HALF_WINDOW, bkt, -1)
        pieces = MAX_DIL // d if 1 < d < MAX_DIL else 1
        q_order = (np.arange(TQ) % (TQ // pieces)) * pieces + np.arange(TQ) // (TQ // pieces)
        k_order = (np.arange(TK) % (TK // pieces)) * pieces + np.arange(TK) // (TK // pieces)
        tiles.append(tile[:, q_order][:, :, k_order])
    return np.concatenate(tiles, axis=0).astype(np.int32)


def _bias_kernel(table_ref, bkt_ref, out_ref, *, n_heads):
    bkt = bkt_ref[...]
    for h in range(n_heads):
        acc = jnp.full(bkt.shape, NEG_INF, F32)
        for b in range(N_REL_BUCKETS):
            acc = jnp.where(bkt == b, table_ref[b, h], acc)
        out_ref[h] = acc


def _bias_tiles(rel_table, dilations):
    n_heads = rel_table.shape[1]
    bkt = jnp.asarray(_bucket_tiles(dilations))
    return pl.pallas_call(
        functools.partial(_bias_kernel, n_heads=n_heads),
        grid=(bkt.shape[0],),
        in_specs=[pl.BlockSpec(memory_space=pltpu.SMEM),
                  pl.BlockSpec((None, TQ, TK), lambda t: (t, 0, 0))],
        out_specs=pl.BlockSpec((None, n_heads, TQ, TK), lambda t: (t, 0, 0, 0)),
        out_shape=jax.ShapeDtypeStruct((bkt.shape[0], n_heads, TQ, TK), F32),
        compiler_params=_cparams(1),
        name="rel_bias_tiles",
    )(rel_table.astype(F32), bkt)


def _block_attention(qb, kb, vb, bias0, bias1, first_head):
    nt_dims = (((1,), (1,)), ((), ()))
    zero = jnp.zeros_like(qb)
    outs, lses = [], []
    for keep, bias in ((first_head, bias0), (~first_head, bias1)):
        qh = jnp.where(keep, qb, zero)
        s = lax.dot_general(qh, kb, nt_dims, preferred_element_type=F32) + bias
        m = jnp.max(s, axis=-1, keepdims=True)
        p = jnp.exp(s - m)
        l = jnp.sum(p, axis=-1, keepdims=True)
        pv = jnp.dot(p.astype(BF16), vb, preferred_element_type=F32)
        outs.append(pv * (1.0 / l))
        lses.append(m + jnp.log(l))
    return jnp.where(first_head, outs[0], outs[1]), jnp.where(first_head, lses[0], lses[1])


def _merge(o_a, lse_a, o_b, lse_b):
    top = jnp.maximum(lse_a, lse_b)
    w_a = jnp.exp(lse_a - top)
    w_b = jnp.exp(lse_b - top)
    den = w_a + w_b
    return (w_a * o_a + w_b * o_b) * (1.0 / den), top + jnp.log(den)


def _split_f32(x):
    hi = x.astype(BF16)
    return hi, (x - hi.astype(F32)).astype(BF16)


def _attn_kernel(*refs, dilation, merge_in, split_out):
    q_ref, k_ref, v_ref, bias_ref = refs[:4]
    pos = 4
    if merge_in is not None:
        po_ref, pl_ref = refs[pos:pos + 2]
        pos += 2
    o_ref = refs[pos]
    lse_refs = refs[pos + 1:]
    pieces = MAX_DIL // dilation if dilation > 1 else 1
    sub_len = q_ref.shape[-2] * pieces
    n_blk = sub_len // TQ
    tq_p, tk_p = TQ // pieces, TK // pieces
    first_head = lax.broadcasted_iota(jnp.int32, (TQ, LANES), 1) < HEAD_DIM

    def load(ref, r, row, n_rows):
        if dilation == 1:
            return ref[pl.ds(row, n_rows), :]
        parts = [ref[a * dilation + r, pl.ds(row, n_rows), :] for a in range(pieces)]
        return parts[0] if pieces == 1 else jnp.concatenate(parts, axis=0)

    def store(ref, r, row, n_rows, val):
        if dilation == 1:
            ref[pl.ds(row, n_rows), :] = val
        else:
            for a in range(pieces):
                ref[a * dilation + r, pl.ds(row, n_rows), :] = val[a * n_rows:(a + 1) * n_rows]

    def block(n, carry):
        q_row = pl.multiple_of(n * tq_p, tq_p)
        k_row = pl.multiple_of(jnp.clip(n * tq_p - tk_p // 4, 0, sub_len // pieces - tk_p), tk_p // 4)
        edge = jnp.where(n > 0, 1, 0) + jnp.where(n == n_blk - 1, 1, 0)
        for r in range(dilation):
            qb = load(q_ref, r, q_row, tq_p)
            kb = load(k_ref, r, k_row, tk_p)
            vb = load(v_ref, r, k_row, tk_p)
            o, lse = _block_attention(qb, kb, vb, bias_ref[edge, 0], bias_ref[edge, 1], first_head)
            if merge_in is not None:
                o, lse = _merge(o, lse, load(po_ref, r, q_row, tq_p).astype(F32), load(pl_ref, r, q_row, tq_p))
            store(o_ref, r, q_row, tq_p, o.astype(BF16))
            if split_out:
                hi, lo = _split_f32(lse)
                store(lse_refs[0], r, q_row, tq_p, hi)
                store(lse_refs[1], r, q_row, tq_p, lo)
            elif lse_refs:
                store(lse_refs[0], r, q_row, tq_p, lse)
        return carry

    lax.fori_loop(0, n_blk, block, 0, unroll=max(1, BLOCKS_IN_FLIGHT // dilation))


def _attn_token_kernel(q_ref, k_ref, v_ref, bias_ref, perm_ref, po_ref, hi_ref, lo_ref, o_ref,
                       o_tok, lse_tok):
    perm = perm_ref[...]
    for g in range(po_ref.shape[1] // BF16_ROWS):
        rows = slice(g * BF16_ROWS, (g + 1) * BF16_ROWS)
        o_hi = jnp.concatenate([po_ref[:, rows, :].reshape(GROUP, LANES),
                                hi_ref[:, rows, :].reshape(GROUP, LANES)], axis=1)
        tok = jnp.dot(perm, o_hi, preferred_element_type=F32)
        lo = jnp.dot(perm, lo_ref[:, rows, :].reshape(GROUP, LANES), preferred_element_type=F32)
        o_tok[g * GROUP:(g + 1) * GROUP, :] = tok[:, :LANES].astype(BF16)
        lse_tok[g * GROUP:(g + 1) * GROUP, :] = tok[:, LANES:] + lo
    _attn_kernel(q_ref, k_ref, v_ref, bias_ref, o_tok, lse_tok, o_ref,
                 dilation=1, merge_in='token', split_out=False)


def _attention(q, k, v, qr, kr, vr, bias, perm, dilations):
    batch, n_hp, seq, _ = q.shape
    assert dilations[0] == MAX_DIL and dilations[-1] == 1
    res_slab = pl.BlockSpec((None, None, MAX_DIL, seq // MAX_DIL, LANES), lambda b, h: (b, h, 0, 0, 0))
    tok_slab = pl.BlockSpec((None, None, seq, LANES), lambda b, h: (b, h, 0, 0))
    res_shape = lambda dt: jax.ShapeDtypeStruct((batch, n_hp, MAX_DIL, seq // MAX_DIL, LANES), dt)
    bias_spec = lambda i: pl.BlockSpec((3, 2, TQ, TK), lambda b, h: (i, h, 0, 0))

    merged = None
    for i, d in enumerate(dilations[:-1]):
        final_res = i == len(dilations) - 2
        args = [qr, kr, vr, bias] + (list(merged) if merged else [])
        in_specs = [res_slab] * 3 + [bias_spec(i)] + ([res_slab] * 2 if merged else [])
        out_shape = [res_shape(BF16)] + ([res_shape(BF16)] * 2 if final_res else [res_shape(F32)])
        merged = pl.pallas_call(
            functools.partial(_attn_kernel, dilation=d, merge_in='f32' if merged else None,
                              split_out=final_res),
            grid=(batch, n_hp),
            in_specs=in_specs,
            out_specs=[res_slab] * len(out_shape),
            out_shape=out_shape,
            compiler_params=_cparams(2),
            name=f"dilated_attn_d{d}",
        )(*args)

    return pl.pallas_call(
        _attn_token_kernel,
        grid=(batch, n_hp),
        in_specs=[tok_slab] * 3 + [bias_spec(len(dilations) - 1), pl.BlockSpec(perm.shape, lambda b, h: (0, 0))]
        + [res_slab] * 3,
        out_specs=tok_slab,
        out_shape=jax.ShapeDtypeStruct((batch, n_hp, seq, LANES), BF16),
        scratch_shapes=[pltpu.VMEM((seq, LANES), BF16), pltpu.VMEM((seq, LANES), F32)],
        compiler_params=_cparams(2),
        name="dilated_attn_d1",
    )(q, k, v, bias, perm, *merged)


def _fft_stage1_kernel(w_ref, v_ref, z_ref, *, chunk):
    w = w_ref[...]
    for c0 in range(0, v_ref.shape[-1], chunk):
        z = jnp.dot(w, v_ref[:, c0:c0 + chunk], preferred_element_type=F32)
        z_ref[:, c0:c0 + chunk] = z.astype(BF16)


def _fft_stage2_kernel(m_ref, z_ref, b_ref, o_ref, *, group, fw):
    for j in range(group):
        zc = jnp.concatenate([z_ref[0, j], z_ref[1, j]], axis=0)
        x = jnp.dot(m_ref[j], zc, preferred_element_type=F32) + b_ref[...]
        o_ref[:, j * fw:(j + 1) * fw] = x.astype(BF16)


def _fourier(pq, fourier_b):
    batch, _, seq, fw = pq.shape
    R = FFT_RADIX
    assert seq == R * R
    i = np.arange(R)
    ang1 = 2.0 * np.pi * np.outer(i, i) / R
    c1, s1 = np.cos(ang1), np.sin(ang1)
    w_cat = jnp.asarray(np.block([[c1, -s1], [-s1, -c1]]), BF16)
    k_all = i[:, None, None] + R * i[None, :, None]
    ang2 = 2.0 * np.pi * ((k_all * i[None, None, :]) % seq) / seq
    m_cat = jnp.asarray(np.concatenate([np.cos(ang2), np.sin(ang2)], axis=-1), BF16)

    v = pq.reshape(batch, 2 * R, R * fw)
    z = pl.pallas_call(
        functools.partial(_fft_stage1_kernel, chunk=2048),
        grid=(batch,),
        in_specs=[pl.BlockSpec((2 * R, 2 * R), lambda b: (0, 0)),
                  pl.BlockSpec((None, 2 * R, R * fw), lambda b: (b, 0, 0))],
        out_specs=pl.BlockSpec((None, 2 * R, R * fw), lambda b: (b, 0, 0)),
        out_shape=jax.ShapeDtypeStruct((batch, 2 * R, R * fw), BF16),
        compiler_params=_cparams(1),
        name="fft_stage1",
    )(w_cat, v)

    group = 16
    z5 = z.reshape(batch, 2, R, R, fw)
    out = pl.pallas_call(
        functools.partial(_fft_stage2_kernel, group=group, fw=fw),
        grid=(batch, R // group),
        in_specs=[pl.BlockSpec((group, R, 2 * R), lambda b, g: (g, 0, 0)),
                  pl.BlockSpec((None, 2, group, R, fw), lambda b, g: (b, 0, g, 0, 0)),
                  pl.BlockSpec((1, fw), lambda b, g: (0, 0))],
        out_specs=pl.BlockSpec((None, R, group * fw), lambda b, g: (b, 0, g)),
        out_shape=jax.ShapeDtypeStruct((batch, R, R * fw), BF16),
        compiler_params=_cparams(2),
        name="fft_stage2",
    )(m_cat, z5, fourier_b.reshape(1, fw).astype(F32))
    return out.reshape(batch, seq, fw)


def _outproj_kernel(a_ref, f_ref, x_ref, ga_ref, gf_ref, w_ref, o_ref, *, n_hp):
    parts = [a_ref[j].astype(F32) for j in range(n_hp)]
    ssq = functools.reduce(lambda a, b: a + b,
                           [jnp.sum(p * p, axis=-1, keepdims=True) for p in parts])
    inv = lax.rsqrt(ssq / (n_hp * LANES) + EPS)
    ga = ga_ref[...]
    cols = [(p * inv * ga[:, j * LANES:(j + 1) * LANES]).astype(BF16) for j, p in enumerate(parts)]
    cols.append(_rms(f_ref[...].astype(F32), gf_ref[...]).astype(BF16))
    mixed = jnp.concatenate(cols, axis=1)
    o_ref[...] = x_ref[...] + jnp.dot(mixed, w_ref[...], preferred_element_type=F32)


def _outproj(attn, four, x2d, ga, gf, w_out, tm):
    batch, n_hp, seq, _ = attn.shape
    fw = four.shape[-1]
    n_tok, d_model = x2d.shape
    nt = seq // tm
    return pl.pallas_call(
        functools.partial(_outproj_kernel, n_hp=n_hp),
        grid=(n_tok // tm,),
        in_specs=[pl.BlockSpec((None, n_hp, tm, LANES), lambda i: (i // nt, 0, i % nt, 0)),
                  pl.BlockSpec((None, tm, fw), lambda i: (i // nt, i % nt, 0)),
                  pl.BlockSpec((tm, d_model), lambda i: (i, 0)),
                  pl.BlockSpec((1, n_hp * LANES), lambda i: (0, 0)),
                  pl.BlockSpec((1, fw), lambda i: (0, 0)),
                  pl.BlockSpec(w_out.shape, lambda i: (0, 0))],
        out_specs=pl.BlockSpec((tm, d_model), lambda i: (i, 0)),
        out_shape=jax.ShapeDtypeStruct((n_tok, d_model), F32),
        compiler_params=_cparams(1),
        name="mix_outproj",
    )(attn, four, x2d, ga, gf, w_out)


def _ffn_kernel(xp_ref, x_ref, xn_ref, g_ref, wg_ref, wv_ref, cw_ref, cb_ref, wd_ref, gfin_ref,
                o_ref, acc_ref, *, tiles_per_seq, fc, final_norm):
    i = pl.program_id(0)
    tm = x_ref.shape[0]
    gain = g_ref[...]
    x = x_ref[...]
    keep_prev = (i % tiles_per_seq != 0).astype(F32)
    keep_next = (i % tiles_per_seq != tiles_per_seq - 1).astype(F32)
    x_ext = jnp.concatenate([xp_ref[...], x, xn_ref[...]], axis=0)
    h_ext = _rms(x_ext, gain)
    row = lax.broadcasted_iota(jnp.int32, (tm + 2 * HALO, 1), 0)
    keep = jnp.where(row < HALO, keep_prev, jnp.where(row >= HALO + tm, keep_next, 1.0))
    h_ext = (h_ext * keep).astype(BF16)
    h = _rms(x, gain).astype(BF16)
    n_ext = tm + 2 * HALO
    d_ff = wg_ref.shape[1]
    for c0 in range(0, d_ff, fc):
        cs = slice(c0, c0 + fc)
        g = jnp.dot(h_ext, wg_ref[:, cs], preferred_element_type=F32)
        g_prev = pltpu.roll(g, 1, axis=0)[HALO:HALO + tm]
        g_next = pltpu.roll(g, n_ext - 1, axis=0)[HALO:HALO + tm]
        cw = cw_ref[:, cs]
        conv = cw[0:1] * g_prev + cw[1:2] * g[HALO:HALO + tm] + cw[2:3] * g_next + cb_ref[:, cs]
        val = jnp.dot(h, wv_ref[:, cs], preferred_element_type=F32)
        act = (conv * (1.0 / (1.0 + jnp.exp(-conv))) * val).astype(BF16)
        down = jnp.dot(act, wd_ref[cs, :], preferred_element_type=F32)
        if c0 == 0:
            acc_ref[...] = down
        else:
            acc_ref[...] += down
    y = x + acc_ref[...]
    o_ref[...] = _rms(y, gfin_ref[...]) if final_norm else y


def _ffn(x1, gain, wg, wv, conv_w, conv_b, wd, gfin, seq, tm, fc, final_norm):
    n_tok, d_model = x1.shape
    d_ff = wg.shape[1]
    blocks_per_tile = tm // HALO
    n_halo_blocks = n_tok // HALO
    const = lambda shape: pl.BlockSpec(shape, lambda i: (0,) * len(shape))
    return pl.pallas_call(
        functools.partial(_ffn_kernel, tiles_per_seq=seq // tm, fc=fc, final_norm=final_norm),
        grid=(n_tok // tm,),
        in_specs=[pl.BlockSpec((HALO, d_model), lambda i: (jnp.maximum(i * blocks_per_tile - 1, 0), 0)),
                  pl.BlockSpec((tm, d_model), lambda i: (i, 0)),
                  pl.BlockSpec((HALO, d_model),
                               lambda i: (jnp.minimum((i + 1) * blocks_per_tile, n_halo_blocks - 1), 0)),
                  const((1, d_model)), const(wg.shape), const(wv.shape),
                  const(conv_w.shape), const((1, d_ff)), const(wd.shape), const((1, d_model))],
        out_specs=pl.BlockSpec((tm, d_model), lambda i: (i, 0)),
        out_shape=jax.ShapeDtypeStruct((n_tok, d_model), F32),
        scratch_shapes=[pltpu.VMEM((tm, d_model), F32)],
        compiler_params=_cparams(1),
        name="convglu_ffn",
    )(x1, x1, x1, gain, wg, wv, conv_w, conv_b, wd, gfin)


def kernel(x, norm_mix_gain, w_in, attn_out_gain, rel_bias_table, fourier_w, fourier_b, fourier_out_gain, w_out, norm_ffn_gain, w_gate, w_val, conv_w, conv_b, w_down, final_norm_gain):
    batch, seq, d_model = x.shape
    depth = w_in.shape[0]
    n_heads = rel_bias_table.shape[1]
    attn_w = n_heads * HEAD_DIM
    n_hp = attn_w // LANES
    fw = fourier_w.shape[1] * fourier_w.shape[2]
    assert all(w // (2 * d) == HALF_WINDOW for w, d in DILATED_PATTERNS)
    dilations = sorted((d for _, d in DILATED_PATTERNS), reverse=True)
    tm = 512
    row = lambda g: g.reshape(1, -1).astype(F32)
    perm = _group_permutation()
    bias = _bias_tiles(rel_bias_table, dilations)

    x2d = x.reshape(batch * seq, d_model)
    for layer in range(depth):
        w_l = w_in[layer]
        wp, wq = _fold_fourier_weights(w_l[:, 3 * attn_w:], fourier_w[layer], seq)
        w_all = jnp.concatenate([w_l[:, :attn_w] * (HEAD_DIM ** -0.5), w_l[:, attn_w:3 * attn_w], wp, wq],
                                axis=1).astype(BF16)
        q, k, v, qr, kr, vr, pq = _inproj(x2d, row(norm_mix_gain[layer]), w_all, perm, batch, seq, n_hp, fw, tm)
        attn = _attention(q, k, v, qr, kr, vr, bias, perm, dilations)
        four = _fourier(pq, fourier_b[layer].reshape(-1))
        x2d = _outproj(attn, four, x2d, row(attn_out_gain[layer]), row(fourier_out_gain[layer]),
                       w_out[layer].astype(BF16), tm)
        x2d = _ffn(x2d, row(norm_ffn_gain[layer]), w_gate[layer].astype(BF16), w_val[layer].astype(BF16),
                   conv_w[layer].astype(F32), row(conv_b[layer]), w_down[layer].astype(BF16),
                   row(final_norm_gain), seq, tm, 256, layer == depth - 1)
    return x2d.reshape(batch, seq, d_model)
```

```python
import functools
import math

import numpy as np
import jax
import jax.numpy as jnp
from jax import lax
from jax.experimental import pallas as pl
from jax.experimental.pallas import tpu as pltpu

EPS = 1e-6
NEG_INF = -1e30
LOG2E = math.log2(math.e)
LN2 = math.log(2.0)
HEAD_DIM = 64
DILATED_PATTERNS = ((128, 1), (512, 4), (2048, 16))
N_REL_BUCKETS = 32
REL_MAX_DISTANCE = 1024

LANES = 128
BF16_ROWS = 16
HALF_WINDOW = 64
TQ = 2 * HALF_WINDOW
TK = 4 * HALF_WINDOW
MAX_DIL = max(d for _, d in DILATED_PATTERNS)
GROUP = MAX_DIL * BF16_ROWS
BLOCKS_IN_FLIGHT = 8
FFT_RADIX = 64
HALO = 8
VMEM_LIMIT = 56 * 1024 * 1024

BF16 = jnp.bfloat16
F32 = jnp.float32


def _cparams(n_axes):
    return pltpu.CompilerParams(dimension_semantics=("arbitrary",) * n_axes,
                                vmem_limit_bytes=VMEM_LIMIT)


def _rms(x, gain):
    ms = jnp.mean(x * x, axis=-1, keepdims=True)
    return x * lax.rsqrt(ms + EPS) * gain


def _group_permutation():
    t = np.arange(GROUP)
    swapped = (t % MAX_DIL) * BF16_ROWS + t // MAX_DIL
    perm = np.zeros((GROUP, GROUP), np.float32)
    perm[swapped, t] = 1.0
    assert MAX_DIL == BF16_ROWS and np.array_equal(perm, perm.T)
    return jnp.asarray(perm, BF16)


def _prep_kernel(wu_ref, fw_ref, cc_ref, sc_ref, wp_ref, wq_ref):
    hi = lax.Precision.HIGHEST
    fw = fw_ref[...]
    a = jnp.dot(cc_ref[...], fw, precision=hi, preferred_element_type=F32)
    b = jnp.dot(sc_ref[...], fw, precision=hi, preferred_element_type=F32)
    wu = wu_ref[...]
    wp_ref[...] = jnp.dot(wu, a, precision=hi, preferred_element_type=F32)
    wq_ref[...] = jnp.dot(wu, b, precision=hi, preferred_element_type=F32)


def _fold_fourier_weights(w_u, fourier_w, seq):
    d_model = w_u.shape[0]
    groups, gd, _ = fourier_w.shape
    ang = 2.0 * np.pi * np.outer(np.arange(gd), np.arange(gd)) / gd
    scale = 1.0 / math.sqrt(seq * gd)
    cc = jnp.asarray(np.cos(ang) * scale, F32)
    sc = jnp.asarray(np.sin(ang) * scale, F32)
    wu_g = w_u.reshape(d_model, groups, gd).transpose(1, 0, 2)
    wp, wq = pl.pallas_call(
        _prep_kernel,
        grid=(groups,),
        in_specs=[pl.BlockSpec((None, d_model, gd), lambda g: (g, 0, 0)),
                  pl.BlockSpec((None, gd, gd), lambda g: (g, 0, 0)),
                  pl.BlockSpec((gd, gd), lambda g: (0, 0)),
                  pl.BlockSpec((gd, gd), lambda g: (0, 0))],
        out_specs=[pl.BlockSpec((None, d_model, gd), lambda g: (g, 0, 0))] * 2,
        out_shape=[jax.ShapeDtypeStruct((groups, d_model, gd), F32)] * 2,
        compiler_params=_cparams(1),
        name="fourier_weight_fold",
    )(wu_g, fourier_w, cc, sc)
    unfold = lambda w: w.transpose(1, 0, 2).reshape(d_model, groups * gd)
    return unfold(wp), unfold(wq)


def _inproj_kernel(x_ref, g_ref, w_ref, perm_ref, q_ref, k_ref, v_ref, qr_ref, kr_ref, vr_ref, pq_ref,
                   *, n_hp, fw):
    h = _rms(x_ref[...], g_ref[...]).astype(BF16)
    tm = h.shape[0]
    perm = perm_ref[...]

    def proj(c0, n):
        return jnp.dot(h, w_ref[:, c0:c0 + n], preferred_element_type=F32)

    aw = n_hp * LANES
    pairs = ((q_ref, qr_ref), (k_ref, kr_ref), (v_ref, vr_ref))
    for t, (tok_ref, _) in enumerate(pairs):
        for c in range(n_hp // 2):
            res = proj(t * aw + c * 2 * LANES, 2 * LANES).astype(BF16)
            tok_ref[2 * c] = res[:, :LANES]
            tok_ref[2 * c + 1] = res[:, LANES:]
    for t in range(2):
        pq_ref[t] = proj(3 * aw + t * fw, fw).astype(BF16)
    for tok_ref, res_ref in pairs:
        for c in range(n_hp // 2):
            for g in range(tm // GROUP):
                tok = jnp.concatenate([tok_ref[2 * c, g * GROUP:(g + 1) * GROUP, :],
                                       tok_ref[2 * c + 1, g * GROUP:(g + 1) * GROUP, :]], axis=1)
                rows = jnp.dot(perm, tok, preferred_element_type=F32).astype(BF16)
                for half in range(2):
                    blk = rows[:, half * LANES:(half + 1) * LANES].reshape(MAX_DIL, BF16_ROWS, LANES)
                    res_ref[2 * c + half, :, g * BF16_ROWS:(g + 1) * BF16_ROWS, :] = blk


def _inproj(x2d, gain, w_all, perm, batch, seq, n_hp, fw, tm):
    n_tok, d_model = x2d.shape
    nt = seq // tm
    tok_map = lambda i: (i // nt, 0, i % nt, 0)
    res_map = lambda i: (i // nt, 0, 0, i % nt, 0)
    tok_shape = jax.ShapeDtypeStruct((batch, n_hp, seq, LANES), BF16)
    res_shape = jax.ShapeDtypeStruct((batch, n_hp, MAX_DIL, seq // MAX_DIL, LANES), BF16)
    return pl.pallas_call(
        functools.partial(_inproj_kernel, n_hp=n_hp, fw=fw),
        grid=(n_tok // tm,),
        in_specs=[pl.BlockSpec((tm, d_model), lambda i: (i, 0)),
                  pl.BlockSpec((1, d_model), lambda i: (0, 0)),
                  pl.BlockSpec(w_all.shape, lambda i: (0, 0)),
                  pl.BlockSpec(perm.shape, lambda i: (0, 0))],
        out_specs=[pl.BlockSpec((None, n_hp, tm, LANES), tok_map)] * 3
        + [pl.BlockSpec((None, n_hp, MAX_DIL, tm // MAX_DIL, LANES), res_map)] * 3
        + [pl.BlockSpec((None, 2, tm, fw), tok_map)],
        out_shape=[tok_shape] * 3 + [res_shape] * 3 + [jax.ShapeDtypeStruct((batch, 2, seq, fw), BF16)],
        compiler_params=_cparams(1),
        name="rmsnorm_inproj",
    )(x2d, gain, w_all, perm)


def _t5_bucket_static(rel, dtype):
    nb = N_REL_BUCKETS // 2
    max_exact = nb // 2
    n = np.abs(rel)
    nf = np.maximum(n, 1).astype(dtype)
    large = max_exact + (np.log(nf / dtype(max_exact)) / dtype(math.log(REL_MAX_DISTANCE / max_exact))
                         * dtype(nb - max_exact)).astype(np.int32)
    large = np.minimum(large, nb - 1)
    return np.where(rel > 0, nb, 0) + np.where(n < max_exact, n, large)


def _bucket_tiles(dilations):
    qi = np.arange(TQ)[:, None]
    kc = np.arange(TK)[None, :]
    offsets = np.array([0, HALF_WINDOW, 2 * HALF_WINDOW])
    rel = kc[None] - offsets[:, None, None] - qi[None]
    tiles = []
    for d in dilations:
        bkt = _t5_bucket_static(rel * d, np.float32)
        assert np.array_equal(bkt, _t5_bucket_static(rel * d, np.float64))
        tile = np.where(np.abs(rel) <= HALF_WINDOW, bkt, -1)
        pieces = MAX_DIL // d if 1 < d < MAX_DIL else 1
        q_order = (np.arange(TQ) % (TQ // pieces)) * pieces + np.arange(TQ) // (TQ // pieces)
        k_order = (np.arange(TK) % (TK // pieces)) * pieces + np.arange(TK) // (TK // pieces)
        tiles.append(tile[:, q_order][:, :, k_order])
    return np.concatenate(tiles, axis=0).astype(np.int32)


def _bias_kernel(table_ref, bkt_ref, out_ref, *, n_heads):
    bkt = bkt_ref[...]
    for h in range(n_heads):
        acc = jnp.full(bkt.shape, NEG_INF, F32)
        for b in range(N_REL_BUCKETS):
            acc = jnp.where(bkt == b, table_ref[b, h], acc)
        out_ref[h] = acc * LOG2E


def _bias_tiles(rel_table, dilations):
    n_heads = rel_table.shape[1]
    bkt = jnp.asarray(_bucket_tiles(dilations))
    return pl.pallas_call(
        functools.partial(_bias_kernel, n_heads=n_heads),
        grid=(bkt.shape[0],),
        in_specs=[pl.BlockSpec(memory_space=pltpu.SMEM),
                  pl.BlockSpec((None, TQ, TK), lambda t: (t, 0, 0))],
        out_specs=pl.BlockSpec((None, n_heads, TQ, TK), lambda t: (t, 0, 0, 0)),
        out_shape=jax.ShapeDtypeStruct((bkt.shape[0], n_heads, TQ, TK), F32),
        compiler_params=_cparams(1),
        name="rel_bias_tiles",
    )(rel_table.astype(F32), bkt)


def _block_attention(qb, kb, vb, bias2, first_head):
    nt_dims = (((1,), (1,)), ((), ()))
    zero = jnp.zeros_like(qb)
    q2 = jnp.concatenate([jnp.where(first_head, qb, zero), jnp.where(first_head, zero, qb)], axis=0)
    s = lax.dot_general(q2, kb, nt_dims, preferred_element_type=F32) + bias2
    m = jnp.max(s, axis=-1, keepdims=True)
    p = jnp.exp2(s - m)
    l = jnp.sum(p, axis=-1, keepdims=True)
    pv = jnp.dot(p.astype(BF16), vb, preferred_element_type=F32)
    num = jnp.where(first_head, pv[:TQ], pv[TQ:])
    den = jnp.where(first_head, l[:TQ], l[TQ:])
    top = jnp.where(first_head, m[:TQ], m[TQ:])
    return num * (1.0 / den), (top + jnp.log2(den)) * LN2


def _merge(o_a, lse_a, o_b, lse_b):
    top = jnp.maximum(lse_a, lse_b)
    w_a = jnp.exp(lse_a - top)
    w_b = jnp.exp(lse_b - top)
    den = w_a + w_b
    return (w_a * o_a + w_b * o_b) * (1.0 / den), top + jnp.log(den)


def _split_f32(x):
    hi = x.astype(BF16)
    return hi, (x - hi.astype(F32)).astype(BF16)


def _attn_kernel(*refs, dilation, merge_in, split_out):
    q_ref, k_ref, v_ref, bias_ref = refs[:4]
    pos = 4
    if merge_in is not None:
        po_ref, pl_ref = refs[pos:pos + 2]
        pos += 2
    o_ref = refs[pos]
    lse_refs = refs[pos + 1:]
    pieces = MAX_DIL // dilation if dilation > 1 else 1
    sub_len = q_ref.shape[-2] * pieces
    n_blk = sub_len // TQ
    tq_p, tk_p = TQ // pieces, TK // pieces
    first_head = lax.broadcasted_iota(jnp.int32, (TQ, LANES), 1) < HEAD_DIM

    def load(ref, r, row, n_rows):
        if dilation == 1:
            return ref[pl.ds(row, n_rows), :]
        parts = [ref[a * dilation + r, pl.ds(row, n_rows), :] for a in range(pieces)]
        return parts[0] if pieces == 1 else jnp.concatenate(parts, axis=0)

    def store(ref, r, row, n_rows, val):
        if dilation == 1:
            ref[pl.ds(row, n_rows), :] = val
        else:
            for a in range(pieces):
                ref[a * dilation + r, pl.ds(row, n_rows), :] = val[a * n_rows:(a + 1) * n_rows]

    def block(n, carry):
        q_row = pl.multiple_of(n * tq_p, tq_p)
        k_row = pl.multiple_of(jnp.clip(n * tq_p - tk_p // 4, 0, sub_len // pieces - tk_p), tk_p // 4)
        edge = jnp.where(n > 0, 1, 0) + jnp.where(n == n_blk - 1, 1, 0)
        for r in range(dilation):
            qb = load(q_ref, r, q_row, tq_p)
            kb = load(k_ref, r, k_row, tk_p)
            vb = load(v_ref, r, k_row, tk_p)
            o, lse = _block_attention(qb, kb, vb, bias_ref[edge].reshape(2 * TQ, TK), first_head)
            if merge_in is not None:
                o, lse = _merge(o, lse, load(po_ref, r, q_row, tq_p).astype(F32), load(pl_ref, r, q_row, tq_p))
            store(o_ref, r, q_row, tq_p, o.astype(BF16))
            if split_out:
                hi, lo = _split_f32(lse)
                store(lse_refs[0], r, q_row, tq_p, hi)
                store(lse_refs[1], r, q_row, tq_p, lo)
            elif lse_refs:
                store(lse_refs[0], r, q_row, tq_p, lse)
        return carry

    lax.fori_loop(0, n_blk, block, 0, unroll=max(1, BLOCKS_IN_FLIGHT // dilation))


def _attn_token_kernel(q_ref, k_ref, v_ref, bias_ref, perm_ref, po_ref, hi_ref, lo_ref, o_ref,
                       o_tok, lse_tok):
    perm = perm_ref[...]
    for g in range(po_ref.shape[1] // BF16_ROWS):
        rows = slice(g * BF16_ROWS, (g + 1) * BF16_ROWS)
        o_hi = jnp.concatenate([po_ref[:, rows, :].reshape(GROUP, LANES),
                                hi_ref[:, rows, :].reshape(GROUP, LANES)], axis=1)
        tok = jnp.dot(perm, o_hi, preferred_element_type=F32)
        lo = jnp.dot(perm, lo_ref[:, rows, :].reshape(GROUP, LANES), preferred_element_type=F32)
        o_tok[g * GROUP:(g + 1) * GROUP, :] = tok[:, :LANES].astype(BF16)
        lse_tok[g * GROUP:(g + 1) * GROUP, :] = tok[:, LANES:] + lo
    _attn_kernel(q_ref, k_ref, v_ref, bias_ref, o_tok, lse_tok, o_ref,
                 dilation=1, merge_in='token', split_out=False)


def _attention(q, k, v, qr, kr, vr, bias, perm, dilations):
    batch, n_hp, seq, _ = q.shape
    assert dilations[0] == MAX_DIL and dilations[-1] == 1
    res_slab = pl.BlockSpec((None, None, MAX_DIL, seq // MAX_DIL, LANES), lambda b, h: (b, h, 0, 0, 0))
    tok_slab = pl.BlockSpec((None, None, seq, LANES), lambda b, h: (b, h, 0, 0))
    res_shape = lambda dt: jax.ShapeDtypeStruct((batch, n_hp, MAX_DIL, seq // MAX_DIL, LANES), dt)
    bias_spec = lambda i: pl.BlockSpec((3, 2, TQ, TK), lambda b, h: (i, h, 0, 0))

    merged = None
    for i, d in enumerate(dilations[:-1]):
        final_res = i == len(dilations) - 2
        args = [qr, kr, vr, bias] + (list(merged) if merged else [])
        in_specs = [res_slab] * 3 + [bias_spec(i)] + ([res_slab] * 2 if merged else [])
        out_shape = [res_shape(BF16)] + ([res_shape(BF16)] * 2 if final_res else [res_shape(F32)])
        merged = pl.pallas_call(
            functools.partial(_attn_kernel, dilation=d, merge_in='f32' if merged else None,
                              split_out=final_res),
            grid=(batch, n_hp),
            in_specs=in_specs,
            out_specs=[res_slab] * len(out_shape),
            out_shape=out_shape,
            compiler_params=_cparams(2),
            name=f"dilated_attn_d{d}",
        )(*args)

    return pl.pallas_call(
        _attn_token_kernel,
        grid=(batch, n_hp),
        in_specs=[tok_slab] * 3 + [bias_spec(len(dilations) - 1), pl.BlockSpec(perm.shape, lambda b, h: (0, 0))]
        + [res_slab] * 3,
        out_specs=tok_slab,
        out_shape=jax.ShapeDtypeStruct((batch, n_hp, seq, LANES), BF16),
        scratch_shapes=[pltpu.VMEM((seq, LANES), BF16), pltpu.VMEM((seq, LANES), F32)],
        compiler_params=_cparams(2),
        name="dilated_attn_d1",
    )(q, k, v, bias, perm, *merged)


def _fft_stage1_kernel(w_ref, v_ref, z_ref, *, chunk):
    w = w_ref[...]
    for c0 in range(0, v_ref.shape[-1], chunk):
        z = jnp.dot(w, v_ref[:, c0:c0 + chunk], preferred_element_type=F32)
        z_ref[:, c0:c0 + chunk] = z.astype(BF16)


def _fft_stage2_kernel(m_ref, z_ref, b_ref, o_ref, *, group, fw):
    for j in range(group):
        zc = jnp.concatenate([z_ref[0, j], z_ref[1, j]], axis=0)
        x = jnp.dot(m_ref[j], zc, preferred_element_type=F32) + b_ref[...]
        o_ref[:, j * fw:(j + 1) * fw] = x.astype(BF16)


def _fourier(pq, fourier_b):
    batch, _, seq, fw = pq.shape
    R = FFT_RADIX
    assert seq == R * R
    i = np.arange(R)
    ang1 = 2.0 * np.pi * np.outer(i, i) / R
    c1, s1 = np.cos(ang1), np.sin(ang1)
    w_cat = jnp.asarray(np.block([[c1, -s1], [-s1, -c1]]), BF16)
    k_all = i[:, None, None] + R * i[None, :, None]
    ang2 = 2.0 * np.pi * ((k_all * i[None, None, :]) % seq) / seq
    m_cat = jnp.asarray(np.concatenate([np.cos(ang2), np.sin(ang2)], axis=-1), BF16)

    v = pq.reshape(batch, 2 * R, R * fw)
    z = pl.pallas_call(
        functools.partial(_fft_stage1_kernel, chunk=2048),
        grid=(batch,),
        in_specs=[pl.BlockSpec((2 * R, 2 * R), lambda b: (0, 0)),
                  pl.BlockSpec((None, 2 * R, R * fw), lambda b: (b, 0, 0))],
        out_specs=pl.BlockSpec((None, 2 * R, R * fw), lambda b: (b, 0, 0)),
        out_shape=jax.ShapeDtypeStruct((batch, 2 * R, R * fw), BF16),
        compiler_params=_cparams(1),
        name="fft_stage1",
    )(w_cat, v)

    group = 16
    z5 = z.reshape(batch, 2, R, R, fw)
    out = pl.pallas_call(
        functools.partial(_fft_stage2_kernel, group=group, fw=fw),
        grid=(batch, R // group),
        in_specs=[pl.BlockSpec((group, R, 2 * R), lambda b, g: (g, 0, 0)),
                  pl.BlockSpec((None, 2, group, R, fw), lambda b, g: (b, 0, g, 0, 0)),
                  pl.BlockSpec((1, fw), lambda b, g: (0, 0))],
        out_specs=pl.BlockSpec((None, R, group * fw), lambda b, g: (b, 0, g)),
        out_shape=jax.ShapeDtypeStruct((batch, R, R * fw), BF16),
        compiler_params=_cparams(2),
        name="fft_stage2",
    )(m_cat, z5, fourier_b.reshape(1, fw).astype(F32))
    return out.reshape(batch, seq, fw)


def _outproj_kernel(a_ref, f_ref, x_ref, ga_ref, gf_ref, w_ref, o_ref, *, n_hp):
    parts = [a_ref[j].astype(F32) for j in range(n_hp)]
    ssq = functools.reduce(lambda a, b: a + b,
                           [jnp.sum(p * p, axis=-1, keepdims=True) for p in parts])
    inv = lax.rsqrt(ssq / (n_hp * LANES) + EPS)
    ga = ga_ref[...]
    cols = [(p * inv * ga[:, j * LANES:(j + 1) * LANES]).astype(BF16) for j, p in enumerate(parts)]
    cols.append(_rms(f_ref[...].astype(F32), gf_ref[...]).astype(BF16))
    mixed = jnp.concatenate(cols, axis=1)
    o_ref[...] = x_ref[...] + jnp.dot(mixed, w_ref[...], preferred_element_type=F32)


def _outproj(attn, four, x2d, ga, gf, w_out, tm):
    batch, n_hp, seq, _ = attn.shape
    fw = four.shape[-1]
    n_tok, d_model = x2d.shape
    nt = seq // tm
    return pl.pallas_call(
        functools.partial(_outproj_kernel, n_hp=n_hp),
        grid=(n_tok // tm,),
        in_specs=[pl.BlockSpec((None, n_hp, tm, LANES), lambda i: (i // nt, 0, i % nt, 0)),
                  pl.BlockSpec((None, tm, fw), lambda i: (i // nt, i % nt, 0)),
                  pl.BlockSpec((tm, d_model), lambda i: (i, 0)),
                  pl.BlockSpec((1, n_hp * LANES), lambda i: (0, 0)),
                  pl.BlockSpec((1, fw), lambda i: (0, 0)),
                  pl.BlockSpec(w_out.shape, lambda i: (0, 0))],
        out_specs=pl.BlockSpec((tm, d_model), lambda i: (i, 0)),
        out_shape=jax.ShapeDtypeStruct((n_tok, d_model), F32),
        compiler_params=_cparams(1),
        name="mix_outproj",
    )(attn, four, x2d, ga, gf, w_out)


def _ffn_kernel(xp_ref, x_ref, xn_ref, g_ref, wg_ref, wv_ref, cw_ref, cb_ref, wd_ref, gfin_ref,
                o_ref, act_ref, *, tiles_per_seq, fc, final_norm):
    i = pl.program_id(0)
    tm = x_ref.shape[0]
    gain = g_ref[...]
    x = x_ref[...]
    keep_prev = (i % tiles_per_seq != 0).astype(F32)
    keep_next = (i % tiles_per_seq != tiles_per_seq - 1).astype(F32)
    x_ext = jnp.concatenate([xp_ref[...], x, xn_ref[...]], axis=0)
    h_ext = _rms(x_ext, gain)
    row = lax.broadcasted_iota(jnp.int32, (tm + 2 * HALO, 1), 0)
    keep = jnp.where(row < HALO, keep_prev, jnp.where(row >= HALO + tm, keep_next, 1.0))
    h_ext = (h_ext * keep).astype(BF16)
    h = _rms(x, gain).astype(BF16)
    n_ext = tm + 2 * HALO
    d_ff = wg_ref.shape[1]

    def up(c0):
        cs = slice(c0, c0 + fc)
        return (jnp.dot(h_ext, wg_ref[:, cs], preferred_element_type=F32),
                jnp.dot(h, wv_ref[:, cs], preferred_element_type=F32))

    nxt = up(0)
    for c0 in range(0, d_ff, fc):
        cs = slice(c0, c0 + fc)
        g, val = nxt
        if c0 + fc < d_ff:
            nxt = up(c0 + fc)
        g_prev = pltpu.roll(g, 1, axis=0)[HALO:HALO + tm]
        g_next = pltpu.roll(g, n_ext - 1, axis=0)[HALO:HALO + tm]
        cw = cw_ref[:, cs]
        conv = cw[0:1] * g_prev + cw[1:2] * g[HALO:HALO + tm] + cw[2:3] * g_next + cb_ref[:, cs]
        act_ref[:, cs] = (conv * (1.0 / (1.0 + jnp.exp(-conv))) * val).astype(BF16)
    y = x + jnp.dot(act_ref[...], wd_ref[...], preferred_element_type=F32)
    o_ref[...] = _rms(y, gfin_ref[...]) if final_norm else y


def _ffn(x1, gain, wg, wv, conv_w, conv_b, wd, gfin, seq, tm, fc, final_norm):
    n_tok, d_model = x1.shape
    d_ff = wg.shape[1]
    blocks_per_tile = tm // HALO
    n_halo_blocks = n_tok // HALO
    const = lambda shape: pl.BlockSpec(shape, lambda i: (0,) * len(shape), pipeline_mode=pl.Buffered(1))
    return pl.pallas_call(
        functools.partial(_ffn_kernel, tiles_per_seq=seq // tm, fc=fc, final_norm=final_norm),
        grid=(n_tok // tm,),
        in_specs=[pl.BlockSpec((HALO, d_model), lambda i: (jnp.maximum(i * blocks_per_tile - 1, 0), 0)),
                  pl.BlockSpec((tm, d_model), lambda i: (i, 0)),
                  pl.BlockSpec((HALO, d_model),
                               lambda i: (jnp.minimum((i + 1) * blocks_per_tile, n_halo_blocks - 1), 0)),
                  const((1, d_model)), const(wg.shape), const(wv.shape),
                  const(conv_w.shape), const((1, d_ff)), const(wd.shape), const((1, d_model))],
        out_specs=pl.BlockSpec((tm, d_model), lambda i: (i, 0)),
        out_shape=jax.ShapeDtypeStruct((n_tok, d_model), F32),
        scratch_shapes=[pltpu.VMEM((tm, d_ff), BF16)],
        compiler_params=_cparams(1),
        name="convglu_ffn",
    )(x1, x1, x1, gain, wg, wv, conv_w, conv_b, wd, gfin)


def kernel(x, norm_mix_gain, w_in, attn_out_gain, rel_bias_table, fourier_w, fourier_b, fourier_out_gain, w_out, norm_ffn_gain, w_gate, w_val, conv_w, conv_b, w_down, final_norm_gain):
    batch, seq, d_model = x.shape
    depth = w_in.shape[0]
    n_heads = rel_bias_table.shape[1]
    attn_w = n_heads * HEAD_DIM
    n_hp = attn_w // LANES
    fw = fourier_w.shape[1] * fourier_w.shape[2]
    assert all(w // (2 * d) == HALF_WINDOW for w, d in DILATED_PATTERNS)
    dilations = sorted((d for _, d in DILATED_PATTERNS), reverse=True)
    tm = 512
    row = lambda g: g.reshape(1, -1).astype(F32)
    perm = _group_permutation()
    bias = _bias_tiles(rel_bias_table, dilations)

    x2d = x.reshape(batch * seq, d_model)
    for layer in range(depth):
        w_l = w_in[layer]
        wp, wq = _fold_fourier_weights(w_l[:, 3 * attn_w:], fourier_w[layer], seq)
        w_all = jnp.concatenate([w_l[:, :attn_w] * (HEAD_DIM ** -0.5 * LOG2E), w_l[:, attn_w:3 * attn_w], wp, wq],
                                axis=1).astype(BF16)
        q, k, v, qr, kr, vr, pq = _inproj(x2d, row(norm_mix_gain[layer]), w_all, perm, batch, seq, n_hp, fw, tm)
        attn = _attention(q, k, v, qr, kr, vr, bias, perm, dilations)
        four = _fourier(pq, fourier_b[layer].reshape(-1))
        x2d = _outproj(attn, four, x2d, row(attn_out_gain[layer]), row(fourier_out_gain[layer]),
                       w_out[layer].astype(BF16), tm)
        x2d = _ffn(x2d, row(norm_ffn_gain[layer]), w_gate[layer].astype(BF16), w_val[layer].astype(BF16),
                   conv_w[layer].astype(F32), row(conv_b[layer]), w_down[layer].astype(BF16),
                   row(final_norm_gain), seq, tm, 256, layer == depth - 1)
    return x2d.reshape(batch, seq, d_model)
```

```python
import functools
import math

import numpy as np
import jax
import jax.numpy as jnp
from jax import lax
from jax.experimental import pallas as pl
from jax.experimental.pallas import tpu as pltpu

EPS = 1e-6
NEG_INF = -1e30
LOG2E = math.log2(math.e)
HEAD_DIM = 64
DILATED_PATTERNS = ((128, 1), (512, 4), (2048, 16))
N_REL_BUCKETS = 32
REL_MAX_DISTANCE = 1024

LANES = 128
BF16_ROWS = 16
HALF_WINDOW = 64
TQ = 2 * HALF_WINDOW
TK = 4 * HALF_WINDOW
MAX_DIL = max(d for _, d in DILATED_PATTERNS)
GROUP = MAX_DIL * BF16_ROWS
BLOCKS_IN_FLIGHT = 8
FFT_RADIX = 64
HALO = BF16_ROWS
VMEM_LIMIT = 56 * 1024 * 1024

BF16 = jnp.bfloat16
F32 = jnp.float32


def _cparams(n_axes):
    return pltpu.CompilerParams(dimension_semantics=("arbitrary",) * n_axes,
                                vmem_limit_bytes=VMEM_LIMIT)


def _rms(x, gain):
    ms = jnp.mean(x * x, axis=-1, keepdims=True)
    return x * lax.rsqrt(ms + EPS) * gain


def _group_permutation():
    t = np.arange(GROUP)
    swapped = (t % MAX_DIL) * BF16_ROWS + t // MAX_DIL
    perm = np.zeros((GROUP, GROUP), np.float32)
    perm[swapped, t] = 1.0
    assert MAX_DIL == BF16_ROWS and np.array_equal(perm, perm.T)
    return jnp.asarray(perm, BF16)


def _prep_kernel(wu_ref, fw_ref, cc_ref, sc_ref, wp_ref, wq_ref):
    hi = lax.Precision.HIGHEST
    fw = fw_ref[...]
    a = jnp.dot(cc_ref[...], fw, precision=hi, preferred_element_type=F32)
    b = jnp.dot(sc_ref[...], fw, precision=hi, preferred_element_type=F32)
    wu = wu_ref[...]
    wp_ref[...] = jnp.dot(wu, a, precision=hi, preferred_element_type=F32)
    wq_ref[...] = jnp.dot(wu, b, precision=hi, preferred_element_type=F32)


def _fold_fourier_weights(w_u, fourier_w, seq):
    d_model = w_u.shape[0]
    groups, gd, _ = fourier_w.shape
    ang = 2.0 * np.pi * np.outer(np.arange(gd), np.arange(gd)) / gd
    scale = 1.0 / math.sqrt(seq * gd)
    cc = jnp.asarray(np.cos(ang) * scale, F32)
    sc = jnp.asarray(np.sin(ang) * scale, F32)
    wu_g = w_u.reshape(d_model, groups, gd).transpose(1, 0, 2)
    wp, wq = pl.pallas_call(
        _prep_kernel,
        grid=(groups,),
        in_specs=[pl.BlockSpec((None, d_model, gd), lambda g: (g, 0, 0)),
                  pl.BlockSpec((None, gd, gd), lambda g: (g, 0, 0)),
                  pl.BlockSpec((gd, gd), lambda g: (0, 0)),
                  pl.BlockSpec((gd, gd), lambda g: (0, 0))],
        out_specs=[pl.BlockSpec((None, d_model, gd), lambda g: (g, 0, 0))] * 2,
        out_shape=[jax.ShapeDtypeStruct((groups, d_model, gd), F32)] * 2,
        compiler_params=_cparams(1),
        name="fourier_weight_fold",
    )(wu_g, fourier_w, cc, sc)
    unfold = lambda w: w.transpose(1, 0, 2).reshape(d_model, groups * gd)
    return unfold(wp), unfold(wq)


def _inproj_kernel(x_ref, g_ref, w_ref, perm_ref, q_ref, k_ref, v_ref, qr_ref, kr_ref, vr_ref, pq_ref,
                   *, n_hp, fw):
    h = _rms(x_ref[...], g_ref[...]).astype(BF16)
    tm = h.shape[0]
    perm = perm_ref[...]

    def proj(c0, n):
        return jnp.dot(h, w_ref[:, c0:c0 + n], preferred_element_type=F32)

    aw = n_hp * LANES
    pairs = ((q_ref, qr_ref), (k_ref, kr_ref), (v_ref, vr_ref))
    for t, (tok_ref, _) in enumerate(pairs):
        for c in range(n_hp // 2):
            res = proj(t * aw + c * 2 * LANES, 2 * LANES).astype(BF16)
            tok_ref[2 * c] = res[:, :LANES]
            tok_ref[2 * c + 1] = res[:, LANES:]
    for t in range(2):
        pq_ref[t] = proj(3 * aw + t * fw, fw).astype(BF16)
    for tok_ref, res_ref in pairs:
        for c in range(n_hp // 2):
            for g in range(tm // GROUP):
                tok = jnp.concatenate([tok_ref[2 * c, g * GROUP:(g + 1) * GROUP, :],
                                       tok_ref[2 * c + 1, g * GROUP:(g + 1) * GROUP, :]], axis=1)
                rows = jnp.dot(perm, tok, preferred_element_type=F32).astype(BF16)
                for half in range(2):
                    blk = rows[:, half * LANES:(half + 1) * LANES].reshape(MAX_DIL, BF16_ROWS, LANES)
                    res_ref[2 * c + half, :, g * BF16_ROWS:(g + 1) * BF16_ROWS, :] = blk


def _inproj(x2d, gain, w_all, perm, batch, seq, n_hp, fw, tm):
    n_tok, d_model = x2d.shape
    nt = seq // tm
    tok_map = lambda i: (i // nt, 0, i % nt, 0)
    res_map = lambda i: (i // nt, 0, 0, i % nt, 0)
    tok_shape = jax.ShapeDtypeStruct((batch, n_hp, seq, LANES), BF16)
    res_shape = jax.ShapeDtypeStruct((batch, n_hp, MAX_DIL, seq // MAX_DIL, LANES), BF16)
    return pl.pallas_call(
        functools.partial(_inproj_kernel, n_hp=n_hp, fw=fw),
        grid=(n_tok // tm,),
        in_specs=[pl.BlockSpec((tm, d_model), lambda i: (i, 0)),
                  pl.BlockSpec((1, d_model), lambda i: (0, 0)),
                  pl.BlockSpec(w_all.shape, lambda i: (0, 0)),
                  pl.BlockSpec(perm.shape, lambda i: (0, 0))],
        out_specs=[pl.BlockSpec((None, n_hp, tm, LANES), tok_map)] * 3
        + [pl.BlockSpec((None, n_hp, MAX_DIL, tm // MAX_DIL, LANES), res_map)] * 3
        + [pl.BlockSpec((None, 2, tm, fw), tok_map)],
        out_shape=[tok_shape] * 3 + [res_shape] * 3 + [jax.ShapeDtypeStruct((batch, 2, seq, fw), BF16)],
        compiler_params=_cparams(1),
        name="rmsnorm_inproj",
    )(x2d, gain, w_all, perm)


def _t5_bucket_static(rel, dtype):
    nb = N_REL_BUCKETS // 2
    max_exact = nb // 2
    n = np.abs(rel)
    nf = np.maximum(n, 1).astype(dtype)
    large = max_exact + (np.log(nf / dtype(max_exact)) / dtype(math.log(REL_MAX_DISTANCE / max_exact))
                         * dtype(nb - max_exact)).astype(np.int32)
    large = np.minimum(large, nb - 1)
    return np.where(rel > 0, nb, 0) + np.where(n < max_exact, n, large)


def _bucket_tiles(dilations):
    qi = np.arange(TQ)[:, None]
    kc = np.arange(TK)[None, :]
    offsets = np.array([0, HALF_WINDOW, 2 * HALF_WINDOW])
    rel = kc[None] - offsets[:, None, None] - qi[None]
    tiles = []
    for d in dilations:
        bkt = _t5_bucket_static(rel * d, np.float32)
        assert np.array_equal(bkt, _t5_bucket_static(rel * d, np.float64))
        tile = np.where(np.abs(rel) <= HALF_WINDOW, bkt, -1)
        pieces = MAX_DIL // d if 1 < d < MAX_DIL else 1
        q_order = (np.arange(TQ) % (TQ // pieces)) * pieces + np.arange(TQ) // (TQ // pieces)
        k_order = (np.arange(TK) % (TK // pieces)) * pieces + np.arange(TK) // (TK // pieces)
        tiles.append(tile[:, q_order][:, :, k_order])
    return np.concatenate(tiles, axis=0).astype(np.int32)


def _bias_kernel(table_ref, bkt_ref, out_ref, *, n_heads):
    bkt = bkt_ref[...]
    for h in range(n_heads):
        acc = jnp.full(bkt.shape, NEG_INF, F32)
        for b in range(N_REL_BUCKETS):
            acc = jnp.where(bkt == b, table_ref[b, h], acc)
        out_ref[h] = acc * LOG2E


def _bias_tiles(rel_table, dilations):
    n_heads = rel_table.shape[1]
    bkt = jnp.asarray(_bucket_tiles(dilations))
    return pl.pallas_call(
        functools.partial(_bias_kernel, n_heads=n_heads),
        grid=(bkt.shape[0],),
        in_specs=[pl.BlockSpec(memory_space=pltpu.SMEM),
                  pl.BlockSpec((None, TQ, TK), lambda t: (t, 0, 0))],
        out_specs=pl.BlockSpec((None, n_heads, TQ, TK), lambda t: (t, 0, 0, 0)),
        out_shape=jax.ShapeDtypeStruct((bkt.shape[0], n_heads, TQ, TK), F32),
        compiler_params=_cparams(1),
        name="rel_bias_tiles",
    )(rel_table.astype(F32), bkt)


def _block_attention(qb, kb, vb, bias2, first_head):
    nt_dims = (((1,), (1,)), ((), ()))
    zero = jnp.zeros_like(qb)
    q2 = jnp.concatenate([jnp.where(first_head, qb, zero), jnp.where(first_head, zero, qb)], axis=0)
    s = lax.dot_general(q2, kb, nt_dims, preferred_element_type=F32) + bias2
    m = jnp.max(s, axis=-1, keepdims=True)
    p = jnp.exp2(s - m)
    l = jnp.sum(p, axis=-1, keepdims=True)
    pv = jnp.dot(p.astype(BF16), vb, preferred_element_type=F32)
    num = jnp.where(first_head, pv[:TQ], pv[TQ:])
    den = jnp.where(first_head, l[:TQ], l[TQ:])
    top = jnp.where(first_head, m[:TQ], m[TQ:])
    return num * (1.0 / den), top + jnp.log2(den)


def _merge(o_a, lse_a, o_b, lse_b):
    top = jnp.maximum(lse_a, lse_b)
    w_a = jnp.exp2(lse_a - top)
    w_b = jnp.exp2(lse_b - top)
    den = w_a + w_b
    return (w_a * o_a + w_b * o_b) * (1.0 / den), top + jnp.log2(den)


def _split_f32(x):
    hi = x.astype(BF16)
    return hi, (x - hi.astype(F32)).astype(BF16)


def _attn_kernel(*refs, dilation, merge_in, split_out):
    q_ref, k_ref, v_ref, bias_ref = refs[:4]
    pos = 4
    if merge_in is not None:
        po_ref, pl_ref = refs[pos:pos + 2]
        pos += 2
    o_ref = refs[pos]
    lse_refs = refs[pos + 1:]
    pieces = MAX_DIL // dilation if dilation > 1 else 1
    sub_len = q_ref.shape[-2] * pieces
    n_blk = sub_len // TQ
    tq_p, tk_p = TQ // pieces, TK // pieces
    first_head = lax.broadcasted_iota(jnp.int32, (TQ, LANES), 1) < HEAD_DIM

    def load(ref, r, row, n_rows):
        if dilation == 1:
            return ref[pl.ds(row, n_rows), :]
        parts = [ref[a * dilation + r, pl.ds(row, n_rows), :] for a in range(pieces)]
        return parts[0] if pieces == 1 else jnp.concatenate(parts, axis=0)

    def store(ref, r, row, n_rows, val):
        if dilation == 1:
            ref[pl.ds(row, n_rows), :] = val
        else:
            for a in range(pieces):
                ref[a * dilation + r, pl.ds(row, n_rows), :] = val[a * n_rows:(a + 1) * n_rows]

    def block(n, carry):
        q_row = pl.multiple_of(n * tq_p, tq_p)
        k_row = pl.multiple_of(jnp.clip(n * tq_p - tk_p // 4, 0, sub_len // pieces - tk_p), tk_p // 4)
        edge = jnp.where(n > 0, 1, 0) + jnp.where(n == n_blk - 1, 1, 0)
        for r in range(dilation):
            qb = load(q_ref, r, q_row, tq_p)
            kb = load(k_ref, r, k_row, tk_p)
            vb = load(v_ref, r, k_row, tk_p)
            o, lse = _block_attention(qb, kb, vb, bias_ref[edge].reshape(2 * TQ, TK), first_head)
            if merge_in is not None:
                o, lse = _merge(o, lse, load(po_ref, r, q_row, tq_p).astype(F32), load(pl_ref, r, q_row, tq_p))
            store(o_ref, r, q_row, tq_p, o.astype(BF16))
            if split_out:
                hi, lo = _split_f32(lse)
                store(lse_refs[0], r, q_row, tq_p, hi)
                store(lse_refs[1], r, q_row, tq_p, lo)
            elif lse_refs:
                store(lse_refs[0], r, q_row, tq_p, lse)
        return carry

    lax.fori_loop(0, n_blk, block, 0, unroll=max(1, BLOCKS_IN_FLIGHT // dilation))


def _attn_token_kernel(q_ref, k_ref, v_ref, bias_ref, perm_ref, po_ref, hi_ref, lo_ref, o_ref,
                       o_tok, lse_tok):
    perm = perm_ref[...]
    for g in range(po_ref.shape[1] // BF16_ROWS):
        rows = slice(g * BF16_ROWS, (g + 1) * BF16_ROWS)
        o_hi = jnp.concatenate([po_ref[:, rows, :].reshape(GROUP, LANES),
                                hi_ref[:, rows, :].reshape(GROUP, LANES)], axis=1)
        tok = jnp.dot(perm, o_hi, preferred_element_type=F32)
        lo = jnp.dot(perm, lo_ref[:, rows, :].reshape(GROUP, LANES), preferred_element_type=F32)
        o_tok[g * GROUP:(g + 1) * GROUP, :] = tok[:, :LANES].astype(BF16)
        lse_tok[g * GROUP:(g + 1) * GROUP, :] = tok[:, LANES:] + lo
    _attn_kernel(q_ref, k_ref, v_ref, bias_ref, o_tok, lse_tok, o_ref,
                 dilation=1, merge_in='token', split_out=False)


def _attention(q, k, v, qr, kr, vr, bias, perm, dilations):
    batch, n_hp, seq, _ = q.shape
    assert dilations[0] == MAX_DIL and dilations[-1] == 1
    res_slab = pl.BlockSpec((None, None, MAX_DIL, seq // MAX_DIL, LANES), lambda b, h: (b, h, 0, 0, 0))
    tok_slab = pl.BlockSpec((None, None, seq, LANES), lambda b, h: (b, h, 0, 0))
    res_shape = lambda dt: jax.ShapeDtypeStruct((batch, n_hp, MAX_DIL, seq // MAX_DIL, LANES), dt)
    bias_spec = lambda i: pl.BlockSpec((3, 2, TQ, TK), lambda b, h: (i, h, 0, 0))

    merged = None
    for i, d in enumerate(dilations[:-1]):
        final_res = i == len(dilations) - 2
        args = [qr, kr, vr, bias] + (list(merged) if merged else [])
        in_specs = [res_slab] * 3 + [bias_spec(i)] + ([res_slab] * 2 if merged else [])
        out_shape = [res_shape(BF16)] + ([res_shape(BF16)] * 2 if final_res else [res_shape(F32)])
        merged = pl.pallas_call(
            functools.partial(_attn_kernel, dilation=d, merge_in='f32' if merged else None,
                              split_out=final_res),
            grid=(batch, n_hp),
            in_specs=in_specs,
            out_specs=[res_slab] * len(out_shape),
            out_shape=out_shape,
            compiler_params=_cparams(2),
            name=f"dilated_attn_d{d}",
        )(*args)

    return pl.pallas_call(
        _attn_token_kernel,
        grid=(batch, n_hp),
        in_specs=[tok_slab] * 3 + [bias_spec(len(dilations) - 1), pl.BlockSpec(perm.shape, lambda b, h: (0, 0))]
        + [res_slab] * 3,
        out_specs=tok_slab,
        out_shape=jax.ShapeDtypeStruct((batch, n_hp, seq, LANES), BF16),
        scratch_shapes=[pltpu.VMEM((seq, LANES), BF16), pltpu.VMEM((seq, LANES), F32)],
        compiler_params=_cparams(2),
        name="dilated_attn_d1",
    )(q, k, v, bias, perm, *merged)


def _fft_stage1_kernel(w_ref, v_ref, z_ref, *, chunk):
    w = w_ref[...]
    for c0 in range(0, v_ref.shape[-1], chunk):
        z = jnp.dot(w, v_ref[:, c0:c0 + chunk], preferred_element_type=F32)
        z_ref[:, c0:c0 + chunk] = z.astype(BF16)


def _fft_stage2_kernel(m_ref, z_ref, b_ref, o_ref, *, group, fw):
    for j in range(group):
        zc = jnp.concatenate([z_ref[0, j], z_ref[1, j]], axis=0)
        x = jnp.dot(m_ref[j], zc, preferred_element_type=F32) + b_ref[...]
        o_ref[:, j * fw:(j + 1) * fw] = x.astype(BF16)


def _fourier(pq, fourier_b):
    batch, _, seq, fw = pq.shape
    R = FFT_RADIX
    assert seq == R * R
    i = np.arange(R)
    ang1 = 2.0 * np.pi * np.outer(i, i) / R
    c1, s1 = np.cos(ang1), np.sin(ang1)
    w_cat = jnp.asarray(np.block([[c1, -s1], [-s1, -c1]]), BF16)
    k_all = i[:, None, None] + R * i[None, :, None]
    ang2 = 2.0 * np.pi * ((k_all * i[None, None, :]) % seq) / seq
    m_cat = jnp.asarray(np.concatenate([np.cos(ang2), np.sin(ang2)], axis=-1), BF16)

    v = pq.reshape(batch, 2 * R, R * fw)
    z = pl.pallas_call(
        functools.partial(_fft_stage1_kernel, chunk=2048),
        grid=(batch,),
        in_specs=[pl.BlockSpec((2 * R, 2 * R), lambda b: (0, 0)),
                  pl.BlockSpec((None, 2 * R, R * fw), lambda b: (b, 0, 0))],
        out_specs=pl.BlockSpec((None, 2 * R, R * fw), lambda b: (b, 0, 0)),
        out_shape=jax.ShapeDtypeStruct((batch, 2 * R, R * fw), BF16),
        compiler_params=_cparams(1),
        name="fft_stage1",
    )(w_cat, v)

    group = 16
    z5 = z.reshape(batch, 2, R, R, fw)
    out = pl.pallas_call(
        functools.partial(_fft_stage2_kernel, group=group, fw=fw),
        grid=(batch, R // group),
        in_specs=[pl.BlockSpec((group, R, 2 * R), lambda b, g: (g, 0, 0)),
                  pl.BlockSpec((None, 2, group, R, fw), lambda b, g: (b, 0, g, 0, 0)),
                  pl.BlockSpec((1, fw), lambda b, g: (0, 0))],
        out_specs=pl.BlockSpec((None, R, group * fw), lambda b, g: (b, 0, g)),
        out_shape=jax.ShapeDtypeStruct((batch, R, R * fw), BF16),
        compiler_params=_cparams(2),
        name="fft_stage2",
    )(m_cat, z5, fourier_b.reshape(1, fw).astype(F32))
    return out.reshape(batch, seq, fw)


def _mix_ffn_kernel(ap_ref, a_ref, an_ref, fp_ref, f_ref, fn_ref, xp_ref, x_ref, xn_ref,
                    ga_ref, gf_ref, wo_ref, g_ref, wg_ref, wv_ref, cw_ref, cb_ref, wd_ref, gfin_ref,
                    o_ref, act_ref, *, tiles_per_seq, fc, final_norm, n_hp):
    i = pl.program_id(0)
    tm = x_ref.shape[0]
    n_ext = tm + 2 * HALO
    ext = lambda prev, cur, nxt: jnp.concatenate([prev, cur, nxt], axis=0)

    parts = [ext(ap_ref[j], a_ref[j], an_ref[j]).astype(F32) for j in range(n_hp)]
    ssq = functools.reduce(lambda a, b: a + b,
                           [jnp.sum(p * p, axis=-1, keepdims=True) for p in parts])
    inv = lax.rsqrt(ssq / (n_hp * LANES) + EPS)
    ga = ga_ref[...]
    cols = [(p * inv * ga[:, j * LANES:(j + 1) * LANES]).astype(BF16) for j, p in enumerate(parts)]
    cols.append(_rms(ext(fp_ref[...], f_ref[...], fn_ref[...]).astype(F32), gf_ref[...]).astype(BF16))
    mixed = jnp.concatenate(cols, axis=1)
    x_ext = ext(xp_ref[...], x_ref[...], xn_ref[...]) + jnp.dot(mixed, wo_ref[...],
                                                                 preferred_element_type=F32)
    x = x_ext[HALO:HALO + tm]

    keep_prev = (i % tiles_per_seq != 0).astype(F32)
    keep_next = (i % tiles_per_seq != tiles_per_seq - 1).astype(F32)
    row = lax.broadcasted_iota(jnp.int32, (n_ext, 1), 0)
    keep = jnp.where(row < HALO, keep_prev, jnp.where(row >= HALO + tm, keep_next, 1.0))
    h_ext = (_rms(x_ext, g_ref[...]) * keep).astype(BF16)
    h = h_ext[HALO:HALO + tm]
    d_ff = wg_ref.shape[1]

    def up(c0):
        cs = slice(c0, c0 + fc)
        return (jnp.dot(h_ext, wg_ref[:, cs], preferred_element_type=F32),
                jnp.dot(h, wv_ref[:, cs], preferred_element_type=F32))

    nxt = up(0)
    for c0 in range(0, d_ff, fc):
        cs = slice(c0, c0 + fc)
        g, val = nxt
        if c0 + fc < d_ff:
            nxt = up(c0 + fc)
        g_prev = pltpu.roll(g, 1, axis=0)[HALO:HALO + tm]
        g_next = pltpu.roll(g, n_ext - 1, axis=0)[HALO:HALO + tm]
        cw = cw_ref[:, cs]
        conv = cw[0:1] * g_prev + cw[1:2] * g[HALO:HALO + tm] + cw[2:3] * g_next + cb_ref[:, cs]
        act_ref[:, cs] = (conv * (1.0 / (1.0 + jnp.exp(-conv))) * val).astype(BF16)
    y = x + jnp.dot(act_ref[...], wd_ref[...], preferred_element_type=F32)
    o_ref[...] = _rms(y, gfin_ref[...]) if final_norm else y


def _mix_ffn(attn, four, x2d, ga, gf, w_out, gain, wg, wv, conv_w, conv_b, wd, gfin, tm, fc, final_norm):
    batch, n_hp, seq, _ = attn.shape
    fw = four.shape[-1]
    n_tok, d_model = x2d.shape
    d_ff = wg.shape[1]
    nt = seq // tm
    per_tile = tm // HALO
    last = seq // HALO - 1
    prev_blk = lambda i: jnp.maximum((i % nt) * per_tile - 1, 0)
    next_blk = lambda i: jnp.minimum((i % nt + 1) * per_tile, last)
    seq_blk = lambda i: (i // nt) * (last + 1)
    const = lambda shape: pl.BlockSpec(shape, lambda i: (0,) * len(shape), pipeline_mode=pl.Buffered(1))

    def triple(cur_shape, halo_shape, index):
        return [pl.BlockSpec(halo_shape, lambda i: index(i // nt, prev_blk(i))),
                pl.BlockSpec(cur_shape, lambda i: index(i // nt, i % nt)),
                pl.BlockSpec(halo_shape, lambda i: index(i // nt, next_blk(i)))]

    in_specs = (triple((None, n_hp, tm, LANES), (None, n_hp, HALO, LANES), lambda b, t: (b, 0, t, 0))
                + triple((None, tm, fw), (None, HALO, fw), lambda b, t: (b, t, 0))
                + [pl.BlockSpec((HALO, d_model), lambda i: (seq_blk(i) + prev_blk(i), 0)),
                   pl.BlockSpec((tm, d_model), lambda i: (i, 0)),
                   pl.BlockSpec((HALO, d_model), lambda i: (seq_blk(i) + next_blk(i), 0))]
                + [const((1, n_hp * LANES)), const((1, fw)), const(w_out.shape), const((1, d_model)),
                   const(wg.shape), const(wv.shape), const(conv_w.shape), const((1, d_ff)), const(wd.shape),
                   const((1, d_model))])
    return pl.pallas_call(
        functools.partial(_mix_ffn_kernel, tiles_per_seq=nt, fc=fc, final_norm=final_norm, n_hp=n_hp),
        grid=(n_tok // tm,),
        in_specs=in_specs,
        out_specs=pl.BlockSpec((tm, d_model), lambda i: (i, 0)),
        out_shape=jax.ShapeDtypeStruct((n_tok, d_model), F32),
        scratch_shapes=[pltpu.VMEM((tm, d_ff), BF16)],
        compiler_params=_cparams(1),
        name="outproj_convglu_ffn",
    )(attn, attn, attn, four, four, four, x2d, x2d, x2d, ga, gf, w_out, gain, wg, wv, conv_w, conv_b, wd, gfin)


def kernel(x, norm_mix_gain, w_in, attn_out_gain, rel_bias_table, fourier_w, fourier_b, fourier_out_gain, w_out, norm_ffn_gain, w_gate, w_val, conv_w, conv_b, w_down, final_norm_gain):
    batch, seq, d_model = x.shape
    depth = w_in.shape[0]
    n_heads = rel_bias_table.shape[1]
    attn_w = n_heads * HEAD_DIM
    n_hp = attn_w // LANES
    fw = fourier_w.shape[1] * fourier_w.shape[2]
    assert all(w // (2 * d) == HALF_WINDOW for w, d in DILATED_PATTERNS)
    dilations = sorted((d for _, d in DILATED_PATTERNS), reverse=True)
    tm = 512
    row = lambda g: g.reshape(1, -1).astype(F32)
    perm = _group_permutation()
    bias = _bias_tiles(rel_bias_table, dilations)

    x2d = x.reshape(batch * seq, d_model)
    for layer in range(depth):
        w_l = w_in[layer]
        wp, wq = _fold_fourier_weights(w_l[:, 3 * attn_w:], fourier_w[layer], seq)
        w_all = jnp.concatenate([w_l[:, :attn_w] * (HEAD_DIM ** -0.5 * LOG2E), w_l[:, attn_w:3 * attn_w], wp, wq],
                                axis=1).astype(BF16)
        q, k, v, qr, kr, vr, pq = _inproj(x2d, row(norm_mix_gain[layer]), w_all, perm, batch, seq, n_hp, fw, tm)
        attn = _attention(q, k, v, qr, kr, vr, bias, perm, dilations)
        four = _fourier(pq, fourier_b[layer].reshape(-1))
        x2d = _mix_ffn(attn, four, x2d, row(attn_out_gain[layer]), row(fourier_out_gain[layer]),
                       w_out[layer].astype(BF16), row(norm_ffn_gain[layer]), w_gate[layer].astype(BF16),
                       w_val[layer].astype(BF16), conv_w[layer].astype(F32), row(conv_b[layer]),
                       w_down[layer].astype(BF16), row(final_norm_gain), tm, 256, layer == depth - 1)
    return x2d.reshape(batch, seq, d_model)
```

```python
import functools
import math

import numpy as np
import jax
import jax.numpy as jnp
from jax import lax
from jax.experimental import pallas as pl
from jax.experimental.pallas import tpu as pltpu

EPS = 1e-6
NEG_INF = -1e30
LOG2E = math.log2(math.e)
HEAD_DIM = 64
DILATED_PATTERNS = ((128, 1), (512, 4), (2048, 16))
N_REL_BUCKETS = 32
REL_MAX_DISTANCE = 1024

LANES = 128
BF16_ROWS = 16
HALF_WINDOW = 64
TQ = 2 * HALF_WINDOW
TK = 4 * HALF_WINDOW
MAX_DIL = max(d for _, d in DILATED_PATTERNS)
GROUP = MAX_DIL * BF16_ROWS
BLOCKS_IN_FLIGHT = 16
FFT_RADIX = 64
HALO = BF16_ROWS
VMEM_LIMIT = 56 * 1024 * 1024

BF16 = jnp.bfloat16
F32 = jnp.float32


def _cparams(n_axes):
    return pltpu.CompilerParams(dimension_semantics=("arbitrary",) * n_axes,
                                vmem_limit_bytes=VMEM_LIMIT)


def _rms(x, gain):
    ms = jnp.mean(x * x, axis=-1, keepdims=True)
    return x * lax.rsqrt(ms + EPS) * gain


def _group_permutation():
    t = np.arange(GROUP)
    swapped = (t % MAX_DIL) * BF16_ROWS + t // MAX_DIL
    perm = np.zeros((GROUP, GROUP), np.float32)
    perm[swapped, t] = 1.0
    assert MAX_DIL == BF16_ROWS and np.array_equal(perm, perm.T)
    return jnp.asarray(perm, BF16)


def _prep_kernel(wu_ref, fw_ref, cc_ref, sc_ref, wp_ref, wq_ref):
    hi = lax.Precision.HIGHEST
    fw = fw_ref[...]
    a = jnp.dot(cc_ref[...], fw, precision=hi, preferred_element_type=F32)
    b = jnp.dot(sc_ref[...], fw, precision=hi, preferred_element_type=F32)
    wu = wu_ref[...]
    wp_ref[...] = jnp.dot(wu, a, precision=hi, preferred_element_type=F32)
    wq_ref[...] = jnp.dot(wu, b, precision=hi, preferred_element_type=F32)


def _fold_fourier_weights(w_u, fourier_w, seq):
    d_model = w_u.shape[0]
    groups, gd, _ = fourier_w.shape
    ang = 2.0 * np.pi * np.outer(np.arange(gd), np.arange(gd)) / gd
    scale = 1.0 / math.sqrt(seq * gd)
    cc = jnp.asarray(np.cos(ang) * scale, F32)
    sc = jnp.asarray(np.sin(ang) * scale, F32)
    wu_g = w_u.reshape(d_model, groups, gd).transpose(1, 0, 2)
    wp, wq = pl.pallas_call(
        _prep_kernel,
        grid=(groups,),
        in_specs=[pl.BlockSpec((None, d_model, gd), lambda g: (g, 0, 0)),
                  pl.BlockSpec((None, gd, gd), lambda g: (g, 0, 0)),
                  pl.BlockSpec((gd, gd), lambda g: (0, 0)),
                  pl.BlockSpec((gd, gd), lambda g: (0, 0))],
        out_specs=[pl.BlockSpec((None, d_model, gd), lambda g: (g, 0, 0))] * 2,
        out_shape=[jax.ShapeDtypeStruct((groups, d_model, gd), F32)] * 2,
        compiler_params=_cparams(1),
        name="fourier_weight_fold",
    )(wu_g, fourier_w, cc, sc)
    unfold = lambda w: w.transpose(1, 0, 2).reshape(d_model, groups * gd)
    return unfold(wp), unfold(wq)


def _inproj_kernel(x_ref, g_ref, w_ref, perm_ref, fperm_ref, q_ref, k_ref, v_ref, qr_ref, kr_ref, vr_ref,
                   pq_ref, planes_ref, *, n_hp, fw):
    h = _rms(x_ref[...], g_ref[...]).astype(BF16)
    tm = h.shape[0]
    perm = perm_ref[...]

    def proj(c0, n):
        return jnp.dot(h, w_ref[:, c0:c0 + n], preferred_element_type=F32)

    aw = n_hp * LANES
    pairs = ((q_ref, qr_ref), (k_ref, kr_ref), (v_ref, vr_ref))
    planes_ref[...] = proj(3 * aw, 2 * fw).astype(BF16)
    for t, (tok_ref, _) in enumerate(pairs):
        for c in range(n_hp // 2):
            res = proj(t * aw + c * 2 * LANES, 2 * LANES).astype(BF16)
            tok_ref[2 * c] = res[:, :LANES]
            tok_ref[2 * c + 1] = res[:, LANES:]
    planes = jnp.dot(fperm_ref[...], planes_ref[...], preferred_element_type=F32)
    planes = planes.reshape(FFT_RADIX, tm // FFT_RADIX, 2 * fw)
    for t in range(2):
        pq_ref[t] = planes[:, :, t * fw:(t + 1) * fw]
    for tok_ref, res_ref in pairs:
        for c in range(n_hp // 2):
            for g in range(tm // GROUP):
                tok = jnp.concatenate([tok_ref[2 * c, g * GROUP:(g + 1) * GROUP, :],
                                       tok_ref[2 * c + 1, g * GROUP:(g + 1) * GROUP, :]], axis=1)
                rows = jnp.dot(perm, tok, preferred_element_type=F32).astype(BF16)
                for half in range(2):
                    blk = rows[:, half * LANES:(half + 1) * LANES].reshape(MAX_DIL, BF16_ROWS, LANES)
                    res_ref[2 * c + half, :, g * BF16_ROWS:(g + 1) * BF16_ROWS, :] = blk


def _inproj(x2d, gain, w_all, perm, batch, seq, n_hp, fw, tm):
    n_tok, d_model = x2d.shape
    nt = seq // tm
    tok_map = lambda i: (i // nt, 0, i % nt, 0)
    res_map = lambda i: (i // nt, 0, 0, i % nt, 0)
    tok_shape = jax.ShapeDtypeStruct((batch, n_hp, seq, LANES), BF16)
    res_shape = jax.ShapeDtypeStruct((batch, n_hp, MAX_DIL, seq // MAX_DIL, LANES), BF16)
    R = FFT_RADIX
    t = np.arange(tm)
    fperm = np.zeros((tm, tm), np.float32)
    fperm[(t % R) * (tm // R) + t // R, t] = 1.0
    fperm = jnp.asarray(fperm, BF16)
    const = lambda shape: pl.BlockSpec(shape, lambda i: (0,) * len(shape), pipeline_mode=pl.Buffered(1))
    return pl.pallas_call(
        functools.partial(_inproj_kernel, n_hp=n_hp, fw=fw),
        grid=(n_tok // tm,),
        in_specs=[pl.BlockSpec((tm, d_model), lambda i: (i, 0)),
                  const((1, d_model)), const(w_all.shape), const(perm.shape), const(fperm.shape)],
        out_specs=[pl.BlockSpec((None, n_hp, tm, LANES), tok_map)] * 3
        + [pl.BlockSpec((None, n_hp, MAX_DIL, tm // MAX_DIL, LANES), res_map)] * 3
        + [pl.BlockSpec((None, 2, R, tm // R, fw), res_map)],
        out_shape=[tok_shape] * 3 + [res_shape] * 3
        + [jax.ShapeDtypeStruct((batch, 2, R, seq // R, fw), F32)],
        scratch_shapes=[pltpu.VMEM((tm, 2 * fw), BF16)],
        compiler_params=_cparams(1),
        name="rmsnorm_inproj",
    )(x2d, gain, w_all, perm, fperm)


def _t5_bucket_static(rel, dtype):
    nb = N_REL_BUCKETS // 2
    max_exact = nb // 2
    n = np.abs(rel)
    nf = np.maximum(n, 1).astype(dtype)
    large = max_exact + (np.log(nf / dtype(max_exact)) / dtype(math.log(REL_MAX_DISTANCE / max_exact))
                         * dtype(nb - max_exact)).astype(np.int32)
    large = np.minimum(large, nb - 1)
    return np.where(rel > 0, nb, 0) + np.where(n < max_exact, n, large)


def _bucket_tiles(dilations):
    qi = np.arange(TQ)[:, None]
    kc = np.arange(TK)[None, :]
    offsets = np.array([0, HALF_WINDOW, 2 * HALF_WINDOW])
    rel = kc[None] - offsets[:, None, None] - qi[None]
    tiles = []
    for d in dilations:
        bkt = _t5_bucket_static(rel * d, np.float32)
        assert np.array_equal(bkt, _t5_bucket_static(rel * d, np.float64))
        tile = np.where(np.abs(rel) <= HALF_WINDOW, bkt, -1)
        pieces = MAX_DIL // d if 1 < d < MAX_DIL else 1
        q_order = (np.arange(TQ) % (TQ // pieces)) * pieces + np.arange(TQ) // (TQ // pieces)
        k_order = (np.arange(TK) % (TK // pieces)) * pieces + np.arange(TK) // (TK // pieces)
        tiles.append(tile[:, q_order][:, :, k_order])
    return np.concatenate(tiles, axis=0).astype(np.int32)


def _bias_kernel(table_ref, bkt_ref, out_ref, *, n_heads):
    bkt = bkt_ref[...]
    for h in range(n_heads):
        acc = jnp.full(bkt.shape, NEG_INF, F32)
        for b in range(N_REL_BUCKETS):
            acc = jnp.where(bkt == b, table_ref[b, h], acc)
        out_ref[h] = acc * LOG2E


def _bias_tiles(rel_table, dilations):
    n_heads = rel_table.shape[1]
    bkt = jnp.asarray(_bucket_tiles(dilations))
    return pl.pallas_call(
        functools.partial(_bias_kernel, n_heads=n_heads),
        grid=(bkt.shape[0],),
        in_specs=[pl.BlockSpec(memory_space=pltpu.SMEM),
                  pl.BlockSpec((None, TQ, TK), lambda t: (t, 0, 0))],
        out_specs=pl.BlockSpec((None, n_heads, TQ, TK), lambda t: (t, 0, 0, 0)),
        out_shape=jax.ShapeDtypeStruct((bkt.shape[0], n_heads, TQ, TK), F32),
        compiler_params=_cparams(1),
        name="rel_bias_tiles",
    )(rel_table.astype(F32), bkt)


def _block_attention(qb, kb, vb, bias2, first_head):
    nt_dims = (((1,), (1,)), ((), ()))
    zero = jnp.zeros_like(qb)
    q2 = jnp.concatenate([jnp.where(first_head, qb, zero), jnp.where(first_head, zero, qb)], axis=0)
    s = lax.dot_general(q2, kb, nt_dims, preferred_element_type=F32) + bias2
    m = jnp.max(s, axis=-1, keepdims=True)
    p = jnp.exp2(s - m)
    l = jnp.sum(p, axis=-1, keepdims=True)
    pv = jnp.dot(p.astype(BF16), vb, preferred_element_type=F32)
    num = jnp.where(first_head, pv[:TQ], pv[TQ:])
    den = jnp.where(first_head, l[:TQ], l[TQ:])
    top = jnp.where(first_head, m[:TQ], m[TQ:])
    return num * (1.0 / den), top + jnp.log2(den)


def _merge(o_a, lse_a, o_b, lse_b):
    top = jnp.maximum(lse_a, lse_b)
    w_a = jnp.exp2(lse_a - top)
    w_b = jnp.exp2(lse_b - top)
    den = w_a + w_b
    return (w_a * o_a + w_b * o_b) * (1.0 / den), top + jnp.log2(den)


def _split_f32(x):
    hi = x.astype(BF16)
    return hi, (x - hi.astype(F32)).astype(BF16)


def _attn_kernel(*refs, dilation, merge_in, split_out):
    q_ref, k_ref, v_ref, bias_ref = refs[:4]
    pos = 4
    if merge_in is not None:
        po_ref, pl_ref = refs[pos:pos + 2]
        pos += 2
    o_ref = refs[pos]
    lse_refs = refs[pos + 1:]
    pieces = MAX_DIL // dilation if dilation > 1 else 1
    sub_len = q_ref.shape[-2] * pieces
    n_blk = sub_len // TQ
    tq_p, tk_p = TQ // pieces, TK // pieces
    first_head = lax.broadcasted_iota(jnp.int32, (TQ, LANES), 1) < HEAD_DIM

    def load(ref, r, row, n_rows):
        if dilation == 1:
            return ref[pl.ds(row, n_rows), :]
        parts = [ref[a * dilation + r, pl.ds(row, n_rows), :] for a in range(pieces)]
        return parts[0] if pieces == 1 else jnp.concatenate(parts, axis=0)

    def store(ref, r, row, n_rows, val):
        if dilation == 1:
            ref[pl.ds(row, n_rows), :] = val
        else:
            for a in range(pieces):
                ref[a * dilation + r, pl.ds(row, n_rows), :] = val[a * n_rows:(a + 1) * n_rows]

    def block(n, carry):
        q_row = pl.multiple_of(n * tq_p, tq_p)
        k_row = pl.multiple_of(jnp.clip(n * tq_p - tk_p // 4, 0, sub_len // pieces - tk_p), tk_p // 4)
        edge = jnp.where(n > 0, 1, 0) + jnp.where(n == n_blk - 1, 1, 0)
        for r in range(dilation):
            qb = load(q_ref, r, q_row, tq_p)
            kb = load(k_ref, r, k_row, tk_p)
            vb = load(v_ref, r, k_row, tk_p)
            o, lse = _block_attention(qb, kb, vb, bias_ref[edge].reshape(2 * TQ, TK), first_head)
            if merge_in is not None:
                o, lse = _merge(o, lse, load(po_ref, r, q_row, tq_p).astype(F32), load(pl_ref, r, q_row, tq_p))
            store(o_ref, r, q_row, tq_p, o.astype(BF16))
            if split_out:
                hi, lo = _split_f32(lse)
                store(lse_refs[0], r, q_row, tq_p, hi)
                store(lse_refs[1], r, q_row, tq_p, lo)
            elif lse_refs:
                store(lse_refs[0], r, q_row, tq_p, lse)
        return carry

    lax.fori_loop(0, n_blk, block, 0, unroll=max(1, BLOCKS_IN_FLIGHT // dilation))


def _attn_token_kernel(q_ref, k_ref, v_ref, bias_ref, perm_ref, po_ref, hi_ref, lo_ref, o_ref,
                       o_tok, lse_tok):
    perm = perm_ref[...]
    for g in range(po_ref.shape[1] // BF16_ROWS):
        rows = slice(g * BF16_ROWS, (g + 1) * BF16_ROWS)
        o_hi = jnp.concatenate([po_ref[:, rows, :].reshape(GROUP, LANES),
                                hi_ref[:, rows, :].reshape(GROUP, LANES)], axis=1)
        tok = jnp.dot(perm, o_hi, preferred_element_type=F32)
        lo = jnp.dot(perm, lo_ref[:, rows, :].reshape(GROUP, LANES), preferred_element_type=F32)
        o_tok[g * GROUP:(g + 1) * GROUP, :] = tok[:, :LANES].astype(BF16)
        lse_tok[g * GROUP:(g + 1) * GROUP, :] = tok[:, LANES:] + lo
    _attn_kernel(q_ref, k_ref, v_ref, bias_ref, o_tok, lse_tok, o_ref,
                 dilation=1, merge_in='token', split_out=False)


def _attention(q, k, v, qr, kr, vr, bias, perm, dilations):
    batch, n_hp, seq, _ = q.shape
    assert dilations[0] == MAX_DIL and dilations[-1] == 1
    res_slab = pl.BlockSpec((None, None, MAX_DIL, seq // MAX_DIL, LANES), lambda b, h: (b, h, 0, 0, 0))
    tok_slab = pl.BlockSpec((None, None, seq, LANES), lambda b, h: (b, h, 0, 0))
    res_shape = lambda dt: jax.ShapeDtypeStruct((batch, n_hp, MAX_DIL, seq // MAX_DIL, LANES), dt)
    bias_spec = lambda i: pl.BlockSpec((3, 2, TQ, TK), lambda b, h: (i, h, 0, 0))

    merged = None
    for i, d in enumerate(dilations[:-1]):
        final_res = i == len(dilations) - 2
        args = [qr, kr, vr, bias] + (list(merged) if merged else [])
        in_specs = [res_slab] * 3 + [bias_spec(i)] + ([res_slab] * 2 if merged else [])
        out_shape = [res_shape(BF16)] + ([res_shape(BF16)] * 2 if final_res else [res_shape(F32)])
        merged = pl.pallas_call(
            functools.partial(_attn_kernel, dilation=d, merge_in='f32' if merged else None,
                              split_out=final_res),
            grid=(batch, n_hp),
            in_specs=in_specs,
            out_specs=[res_slab] * len(out_shape),
            out_shape=out_shape,
            compiler_params=_cparams(2),
            name=f"dilated_attn_d{d}",
        )(*args)

    return pl.pallas_call(
        _attn_token_kernel,
        grid=(batch, n_hp),
        in_specs=[tok_slab] * 3 + [bias_spec(len(dilations) - 1), pl.BlockSpec(perm.shape, lambda b, h: (0, 0))]
        + [res_slab] * 3,
        out_specs=tok_slab,
        out_shape=jax.ShapeDtypeStruct((batch, n_hp, seq, LANES), BF16),
        scratch_shapes=[pltpu.VMEM((seq, LANES), BF16), pltpu.VMEM((seq, LANES), F32)],
        compiler_params=_cparams(2),
        name="dilated_attn_d1",
    )(q, k, v, bias, perm, *merged)


def _swap_16x16(mats, perm):
    out = []
    for g in range(mats[0].shape[0] // BF16_ROWS):
        rows = jnp.concatenate([m[g * BF16_ROWS:(g + 1) * BF16_ROWS] for m in mats], axis=0)
        swapped = jnp.dot(perm, rows, preferred_element_type=F32).astype(BF16)
        out.append(swapped.reshape(BF16_ROWS, MAX_DIL, rows.shape[-1]))
    return out


def _fft_stage1_kernel(w_ref, perm_ref, v_ref, z_ref):
    w = w_ref[...]
    perm = perm_ref[...]
    R = v_ref.shape[2]
    zs = []
    for j in range(v_ref.shape[1]):
        rhs = jnp.concatenate([v_ref[0, j], v_ref[1, j]], axis=0).astype(BF16)
        zs.append(jnp.dot(w, rhs, preferred_element_type=F32).astype(BF16))
    for plane in range(2):
        groups = _swap_16x16([z[plane * R:(plane + 1) * R] for z in zs], perm)
        for g, blk in enumerate(groups):
            z_ref[g * BF16_ROWS:(g + 1) * BF16_ROWS, plane, :, :] = blk


def _fft_stage2_kernel(m_ref, perm_ref, z_ref, b_ref, o_ref):
    perm = perm_ref[...]
    xs = []
    for j in range(z_ref.shape[0]):
        zc = z_ref[j].reshape(2 * z_ref.shape[2], z_ref.shape[3])
        xs.append((jnp.dot(m_ref[j], zc, preferred_element_type=F32) + b_ref[...]).astype(BF16))
    for g, blk in enumerate(_swap_16x16(xs, perm)):
        o_ref[g * BF16_ROWS:(g + 1) * BF16_ROWS, :, :] = blk


def _fourier(pq, fourier_b, perm):
    batch, _, R, _, fw = pq.shape
    seq = R * R
    assert R == FFT_RADIX and R % BF16_ROWS == 0 and MAX_DIL == BF16_ROWS
    i = np.arange(R)
    ang1 = 2.0 * np.pi * np.outer(i, i) / R
    c1, s1 = np.cos(ang1), np.sin(ang1)
    w_cat = jnp.asarray(np.block([[c1, -s1], [-s1, -c1]]), BF16)
    k_all = i[:, None, None] + R * i[None, :, None]
    ang2 = 2.0 * np.pi * ((k_all * i[None, None, :]) % seq) / seq
    m_cat = jnp.asarray(np.concatenate([np.cos(ang2), np.sin(ang2)], axis=-1), BF16)
    grp = BF16_ROWS
    const2 = lambda shape: pl.BlockSpec(shape, lambda b, g: (0,) * len(shape))

    z = pl.pallas_call(
        _fft_stage1_kernel,
        grid=(batch, R // grp),
        in_specs=[const2((2 * R, 2 * R)), const2(perm.shape),
                  pl.BlockSpec((None, 2, grp, R, fw), lambda b, g: (b, 0, g, 0, 0))],
        out_specs=pl.BlockSpec((None, R, 2, grp, fw), lambda b, g: (b, 0, 0, g, 0)),
        out_shape=jax.ShapeDtypeStruct((batch, R, 2, R, fw), BF16),
        compiler_params=_cparams(2),
        name="fft_stage1",
    )(w_cat, perm, pq)

    out = pl.pallas_call(
        _fft_stage2_kernel,
        grid=(batch, R // grp),
        in_specs=[pl.BlockSpec((grp, R, 2 * R), lambda b, g: (g, 0, 0)), const2(perm.shape),
                  pl.BlockSpec((None, grp, 2, R, fw), lambda b, g: (b, g, 0, 0, 0)),
                  const2((1, fw))],
        out_specs=pl.BlockSpec((None, R, None, grp, fw), lambda b, g: (b, 0, g, 0, 0)),
        out_shape=jax.ShapeDtypeStruct((batch, R, R // grp, grp, fw), BF16),
        compiler_params=_cparams(2),
        name="fft_stage2",
    )(m_cat, perm, z, fourier_b.reshape(1, fw).astype(F32))
    return out.reshape(batch, seq, fw)


def _mix_ffn_kernel(ap_ref, a_ref, an_ref, fp_ref, f_ref, fn_ref, xp_ref, x_ref, xn_ref,
                    ga_ref, gf_ref, wo_ref, g_ref, wg_ref, wv_ref, cw_ref, cb_ref, wd_ref, gfin_ref,
                    o_ref, act_ref, *, tiles_per_seq, fc, final_norm, n_hp):
    i = pl.program_id(0)
    tm = x_ref.shape[0]
    n_ext = tm + 2 * HALO
    ext = lambda prev, cur, nxt: jnp.concatenate([prev, cur, nxt], axis=0)

    parts = [ext(ap_ref[j], a_ref[j], an_ref[j]).astype(F32) for j in range(n_hp)]
    ssq = functools.reduce(lambda a, b: a + b,
                           [jnp.sum(p * p, axis=-1, keepdims=True) for p in parts])
    inv = lax.rsqrt(ssq / (n_hp * LANES) + EPS)
    ga = ga_ref[...]
    cols = [(p * inv * ga[:, j * LANES:(j + 1) * LANES]).astype(BF16) for j, p in enumerate(parts)]
    cols.append(_rms(ext(fp_ref[...], f_ref[...], fn_ref[...]).astype(F32), gf_ref[...]).astype(BF16))
    mixed = jnp.concatenate(cols, axis=1)
    x_ext = ext(xp_ref[...], x_ref[...], xn_ref[...]) + jnp.dot(mixed, wo_ref[...],
                                                                 preferred_element_type=F32)
    x = x_ext[HALO:HALO + tm]

    keep_prev = (i % tiles_per_seq != 0).astype(F32)
    keep_next = (i % tiles_per_seq != tiles_per_seq - 1).astype(F32)
    row = lax.broadcasted_iota(jnp.int32, (n_ext, 1), 0)
    keep = jnp.where(row < HALO, keep_prev, jnp.where(row >= HALO + tm, keep_next, 1.0))
    h_ext = (_rms(x_ext, g_ref[...]) * keep).astype(BF16)
    h = h_ext[HALO:HALO + tm]
    d_ff = wg_ref.shape[1]

    def up(c0):
        cs = slice(c0, c0 + fc)
        return (jnp.dot(h_ext, wg_ref[:, cs], preferred_element_type=F32),
                jnp.dot(h, wv_ref[:, cs], preferred_element_type=F32))

    nxt = up(0)
    for c0 in range(0, d_ff, fc):
        cs = slice(c0, c0 + fc)
        g, val = nxt
        if c0 + fc < d_ff:
            nxt = up(c0 + fc)
        g_prev = pltpu.roll(g, 1, axis=0)[HALO:HALO + tm]
        g_next = pltpu.roll(g, n_ext - 1, axis=0)[HALO:HALO + tm]
        cw = cw_ref[:, cs]
        conv = cw[0:1] * g_prev + cw[1:2] * g[HALO:HALO + tm] + cw[2:3] * g_next + cb_ref[:, cs]
        act_ref[:, cs] = (conv * (1.0 / (1.0 + jnp.exp(-conv))) * val).astype(BF16)
    y = x + jnp.dot(act_ref[...], wd_ref[...], preferred_element_type=F32)
    o_ref[...] = _rms(y, gfin_ref[...]) if final_norm else y


def _mix_ffn(attn, four, x2d, ga, gf, w_out, gain, wg, wv, conv_w, conv_b, wd, gfin, tm, fc, final_norm):
    batch, n_hp, seq, _ = attn.shape
    fw = four.shape[-1]
    n_tok, d_model = x2d.shape
    d_ff = wg.shape[1]
    nt = seq // tm
    per_tile = tm // HALO
    last = seq // HALO - 1
    prev_blk = lambda i: jnp.maximum((i % nt) * per_tile - 1, 0)
    next_blk = lambda i: jnp.minimum((i % nt + 1) * per_tile, last)
    seq_blk = lambda i: (i // nt) * (last + 1)
    const = lambda shape: pl.BlockSpec(shape, lambda i: (0,) * len(shape), pipeline_mode=pl.Buffered(1))

    def triple(cur_shape, halo_shape, index):
        return [pl.BlockSpec(halo_shape, lambda i: index(i // nt, prev_blk(i))),
                pl.BlockSpec(cur_shape, lambda i: index(i // nt, i % nt)),
                pl.BlockSpec(halo_shape, lambda i: index(i // nt, next_blk(i)))]

    in_specs = (triple((None, n_hp, tm, LANES), (None, n_hp, HALO, LANES), lambda b, t: (b, 0, t, 0))
                + triple((None, tm, fw), (None, HALO, fw), lambda b, t: (b, t, 0))
                + [pl.BlockSpec((HALO, d_model), lambda i: (seq_blk(i) + prev_blk(i), 0)),
                   pl.BlockSpec((tm, d_model), lambda i: (i, 0)),
                   pl.BlockSpec((HALO, d_model), lambda i: (seq_blk(i) + next_blk(i), 0))]
                + [const((1, n_hp * LANES)), const((1, fw)), const(w_out.shape), const((1, d_model)),
                   const(wg.shape), const(wv.shape), const(conv_w.shape), const((1, d_ff)), const(wd.shape),
                   const((1, d_model))])
    return pl.pallas_call(
        functools.partial(_mix_ffn_kernel, tiles_per_seq=nt, fc=fc, final_norm=final_norm, n_hp=n_hp),
        grid=(n_tok // tm,),
        in_specs=in_specs,
        out_specs=pl.BlockSpec((tm, d_model), lambda i: (i, 0)),
        out_shape=jax.ShapeDtypeStruct((n_tok, d_model), F32),
        scratch_shapes=[pltpu.VMEM((tm, d_ff), BF16)],
        compiler_params=_cparams(1),
        name="outproj_convglu_ffn",
    )(attn, attn, attn, four, four, four, x2d, x2d, x2d, ga, gf, w_out, gain, wg, wv, conv_w, conv_b, wd, gfin)


def kernel(x, norm_mix_gain, w_in, attn_out_gain, rel_bias_table, fourier_w, fourier_b, fourier_out_gain, w_out, norm_ffn_gain, w_gate, w_val, conv_w, conv_b, w_down, final_norm_gain):
    batch, seq, d_model = x.shape
    depth = w_in.shape[0]
    n_heads = rel_bias_table.shape[1]
    attn_w = n_heads * HEAD_DIM
    n_hp = attn_w // LANES
    fw = fourier_w.shape[1] * fourier_w.shape[2]
    assert all(w // (2 * d) == HALF_WINDOW for w, d in DILATED_PATTERNS)
    dilations = sorted((d for _, d in DILATED_PATTERNS), reverse=True)
    tm = 512
    row = lambda g: g.reshape(1, -1).astype(F32)
    perm = _group_permutation()
    bias = _bias_tiles(rel_bias_table, dilations)

    x2d = x.reshape(batch * seq, d_model)
    for layer in range(depth):
        w_l = w_in[layer]
        wp, wq = _fold_fourier_weights(w_l[:, 3 * attn_w:], fourier_w[layer], seq)
        w_all = jnp.concatenate([w_l[:, :attn_w] * (HEAD_DIM ** -0.5 * LOG2E), w_l[:, attn_w:3 * attn_w], wp, wq],
                                axis=1).astype(BF16)
        q, k, v, qr, kr, vr, pq = _inproj(x2d, row(norm_mix_gain[layer]), w_all, perm, batch, seq, n_hp, fw, tm)
        attn = _attention(q, k, v, qr, kr, vr, bias, perm, dilations)
        four = _fourier(pq, fourier_b[layer].reshape(-1), perm)
        x2d = _mix_ffn(attn, four, x2d, row(attn_out_gain[layer]), row(fourier_out_gain[layer]),
                       w_out[layer].astype(BF16), row(norm_ffn_gain[layer]), w_gate[layer].astype(BF16),
                       w_val[layer].astype(BF16), conv_w[layer].astype(F32), row(conv_b[layer]),
                       w_down[layer].astype(BF16), row(final_norm_gain), tm, 256, layer == depth - 1)
    return x2d.reshape(batch, seq, d_model)
```

```python
import functools
import math

import numpy as np
import jax
import jax.numpy as jnp
from jax import lax
from jax.experimental import pallas as pl
from jax.experimental.pallas import tpu as pltpu

EPS = 1e-6
NEG_INF = -1e30
LOG2E = math.log2(math.e)
HEAD_DIM = 64
DILATED_PATTERNS = ((128, 1), (512, 4), (2048, 16))
N_REL_BUCKETS = 32
REL_MAX_DISTANCE = 1024

LANES = 128
BF16_ROWS = 16
HALF_WINDOW = 64
TQ = 2 * HALF_WINDOW
TK = 4 * HALF_WINDOW
MAX_DIL = max(d for _, d in DILATED_PATTERNS)
GROUP = MAX_DIL * BF16_ROWS
BLOCKS_IN_FLIGHT = 32
FFT_RADIX = 64
HALO = BF16_ROWS
VMEM_LIMIT = 56 * 1024 * 1024

BF16 = jnp.bfloat16
F32 = jnp.float32


def _cparams(n_axes):
    return pltpu.CompilerParams(dimension_semantics=("arbitrary",) * n_axes,
                                vmem_limit_bytes=VMEM_LIMIT)


def _rms(x, gain):
    ms = jnp.mean(x * x, axis=-1, keepdims=True)
    return x * lax.rsqrt(ms + EPS) * gain


def _group_permutation():
    t = np.arange(GROUP)
    swapped = (t % MAX_DIL) * BF16_ROWS + t // MAX_DIL
    perm = np.zeros((GROUP, GROUP), np.float32)
    perm[swapped, t] = 1.0
    assert MAX_DIL == BF16_ROWS and np.array_equal(perm, perm.T)
    return jnp.asarray(perm, BF16)


def _prep_kernel(wu_ref, fw_ref, cc_ref, sc_ref, wp_ref, wq_ref):
    hi = lax.Precision.HIGHEST
    fw = fw_ref[...]
    a = jnp.dot(cc_ref[...], fw, precision=hi, preferred_element_type=F32)
    b = jnp.dot(sc_ref[...], fw, precision=hi, preferred_element_type=F32)
    wu = wu_ref[...]
    wp_ref[...] = jnp.dot(wu, a, precision=hi, preferred_element_type=F32)
    wq_ref[...] = jnp.dot(wu, b, precision=hi, preferred_element_type=F32)


def _fold_fourier_weights(w_u, fourier_w, seq):
    d_model = w_u.shape[0]
    groups, gd, _ = fourier_w.shape
    ang = 2.0 * np.pi * np.outer(np.arange(gd), np.arange(gd)) / gd
    scale = 1.0 / math.sqrt(seq * gd)
    cc = jnp.asarray(np.cos(ang) * scale, F32)
    sc = jnp.asarray(np.sin(ang) * scale, F32)
    wu_g = w_u.reshape(d_model, groups, gd).transpose(1, 0, 2)
    wp, wq = pl.pallas_call(
        _prep_kernel,
        grid=(groups,),
        in_specs=[pl.BlockSpec((None, d_model, gd), lambda g: (g, 0, 0)),
                  pl.BlockSpec((None, gd, gd), lambda g: (g, 0, 0)),
                  pl.BlockSpec((gd, gd), lambda g: (0, 0)),
                  pl.BlockSpec((gd, gd), lambda g: (0, 0))],
        out_specs=[pl.BlockSpec((None, d_model, gd), lambda g: (g, 0, 0))] * 2,
        out_shape=[jax.ShapeDtypeStruct((groups, d_model, gd), F32)] * 2,
        compiler_params=_cparams(1),
        name="fourier_weight_fold",
    )(wu_g, fourier_w, cc, sc)
    unfold = lambda w: w.transpose(1, 0, 2).reshape(d_model, groups * gd)
    return unfold(wp), unfold(wq)


def _inproj_kernel(x_ref, g_ref, w_ref, perm_ref, fperm_ref, q_ref, k_ref, v_ref, qr_ref, kr_ref, vr_ref,
                   pq_ref, planes_ref, *, n_hp, fw):
    h = _rms(x_ref[...], g_ref[...]).astype(BF16)
    tm = h.shape[0]
    perm = perm_ref[...]

    def proj(c0, n):
        return jnp.dot(h, w_ref[:, c0:c0 + n], preferred_element_type=F32)

    aw = n_hp * LANES
    pairs = ((q_ref, qr_ref), (k_ref, kr_ref), (v_ref, vr_ref))
    planes_ref[...] = proj(3 * aw, 2 * fw).astype(BF16)
    for t, (tok_ref, _) in enumerate(pairs):
        for c in range(n_hp // 2):
            res = proj(t * aw + c * 2 * LANES, 2 * LANES).astype(BF16)
            tok_ref[2 * c] = res[:, :LANES]
            tok_ref[2 * c + 1] = res[:, LANES:]
    planes = jnp.dot(fperm_ref[...], planes_ref[...], preferred_element_type=F32)
    planes = planes.reshape(FFT_RADIX, tm // FFT_RADIX, 2 * fw)
    for t in range(2):
        pq_ref[t] = planes[:, :, t * fw:(t + 1) * fw]
    for tok_ref, res_ref in pairs:
        for c in range(n_hp // 2):
            for g in range(tm // GROUP):
                tok = jnp.concatenate([tok_ref[2 * c, g * GROUP:(g + 1) * GROUP, :],
                                       tok_ref[2 * c + 1, g * GROUP:(g + 1) * GROUP, :]], axis=1)
                rows = jnp.dot(perm, tok, preferred_element_type=F32).astype(BF16)
                for half in range(2):
                    blk = rows[:, half * LANES:(half + 1) * LANES].reshape(MAX_DIL, BF16_ROWS, LANES)
                    res_ref[2 * c + half, :, g * BF16_ROWS:(g + 1) * BF16_ROWS, :] = blk


def _inproj(x2d, gain, w_all, perm, batch, seq, n_hp, fw, tm):
    n_tok, d_model = x2d.shape
    nt = seq // tm
    tok_map = lambda i: (i // nt, 0, i % nt, 0)
    res_map = lambda i: (i // nt, 0, 0, i % nt, 0)
    tok_shape = jax.ShapeDtypeStruct((batch, n_hp, seq, LANES), BF16)
    res_shape = jax.ShapeDtypeStruct((batch, n_hp, MAX_DIL, seq // MAX_DIL, LANES), BF16)
    R = FFT_RADIX
    t = np.arange(tm)
    fperm = np.zeros((tm, tm), np.float32)
    fperm[(t % R) * (tm // R) + t // R, t] = 1.0
    fperm = jnp.asarray(fperm, BF16)
    const = lambda shape: pl.BlockSpec(shape, lambda i: (0,) * len(shape), pipeline_mode=pl.Buffered(1))
    return pl.pallas_call(
        functools.partial(_inproj_kernel, n_hp=n_hp, fw=fw),
        grid=(n_tok // tm,),
        in_specs=[pl.BlockSpec((tm, d_model), lambda i: (i, 0)),
                  const((1, d_model)), const(w_all.shape), const(perm.shape), const(fperm.shape)],
        out_specs=[pl.BlockSpec((None, n_hp, tm, LANES), tok_map)] * 3
        + [pl.BlockSpec((None, n_hp, MAX_DIL, tm // MAX_DIL, LANES), res_map)] * 3
        + [pl.BlockSpec((None, 2, R, tm // R, fw), res_map)],
        out_shape=[tok_shape] * 3 + [res_shape] * 3
        + [jax.ShapeDtypeStruct((batch, 2, R, seq // R, fw), F32)],
        scratch_shapes=[pltpu.VMEM((tm, 2 * fw), BF16)],
        compiler_params=_cparams(1),
        name="rmsnorm_inproj",
    )(x2d, gain, w_all, perm, fperm)


def _t5_bucket_static(rel, dtype):
    nb = N_REL_BUCKETS // 2
    max_exact = nb // 2
    n = np.abs(rel)
    nf = np.maximum(n, 1).astype(dtype)
    large = max_exact + (np.log(nf / dtype(max_exact)) / dtype(math.log(REL_MAX_DISTANCE / max_exact))
                         * dtype(nb - max_exact)).astype(np.int32)
    large = np.minimum(large, nb - 1)
    return np.where(rel > 0, nb, 0) + np.where(n < max_exact, n, large)


def _bucket_tiles(dilations):
    qi = np.arange(TQ)[:, None]
    kc = np.arange(TK)[None, :]
    offsets = np.array([0, HALF_WINDOW, 2 * HALF_WINDOW])
    rel = kc[None] - offsets[:, None, None] - qi[None]
    tiles = []
    for d in dilations:
        bkt = _t5_bucket_static(rel * d, np.float32)
        assert np.array_equal(bkt, _t5_bucket_static(rel * d, np.float64))
        tile = np.where(np.abs(rel) <= HALF_WINDOW, bkt, -1)
        pieces = MAX_DIL // d if 1 < d < MAX_DIL else 1
        q_order = (np.arange(TQ) % (TQ // pieces)) * pieces + np.arange(TQ) // (TQ // pieces)
        k_order = (np.arange(TK) % (TK // pieces)) * pieces + np.arange(TK) // (TK // pieces)
        tiles.append(tile[:, q_order][:, :, k_order])
    return np.concatenate(tiles, axis=0).astype(np.int32)


def _bias_kernel(table_ref, bkt_ref, out_ref, *, n_heads):
    bkt = bkt_ref[...]
    for h in range(n_heads):
        acc = jnp.full(bkt.shape, NEG_INF, F32)
        for b in range(N_REL_BUCKETS):
            acc = jnp.where(bkt == b, table_ref[b, h], acc)
        out_ref[h] = acc * LOG2E


def _bias_tiles(rel_table, dilations):
    n_heads = rel_table.shape[1]
    bkt = jnp.asarray(_bucket_tiles(dilations))
    return pl.pallas_call(
        functools.partial(_bias_kernel, n_heads=n_heads),
        grid=(bkt.shape[0],),
        in_specs=[pl.BlockSpec(memory_space=pltpu.SMEM),
                  pl.BlockSpec((None, TQ, TK), lambda t: (t, 0, 0))],
        out_specs=pl.BlockSpec((None, n_heads, TQ, TK), lambda t: (t, 0, 0, 0)),
        out_shape=jax.ShapeDtypeStruct((bkt.shape[0], n_heads, TQ, TK), F32),
        compiler_params=_cparams(1),
        name="rel_bias_tiles",
    )(rel_table.astype(F32), bkt)


def _block_attention(qb, kb, vb, bias2, first_head):
    nt_dims = (((1,), (1,)), ((), ()))
    zero = jnp.zeros_like(qb)
    q2 = jnp.concatenate([jnp.where(first_head, qb, zero), jnp.where(first_head, zero, qb)], axis=0)
    s = lax.dot_general(q2, kb, nt_dims, preferred_element_type=F32) + bias2
    m = jnp.max(s, axis=-1, keepdims=True)
    p = jnp.exp2(s - m)
    l = jnp.sum(p, axis=-1, keepdims=True)
    pv = jnp.dot(p.astype(BF16), vb, preferred_element_type=F32)
    num = jnp.where(first_head, pv[:TQ], pv[TQ:])
    den = jnp.where(first_head, l[:TQ], l[TQ:])
    top = jnp.where(first_head, m[:TQ], m[TQ:])
    return num * (1.0 / den), top + jnp.log2(den)


def _merge(o_a, lse_a, o_b, lse_b):
    top = jnp.maximum(lse_a, lse_b)
    w_a = jnp.exp2(lse_a - top)
    w_b = jnp.exp2(lse_b - top)
    den = w_a + w_b
    return (w_a * o_a + w_b * o_b) * (1.0 / den), top + jnp.log2(den)


def _split_f32(x):
    hi = x.astype(BF16)
    return hi, (x - hi.astype(F32)).astype(BF16)


def _attn_kernel(*refs, dilation, merge_in, split_out):
    q_ref, k_ref, v_ref, bias_ref = refs[:4]
    pos = 4
    if merge_in is not None:
        po_ref, pl_ref = refs[pos:pos + 2]
        pos += 2
    o_ref = refs[pos]
    lse_refs = refs[pos + 1:]
    pieces = MAX_DIL // dilation if dilation > 1 else 1
    sub_len = q_ref.shape[-2] * pieces
    n_blk = sub_len // TQ
    tq_p, tk_p = TQ // pieces, TK // pieces
    first_head = lax.broadcasted_iota(jnp.int32, (TQ, LANES), 1) < HEAD_DIM

    def load(ref, r, row, n_rows):
        if dilation == 1:
            return ref[pl.ds(row, n_rows), :]
        parts = [ref[a * dilation + r, pl.ds(row, n_rows), :] for a in range(pieces)]
        return parts[0] if pieces == 1 else jnp.concatenate(parts, axis=0)

    def store(ref, r, row, n_rows, val):
        if dilation == 1:
            ref[pl.ds(row, n_rows), :] = val
        else:
            for a in range(pieces):
                ref[a * dilation + r, pl.ds(row, n_rows), :] = val[a * n_rows:(a + 1) * n_rows]

    def block(n, carry):
        q_row = pl.multiple_of(n * tq_p, tq_p)
        k_row = pl.multiple_of(jnp.clip(n * tq_p - tk_p // 4, 0, sub_len // pieces - tk_p), tk_p // 4)
        edge = jnp.where(n > 0, 1, 0) + jnp.where(n == n_blk - 1, 1, 0)
        for r in range(dilation):
            qb = load(q_ref, r, q_row, tq_p)
            kb = load(k_ref, r, k_row, tk_p)
            vb = load(v_ref, r, k_row, tk_p)
            o, lse = _block_attention(qb, kb, vb, bias_ref[edge].reshape(2 * TQ, TK), first_head)
            if merge_in is not None:
                o, lse = _merge(o, lse, load(po_ref, r, q_row, tq_p).astype(F32), load(pl_ref, r, q_row, tq_p))
            store(o_ref, r, q_row, tq_p, o.astype(BF16))
            if split_out:
                hi, lo = _split_f32(lse)
                store(lse_refs[0], r, q_row, tq_p, hi)
                store(lse_refs[1], r, q_row, tq_p, lo)
            elif lse_refs:
                store(lse_refs[0], r, q_row, tq_p, lse)
        return carry

    lax.fori_loop(0, n_blk, block, 0, unroll=max(1, BLOCKS_IN_FLIGHT // dilation))


def _attn_token_kernel(q_ref, k_ref, v_ref, bias_ref, perm_ref, po_ref, hi_ref, lo_ref, o_ref,
                       o_tok, lse_tok):
    perm = perm_ref[...]
    for g in range(po_ref.shape[1] // BF16_ROWS):
        rows = slice(g * BF16_ROWS, (g + 1) * BF16_ROWS)
        o_hi = jnp.concatenate([po_ref[:, rows, :].reshape(GROUP, LANES),
                                hi_ref[:, rows, :].reshape(GROUP, LANES)], axis=1)
        tok = jnp.dot(perm, o_hi, preferred_element_type=F32)
        lo = jnp.dot(perm, lo_ref[:, rows, :].reshape(GROUP, LANES), preferred_element_type=F32)
        o_tok[g * GROUP:(g + 1) * GROUP, :] = tok[:, :LANES].astype(BF16)
        lse_tok[g * GROUP:(g + 1) * GROUP, :] = tok[:, LANES:] + lo
    _attn_kernel(q_ref, k_ref, v_ref, bias_ref, o_tok, lse_tok, o_ref,
                 dilation=1, merge_in='token', split_out=False)


def _attention(q, k, v, qr, kr, vr, bias, perm, dilations):
    batch, n_hp, seq, _ = q.shape
    assert dilations[0] == MAX_DIL and dilations[-1] == 1
    res_slab = pl.BlockSpec((None, None, MAX_DIL, seq // MAX_DIL, LANES), lambda b, h: (b, h, 0, 0, 0))
    tok_slab = pl.BlockSpec((None, None, seq, LANES), lambda b, h: (b, h, 0, 0))
    res_shape = lambda dt: jax.ShapeDtypeStruct((batch, n_hp, MAX_DIL, seq // MAX_DIL, LANES), dt)
    bias_spec = lambda i: pl.BlockSpec((3, 2, TQ, TK), lambda b, h: (i, h, 0, 0))

    merged = None
    for i, d in enumerate(dilations[:-1]):
        final_res = i == len(dilations) - 2
        args = [qr, kr, vr, bias] + (list(merged) if merged else [])
        in_specs = [res_slab] * 3 + [bias_spec(i)] + ([res_slab] * 2 if merged else [])
        out_shape = [res_shape(BF16)] + ([res_shape(BF16)] * 2 if final_res else [res_shape(F32)])
        merged = pl.pallas_call(
            functools.partial(_attn_kernel, dilation=d, merge_in='f32' if merged else None,
                              split_out=final_res),
            grid=(batch, n_hp),
            in_specs=in_specs,
            out_specs=[res_slab] * len(out_shape),
            out_shape=out_shape,
            compiler_params=_cparams(2),
            name=f"dilated_attn_d{d}",
        )(*args)

    return pl.pallas_call(
        _attn_token_kernel,
        grid=(batch, n_hp),
        in_specs=[tok_slab] * 3 + [bias_spec(len(dilations) - 1), pl.BlockSpec(perm.shape, lambda b, h: (0, 0))]
        + [res_slab] * 3,
        out_specs=tok_slab,
        out_shape=jax.ShapeDtypeStruct((batch, n_hp, seq, LANES), BF16),
        scratch_shapes=[pltpu.VMEM((seq, LANES), BF16), pltpu.VMEM((seq, LANES), F32)],
        compiler_params=_cparams(2),
        name="dilated_attn_d1",
    )(q, k, v, bias, perm, *merged)


def _swap_16x16(mats, perm):
    out = []
    for g in range(mats[0].shape[0] // BF16_ROWS):
        rows = jnp.concatenate([m[g * BF16_ROWS:(g + 1) * BF16_ROWS] for m in mats], axis=0)
        swapped = jnp.dot(perm, rows, preferred_element_type=F32).astype(BF16)
        out.append(swapped.reshape(BF16_ROWS, MAX_DIL, rows.shape[-1]))
    return out


def _fft_stage1_kernel(w_ref, perm_ref, v_ref, z_ref):
    w = w_ref[...]
    perm = perm_ref[...]
    R = v_ref.shape[2]
    zs = []
    for j in range(v_ref.shape[1]):
        rhs = jnp.concatenate([v_ref[0, j], v_ref[1, j]], axis=0).astype(BF16)
        zs.append(jnp.dot(w, rhs, preferred_element_type=F32).astype(BF16))
    for plane in range(2):
        groups = _swap_16x16([z[plane * R:(plane + 1) * R] for z in zs], perm)
        for g, blk in enumerate(groups):
            z_ref[g * BF16_ROWS:(g + 1) * BF16_ROWS, plane, :, :] = blk


def _fft_stage2_kernel(m_ref, perm_ref, z_ref, b_ref, o_ref):
    perm = perm_ref[...]
    xs = []
    for j in range(z_ref.shape[0]):
        zc = z_ref[j].reshape(2 * z_ref.shape[2], z_ref.shape[3])
        xs.append((jnp.dot(m_ref[j], zc, preferred_element_type=F32) + b_ref[...]).astype(BF16))
    for g, blk in enumerate(_swap_16x16(xs, perm)):
        o_ref[g * BF16_ROWS:(g + 1) * BF16_ROWS, :, :] = blk


def _fourier(pq, fourier_b, perm):
    batch, _, R, _, fw = pq.shape
    seq = R * R
    assert R == FFT_RADIX and R % BF16_ROWS == 0 and MAX_DIL == BF16_ROWS
    i = np.arange(R)
    ang1 = 2.0 * np.pi * np.outer(i, i) / R
    c1, s1 = np.cos(ang1), np.sin(ang1)
    w_cat = jnp.asarray(np.block([[c1, -s1], [-s1, -c1]]), BF16)
    k_all = i[:, None, None] + R * i[None, :, None]
    ang2 = 2.0 * np.pi * ((k_all * i[None, None, :]) % seq) / seq
    m_cat = jnp.asarray(np.concatenate([np.cos(ang2), np.sin(ang2)], axis=-1), BF16)
    grp = BF16_ROWS
    const2 = lambda shape: pl.BlockSpec(shape, lambda b, g: (0,) * len(shape))

    z = pl.pallas_call(
        _fft_stage1_kernel,
        grid=(batch, R // grp),
        in_specs=[const2((2 * R, 2 * R)), const2(perm.shape),
                  pl.BlockSpec((None, 2, grp, R, fw), lambda b, g: (b, 0, g, 0, 0))],
        out_specs=pl.BlockSpec((None, R, 2, grp, fw), lambda b, g: (b, 0, 0, g, 0)),
        out_shape=jax.ShapeDtypeStruct((batch, R, 2, R, fw), BF16),
        compiler_params=_cparams(2),
        name="fft_stage1",
    )(w_cat, perm, pq)

    out = pl.pallas_call(
        _fft_stage2_kernel,
        grid=(batch, R // grp),
        in_specs=[pl.BlockSpec((grp, R, 2 * R), lambda b, g: (g, 0, 0)), const2(perm.shape),
                  pl.BlockSpec((None, grp, 2, R, fw), lambda b, g: (b, g, 0, 0, 0)),
                  const2((1, fw))],
        out_specs=pl.BlockSpec((None, R, None, grp, fw), lambda b, g: (b, 0, g, 0, 0)),
        out_shape=jax.ShapeDtypeStruct((batch, R, R // grp, grp, fw), BF16),
        compiler_params=_cparams(2),
        name="fft_stage2",
    )(m_cat, perm, z, fourier_b.reshape(1, fw).astype(F32))
    return out.reshape(batch, seq, fw)


def _mix_ffn_kernel(ap_ref, a_ref, an_ref, fp_ref, f_ref, fn_ref, xp_ref, x_ref, xn_ref,
                    ga_ref, gf_ref, wo_ref, g_ref, wg_ref, wv_ref, cw_ref, cb_ref, wd_ref, gfin_ref,
                    o_ref, act_ref, *, tiles_per_seq, fc, final_norm, n_hp):
    i = pl.program_id(0)
    tm = x_ref.shape[0]
    n_ext = tm + 2 * HALO
    ext = lambda prev, cur, nxt: jnp.concatenate([prev, cur, nxt], axis=0)

    parts = [ext(ap_ref[j], a_ref[j], an_ref[j]).astype(F32) for j in range(n_hp)]
    ssq = functools.reduce(lambda a, b: a + b,
                           [jnp.sum(p * p, axis=-1, keepdims=True) for p in parts])
    inv = lax.rsqrt(ssq / (n_hp * LANES) + EPS)
    ga = ga_ref[...]
    cols = [(p * inv * ga[:, j * LANES:(j + 1) * LANES]).astype(BF16) for j, p in enumerate(parts)]
    cols.append(_rms(ext(fp_ref[...], f_ref[...], fn_ref[...]).astype(F32), gf_ref[...]).astype(BF16))
    mixed = jnp.concatenate(cols, axis=1)
    x_ext = ext(xp_ref[...], x_ref[...], xn_ref[...]) + jnp.dot(mixed, wo_ref[...],
                                                                 preferred_element_type=F32)
    x = x_ext[HALO:HALO + tm]

    keep_prev = (i % tiles_per_seq != 0).astype(F32)
    keep_next = (i % tiles_per_seq != tiles_per_seq - 1).astype(F32)
    row = lax.broadcasted_iota(jnp.int32, (n_ext, 1), 0)
    keep = jnp.where(row < HALO, keep_prev, jnp.where(row >= HALO + tm, keep_next, 1.0))
    h_ext = (_rms(x_ext, g_ref[...]) * keep).astype(BF16)
    h = h_ext[HALO:HALO + tm]
    d_ff = wg_ref.shape[1]

    def up(c0):
        cs = slice(c0, c0 + fc)
        return (jnp.dot(h_ext, wg_ref[:, cs], preferred_element_type=F32),
                jnp.dot(h, wv_ref[:, cs], preferred_element_type=F32))

    nxt = up(0)
    for c0 in range(0, d_ff, fc):
        cs = slice(c0, c0 + fc)
        g, val = nxt
        if c0 + fc < d_ff:
            nxt = up(c0 + fc)
        g_prev = pltpu.roll(g, 1, axis=0)[HALO:HALO + tm]
        g_next = pltpu.roll(g, n_ext - 1, axis=0)[HALO:HALO + tm]
        cw = cw_ref[:, cs]
        conv = cw[0:1] * g_prev + cw[1:2] * g[HALO:HALO + tm] + cw[2:3] * g_next + cb_ref[:, cs]
        act_ref[:, cs] = (conv * (1.0 / (1.0 + jnp.exp(-conv))) * val).astype(BF16)
    y = x + jnp.dot(act_ref[...], wd_ref[...], preferred_element_type=F32)
    o_ref[...] = _rms(y, gfin_ref[...]) if final_norm else y


def _mix_ffn(attn, four, x2d, ga, gf, w_out, gain, wg, wv, conv_w, conv_b, wd, gfin, tm, fc, final_norm):
    batch, n_hp, seq, _ = attn.shape
    fw = four.shape[-1]
    n_tok, d_model = x2d.shape
    d_ff = wg.shape[1]
    nt = seq // tm
    per_tile = tm // HALO
    last = seq // HALO - 1
    prev_blk = lambda i: jnp.maximum((i % nt) * per_tile - 1, 0)
    next_blk = lambda i: jnp.minimum((i % nt + 1) * per_tile, last)
    seq_blk = lambda i: (i // nt) * (last + 1)
    const = lambda shape: pl.BlockSpec(shape, lambda i: (0,) * len(shape), pipeline_mode=pl.Buffered(1))

    def triple(cur_shape, halo_shape, index):
        return [pl.BlockSpec(halo_shape, lambda i: index(i // nt, prev_blk(i))),
                pl.BlockSpec(cur_shape, lambda i: index(i // nt, i % nt)),
                pl.BlockSpec(halo_shape, lambda i: index(i // nt, next_blk(i)))]

    in_specs = (triple((None, n_hp, tm, LANES), (None, n_hp, HALO, LANES), lambda b, t: (b, 0, t, 0))
                + triple((None, tm, fw), (None, HALO, fw), lambda b, t: (b, t, 0))
                + [pl.BlockSpec((HALO, d_model), lambda i: (seq_blk(i) + prev_blk(i), 0)),
                   pl.BlockSpec((tm, d_model), lambda i: (i, 0)),
                   pl.BlockSpec((HALO, d_model), lambda i: (seq_blk(i) + next_blk(i), 0))]
                + [const((1, n_hp * LANES)), const((1, fw)), const(w_out.shape), const((1, d_model)),
                   const(wg.shape), const(wv.shape), const(conv_w.shape), const((1, d_ff)), const(wd.shape),
                   const((1, d_model))])
    return pl.pallas_call(
        functools.partial(_mix_ffn_kernel, tiles_per_seq=nt, fc=fc, final_norm=final_norm, n_hp=n_hp),
        grid=(n_tok // tm,),
        in_specs=in_specs,
        out_specs=pl.BlockSpec((tm, d_model), lambda i: (i, 0)),
        out_shape=jax.ShapeDtypeStruct((n_tok, d_model), F32),
        scratch_shapes=[pltpu.VMEM((tm, d_ff), BF16)],
        compiler_params=_cparams(1),
        name="outproj_convglu_ffn",
    )(attn, attn, attn, four, four, four, x2d, x2d, x2d, ga, gf, w_out, gain, wg, wv, conv_w, conv_b, wd, gfin)


def kernel(x, norm_mix_gain, w_in, attn_out_gain, rel_bias_table, fourier_w, fourier_b, fourier_out_gain, w_out, norm_ffn_gain, w_gate, w_val, conv_w, conv_b, w_down, final_norm_gain):
    batch, seq, d_model = x.shape
    depth = w_in.shape[0]
    n_heads = rel_bias_table.shape[1]
    attn_w = n_heads * HEAD_DIM
    n_hp = attn_w // LANES
    fw = fourier_w.shape[1] * fourier_w.shape[2]
    assert all(w // (2 * d) == HALF_WINDOW for w, d in DILATED_PATTERNS)
    dilations = sorted((d for _, d in DILATED_PATTERNS), reverse=True)
    tm = 512
    row = lambda g: g.reshape(1, -1).astype(F32)
    perm = _group_permutation()
    bias = _bias_tiles(rel_bias_table, dilations)

    x2d = x.reshape(batch * seq, d_model)
    for layer in range(depth):
        w_l = w_in[layer]
        wp, wq = _fold_fourier_weights(w_l[:, 3 * attn_w:], fourier_w[layer], seq)
        w_all = jnp.concatenate([w_l[:, :attn_w] * (HEAD_DIM ** -0.5 * LOG2E), w_l[:, attn_w:3 * attn_w], wp, wq],
                                axis=1).astype(BF16)
        q, k, v, qr, kr, vr, pq = _inproj(x2d, row(norm_mix_gain[layer]), w_all, perm, batch, seq, n_hp, fw, tm)
        attn = _attention(q, k, v, qr, kr, vr, bias, perm, dilations)
        four = _fourier(pq, fourier_b[layer].reshape(-1), perm)
        x2d = _mix_ffn(attn, four, x2d, row(attn_out_gain[layer]), row(fourier_out_gain[layer]),
                       w_out[layer].astype(BF16), row(norm_ffn_gain[layer]), w_gate[layer].astype(BF16),
                       w_val[layer].astype(BF16), conv_w[layer].astype(F32), row(conv_b[layer]),
                       w_down[layer].astype(BF16), row(final_norm_gain), tm, 256, layer == depth - 1)
    return x2d.reshape(batch, seq, d_model)
```

```python
import functools
import math

import numpy as np
import jax
import jax.numpy as jnp
from jax import lax
from jax.experimental import pallas as pl
from jax.experimental.pallas import tpu as pltpu

EPS = 1e-6
NEG_INF = -1e30
LOG2E = math.log2(math.e)
HEAD_DIM = 64
DILATED_PATTERNS = ((128, 1), (512, 4), (2048, 16))
N_REL_BUCKETS = 32
REL_MAX_DISTANCE = 1024

LANES = 128
BF16_ROWS = 16
HALF_WINDOW = 64
TQ = 2 * HALF_WINDOW
TK = 4 * HALF_WINDOW
MAX_DIL = max(d for _, d in DILATED_PATTERNS)
GROUP = MAX_DIL * BF16_ROWS
BLOCKS_IN_FLIGHT = 32
SLABS_PER_STEP = 2
FFT_RADIX = 64
HALO = BF16_ROWS
VMEM_LIMIT = 56 * 1024 * 1024

BF16 = jnp.bfloat16
F32 = jnp.float32


def _cparams(n_axes):
    return pltpu.CompilerParams(dimension_semantics=("arbitrary",) * n_axes,
                                vmem_limit_bytes=VMEM_LIMIT)


def _rms(x, gain):
    ms = jnp.mean(x * x, axis=-1, keepdims=True)
    return x * lax.rsqrt(ms + EPS) * gain


def _group_permutation():
    t = np.arange(GROUP)
    swapped = (t % MAX_DIL) * BF16_ROWS + t // MAX_DIL
    perm = np.zeros((GROUP, GROUP), np.float32)
    perm[swapped, t] = 1.0
    assert MAX_DIL == BF16_ROWS and np.array_equal(perm, perm.T)
    return jnp.asarray(perm, BF16)


def _prep_kernel(wu_ref, fw_ref, cc_ref, sc_ref, wp_ref, wq_ref):
    hi = lax.Precision.HIGHEST
    fw = fw_ref[...]
    a = jnp.dot(cc_ref[...], fw, precision=hi, preferred_element_type=F32)
    b = jnp.dot(sc_ref[...], fw, precision=hi, preferred_element_type=F32)
    wu = wu_ref[...]
    wp_ref[...] = jnp.dot(wu, a, precision=hi, preferred_element_type=F32)
    wq_ref[...] = jnp.dot(wu, b, precision=hi, preferred_element_type=F32)


def _fold_fourier_weights(w_u, fourier_w, seq):
    d_model = w_u.shape[0]
    groups, gd, _ = fourier_w.shape
    ang = 2.0 * np.pi * np.outer(np.arange(gd), np.arange(gd)) / gd
    scale = 1.0 / math.sqrt(seq * gd)
    cc = jnp.asarray(np.cos(ang) * scale, F32)
    sc = jnp.asarray(np.sin(ang) * scale, F32)
    wu_g = w_u.reshape(d_model, groups, gd).transpose(1, 0, 2)
    wp, wq = pl.pallas_call(
        _prep_kernel,
        grid=(groups,),
        in_specs=[pl.BlockSpec((None, d_model, gd), lambda g: (g, 0, 0)),
                  pl.BlockSpec((None, gd, gd), lambda g: (g, 0, 0)),
                  pl.BlockSpec((gd, gd), lambda g: (0, 0)),
                  pl.BlockSpec((gd, gd), lambda g: (0, 0))],
        out_specs=[pl.BlockSpec((None, d_model, gd), lambda g: (g, 0, 0))] * 2,
        out_shape=[jax.ShapeDtypeStruct((groups, d_model, gd), F32)] * 2,
        compiler_params=_cparams(1),
        name="fourier_weight_fold",
    )(wu_g, fourier_w, cc, sc)
    unfold = lambda w: w.transpose(1, 0, 2).reshape(d_model, groups * gd)
    return unfold(wp), unfold(wq)


def _inproj_kernel(x_ref, g_ref, w_ref, perm_ref, fperm_ref, q_ref, k_ref, v_ref, qr_ref, kr_ref, vr_ref,
                   pq_ref, planes_ref, *, n_hp, fw):
    h = _rms(x_ref[...], g_ref[...]).astype(BF16)
    tm = h.shape[0]
    perm = perm_ref[...]

    def proj(c0, n):
        return jnp.dot(h, w_ref[:, c0:c0 + n], preferred_element_type=F32)

    aw = n_hp * LANES
    pairs = ((q_ref, qr_ref), (k_ref, kr_ref), (v_ref, vr_ref))
    planes_ref[...] = proj(3 * aw, 2 * fw).astype(BF16)
    for t, (tok_ref, _) in enumerate(pairs):
        for c in range(n_hp // 2):
            res = proj(t * aw + c * 2 * LANES, 2 * LANES).astype(BF16)
            tok_ref[2 * c] = res[:, :LANES]
            tok_ref[2 * c + 1] = res[:, LANES:]
    planes = jnp.dot(fperm_ref[...], planes_ref[...], preferred_element_type=F32)
    planes = planes.reshape(FFT_RADIX, tm // FFT_RADIX, 2 * fw)
    for t in range(2):
        pq_ref[t] = planes[:, :, t * fw:(t + 1) * fw]
    for tok_ref, res_ref in pairs:
        for c in range(n_hp // 2):
            for g in range(tm // GROUP):
                tok = jnp.concatenate([tok_ref[2 * c, g * GROUP:(g + 1) * GROUP, :],
                                       tok_ref[2 * c + 1, g * GROUP:(g + 1) * GROUP, :]], axis=1)
                rows = jnp.dot(perm, tok, preferred_element_type=F32).astype(BF16)
                for half in range(2):
                    blk = rows[:, half * LANES:(half + 1) * LANES].reshape(MAX_DIL, BF16_ROWS, LANES)
                    res_ref[2 * c + half, :, g * BF16_ROWS:(g + 1) * BF16_ROWS, :] = blk


def _inproj(x2d, gain, w_all, perm, batch, seq, n_hp, fw, tm):
    n_tok, d_model = x2d.shape
    nt = seq // tm
    tok_map = lambda i: (i // nt, 0, i % nt, 0)
    res_map = lambda i: (i // nt, 0, 0, i % nt, 0)
    tok_shape = jax.ShapeDtypeStruct((batch, n_hp, seq, LANES), BF16)
    res_shape = jax.ShapeDtypeStruct((batch, n_hp, MAX_DIL, seq // MAX_DIL, LANES), BF16)
    R = FFT_RADIX
    t = np.arange(tm)
    fperm = np.zeros((tm, tm), np.float32)
    fperm[(t % R) * (tm // R) + t // R, t] = 1.0
    fperm = jnp.asarray(fperm, BF16)
    const = lambda shape: pl.BlockSpec(shape, lambda i: (0,) * len(shape), pipeline_mode=pl.Buffered(1))
    return pl.pallas_call(
        functools.partial(_inproj_kernel, n_hp=n_hp, fw=fw),
        grid=(n_tok // tm,),
        in_specs=[pl.BlockSpec((tm, d_model), lambda i: (i, 0)),
                  const((1, d_model)), const(w_all.shape), const(perm.shape), const(fperm.shape)],
        out_specs=[pl.BlockSpec((None, n_hp, tm, LANES), tok_map)] * 3
        + [pl.BlockSpec((None, n_hp, MAX_DIL, tm // MAX_DIL, LANES), res_map)] * 3
        + [pl.BlockSpec((None, 2, R, tm // R, fw), res_map)],
        out_shape=[tok_shape] * 3 + [res_shape] * 3
        + [jax.ShapeDtypeStruct((batch, 2, R, seq // R, fw), F32)],
        scratch_shapes=[pltpu.VMEM((tm, 2 * fw), BF16)],
        compiler_params=_cparams(1),
        name="rmsnorm_inproj",
    )(x2d, gain, w_all, perm, fperm)


def _t5_bucket_static(rel, dtype):
    nb = N_REL_BUCKETS // 2
    max_exact = nb // 2
    n = np.abs(rel)
    nf = np.maximum(n, 1).astype(dtype)
    large = max_exact + (np.log(nf / dtype(max_exact)) / dtype(math.log(REL_MAX_DISTANCE / max_exact))
                         * dtype(nb - max_exact)).astype(np.int32)
    large = np.minimum(large, nb - 1)
    return np.where(rel > 0, nb, 0) + np.where(n < max_exact, n, large)


def _bucket_tiles(dilations):
    qi = np.arange(TQ)[:, None]
    kc = np.arange(TK)[None, :]
    offsets = np.array([0, HALF_WINDOW, 2 * HALF_WINDOW])
    rel = kc[None] - offsets[:, None, None] - qi[None]
    tiles = []
    for d in dilations:
        bkt = _t5_bucket_static(rel * d, np.float32)
        assert np.array_equal(bkt, _t5_bucket_static(rel * d, np.float64))
        tile = np.where(np.abs(rel) <= HALF_WINDOW, bkt, -1)
        pieces = MAX_DIL // d if 1 < d < MAX_DIL else 1
        q_order = (np.arange(TQ) % (TQ // pieces)) * pieces + np.arange(TQ) // (TQ // pieces)
        k_order = (np.arange(TK) % (TK // pieces)) * pieces + np.arange(TK) // (TK // pieces)
        tiles.append(tile[:, q_order][:, :, k_order])
    return np.concatenate(tiles, axis=0).astype(np.int32)


def _bias_kernel(table_ref, bkt_ref, out_ref, *, n_heads):
    bkt = bkt_ref[...]
    for h in range(n_heads):
        acc = jnp.full(bkt.shape, NEG_INF, F32)
        for b in range(N_REL_BUCKETS):
            acc = jnp.where(bkt == b, table_ref[b, h], acc)
        out_ref[h] = acc * LOG2E


def _bias_tiles(rel_table, dilations):
    n_heads = rel_table.shape[1]
    bkt = jnp.asarray(_bucket_tiles(dilations))
    return pl.pallas_call(
        functools.partial(_bias_kernel, n_heads=n_heads),
        grid=(bkt.shape[0],),
        in_specs=[pl.BlockSpec(memory_space=pltpu.SMEM),
                  pl.BlockSpec((None, TQ, TK), lambda t: (t, 0, 0))],
        out_specs=pl.BlockSpec((None, n_heads, TQ, TK), lambda t: (t, 0, 0, 0)),
        out_shape=jax.ShapeDtypeStruct((bkt.shape[0], n_heads, TQ, TK), F32),
        compiler_params=_cparams(1),
        name="rel_bias_tiles",
    )(rel_table.astype(F32), bkt)


def _block_attention(qb, kb, vb, bias2, first_head):
    nt_dims = (((1,), (1,)), ((), ()))
    zero = jnp.zeros_like(qb)
    q2 = jnp.concatenate([jnp.where(first_head, qb, zero), jnp.where(first_head, zero, qb)], axis=0)
    s = lax.dot_general(q2, kb, nt_dims, preferred_element_type=F32) + bias2
    m = jnp.max(s, axis=-1, keepdims=True)
    p = jnp.exp2(s - m)
    l = jnp.sum(p, axis=-1, keepdims=True)
    pv = jnp.dot(p.astype(BF16), vb, preferred_element_type=F32)
    num = jnp.where(first_head, pv[:TQ], pv[TQ:])
    den = jnp.where(first_head, l[:TQ], l[TQ:])
    top = jnp.where(first_head, m[:TQ], m[TQ:])
    return num * (1.0 / den), top + jnp.log2(den)


def _merge(o_a, lse_a, o_b, lse_b):
    top = jnp.maximum(lse_a, lse_b)
    w_a = jnp.exp2(lse_a - top)
    w_b = jnp.exp2(lse_b - top)
    den = w_a + w_b
    return (w_a * o_a + w_b * o_b) * (1.0 / den), top + jnp.log2(den)


def _split_f32(x):
    hi = x.astype(BF16)
    return hi, (x - hi.astype(F32)).astype(BF16)


def _attn_kernel(*refs, dilation, merge_in, split_out):
    q_ref, k_ref, v_ref, bias_ref = refs[:4]
    pos = 4
    if merge_in is not None:
        po_ref, pl_ref = refs[pos:pos + 2]
        pos += 2
    o_ref = refs[pos]
    lse_refs = refs[pos + 1:]
    n_slabs = q_ref.shape[0]
    pieces = MAX_DIL // dilation if dilation > 1 else 1
    sub_len = q_ref.shape[-2] * pieces
    n_blk = sub_len // TQ
    tq_p, tk_p = TQ // pieces, TK // pieces
    first_head = lax.broadcasted_iota(jnp.int32, (TQ, LANES), 1) < HEAD_DIM

    def load(ref, hh, r, row, n_rows):
        if dilation == 1:
            return ref[hh, pl.ds(row, n_rows), :]
        parts = [ref[hh, a * dilation + r, pl.ds(row, n_rows), :] for a in range(pieces)]
        return parts[0] if pieces == 1 else jnp.concatenate(parts, axis=0)

    def store(ref, hh, r, row, n_rows, val):
        if dilation == 1:
            ref[hh, pl.ds(row, n_rows), :] = val
        else:
            for a in range(pieces):
                ref[hh, a * dilation + r, pl.ds(row, n_rows), :] = val[a * n_rows:(a + 1) * n_rows]

    def block(n, carry):
        q_row = pl.multiple_of(n * tq_p, tq_p)
        k_row = pl.multiple_of(jnp.clip(n * tq_p - tk_p // 4, 0, sub_len // pieces - tk_p), tk_p // 4)
        edge = jnp.where(n > 0, 1, 0) + jnp.where(n == n_blk - 1, 1, 0)
        for r in range(dilation):
            for hh in range(n_slabs):
                qb = load(q_ref, hh, r, q_row, tq_p)
                kb = load(k_ref, hh, r, k_row, tk_p)
                vb = load(v_ref, hh, r, k_row, tk_p)
                bias2 = bias_ref[edge, 2 * hh:2 * hh + 2].reshape(2 * TQ, TK)
                o, lse = _block_attention(qb, kb, vb, bias2, first_head)
                if merge_in is not None:
                    o, lse = _merge(o, lse, load(po_ref, hh, r, q_row, tq_p).astype(F32),
                                    load(pl_ref, hh, r, q_row, tq_p))
                store(o_ref, hh, r, q_row, tq_p, o.astype(BF16))
                if split_out:
                    hi, lo = _split_f32(lse)
                    store(lse_refs[0], hh, r, q_row, tq_p, hi)
                    store(lse_refs[1], hh, r, q_row, tq_p, lo)
                elif lse_refs:
                    store(lse_refs[0], hh, r, q_row, tq_p, lse)
        return carry

    lax.fori_loop(0, n_blk, block, 0, unroll=max(1, BLOCKS_IN_FLIGHT // dilation))


def _attn_token_kernel(q_ref, k_ref, v_ref, bias_ref, perm_ref, po_ref, hi_ref, lo_ref, o_ref,
                       o_tok, lse_tok):
    perm = perm_ref[...]
    for hh in range(po_ref.shape[0]):
        for g in range(po_ref.shape[2] // BF16_ROWS):
            rows = slice(g * BF16_ROWS, (g + 1) * BF16_ROWS)
            o_hi = jnp.concatenate([po_ref[hh, :, rows, :].reshape(GROUP, LANES),
                                    hi_ref[hh, :, rows, :].reshape(GROUP, LANES)], axis=1)
            tok = jnp.dot(perm, o_hi, preferred_element_type=F32)
            lo = jnp.dot(perm, lo_ref[hh, :, rows, :].reshape(GROUP, LANES), preferred_element_type=F32)
            o_tok[hh, g * GROUP:(g + 1) * GROUP, :] = tok[:, :LANES].astype(BF16)
            lse_tok[hh, g * GROUP:(g + 1) * GROUP, :] = tok[:, LANES:] + lo
    _attn_kernel(q_ref, k_ref, v_ref, bias_ref, o_tok, lse_tok, o_ref,
                 dilation=1, merge_in='token', split_out=False)


def _attention(q, k, v, qr, kr, vr, bias, perm, dilations):
    batch, n_hp, seq, _ = q.shape
    assert dilations[0] == MAX_DIL and dilations[-1] == 1 and n_hp % SLABS_PER_STEP == 0
    ns = SLABS_PER_STEP
    res_slab = pl.BlockSpec((None, ns, MAX_DIL, seq // MAX_DIL, LANES), lambda b, h: (b, h, 0, 0, 0))
    tok_slab = pl.BlockSpec((None, ns, seq, LANES), lambda b, h: (b, h, 0, 0))
    res_shape = lambda dt: jax.ShapeDtypeStruct((batch, n_hp, MAX_DIL, seq // MAX_DIL, LANES), dt)
    bias_spec = lambda i: pl.BlockSpec((3, 2 * ns, TQ, TK), lambda b, h: (i, h, 0, 0))

    merged = None
    for i, d in enumerate(dilations[:-1]):
        final_res = i == len(dilations) - 2
        args = [qr, kr, vr, bias] + (list(merged) if merged else [])
        in_specs = [res_slab] * 3 + [bias_spec(i)] + ([res_slab] * 2 if merged else [])
        out_shape = [res_shape(BF16)] + ([res_shape(BF16)] * 2 if final_res else [res_shape(F32)])
        merged = pl.pallas_call(
            functools.partial(_attn_kernel, dilation=d, merge_in='f32' if merged else None,
                              split_out=final_res),
            grid=(batch, n_hp // ns),
            in_specs=in_specs,
            out_specs=[res_slab] * len(out_shape),
            out_shape=out_shape,
            compiler_params=_cparams(2),
            name=f"dilated_attn_d{d}",
        )(*args)

    return pl.pallas_call(
        _attn_token_kernel,
        grid=(batch, n_hp // ns),
        in_specs=[tok_slab] * 3 + [bias_spec(len(dilations) - 1), pl.BlockSpec(perm.shape, lambda b, h: (0, 0))]
        + [res_slab] * 3,
        out_specs=tok_slab,
        out_shape=jax.ShapeDtypeStruct((batch, n_hp, seq, LANES), BF16),
        scratch_shapes=[pltpu.VMEM((ns, seq, LANES), BF16), pltpu.VMEM((ns, seq, LANES), F32)],
        compiler_params=_cparams(2),
        name="dilated_attn_d1",
    )(q, k, v, bias, perm, *merged)


def _swap_16x16(mats, perm):
    out = []
    for g in range(mats[0].shape[0] // BF16_ROWS):
        rows = jnp.concatenate([m[g * BF16_ROWS:(g + 1) * BF16_ROWS] for m in mats], axis=0)
        swapped = jnp.dot(perm, rows, preferred_element_type=F32).astype(BF16)
        out.append(swapped.reshape(BF16_ROWS, MAX_DIL, rows.shape[-1]))
    return out


def _fft_stage1_kernel(w_ref, perm_ref, v_ref, z_ref):
    w = w_ref[...]
    perm = perm_ref[...]
    R = v_ref.shape[2]
    zs = []
    for j in range(v_ref.shape[1]):
        rhs = jnp.concatenate([v_ref[0, j], v_ref[1, j]], axis=0).astype(BF16)
        zs.append(jnp.dot(w, rhs, preferred_element_type=F32).astype(BF16))
    for plane in range(2):
        groups = _swap_16x16([z[plane * R:(plane + 1) * R] for z in zs], perm)
        for g, blk in enumerate(groups):
            z_ref[g * BF16_ROWS:(g + 1) * BF16_ROWS, plane, :, :] = blk


def _fft_stage2_kernel(m_ref, perm_ref, z_ref, b_ref, o_ref):
    perm = perm_ref[...]
    xs = []
    for j in range(z_ref.shape[0]):
        zc = z_ref[j].reshape(2 * z_ref.shape[2], z_ref.shape[3])
        xs.append((jnp.dot(m_ref[j], zc, preferred_element_type=F32) + b_ref[...]).astype(BF16))
    for g, blk in enumerate(_swap_16x16(xs, perm)):
        o_ref[g * BF16_ROWS:(g + 1) * BF16_ROWS, :, :] = blk


def _fourier(pq, fourier_b, perm):
    batch, _, R, _, fw = pq.shape
    seq = R * R
    assert R == FFT_RADIX and R % BF16_ROWS == 0 and MAX_DIL == BF16_ROWS
    i = np.arange(R)
    ang1 = 2.0 * np.pi * np.outer(i, i) / R
    c1, s1 = np.cos(ang1), np.sin(ang1)
    w_cat = jnp.asarray(np.block([[c1, -s1], [-s1, -c1]]), BF16)
    k_all = i[:, None, None] + R * i[None, :, None]
    ang2 = 2.0 * np.pi * ((k_all * i[None, None, :]) % seq) / seq
    m_cat = jnp.asarray(np.concatenate([np.cos(ang2), np.sin(ang2)], axis=-1), BF16)
    grp = BF16_ROWS
    const2 = lambda shape: pl.BlockSpec(shape, lambda b, g: (0,) * len(shape))

    z = pl.pallas_call(
        _fft_stage1_kernel,
        grid=(batch, R // grp),
        in_specs=[const2((2 * R, 2 * R)), const2(perm.shape),
                  pl.BlockSpec((None, 2, grp, R, fw), lambda b, g: (b, 0, g, 0, 0))],
        out_specs=pl.BlockSpec((None, R, 2, grp, fw), lambda b, g: (b, 0, 0, g, 0)),
        out_shape=jax.ShapeDtypeStruct((batch, R, 2, R, fw), BF16),
        compiler_params=_cparams(2),
        name="fft_stage1",
    )(w_cat, perm, pq)

    out = pl.pallas_call(
        _fft_stage2_kernel,
        grid=(batch, R // grp),
        in_specs=[pl.BlockSpec((grp, R, 2 * R), lambda b, g: (g, 0, 0)), const2(perm.shape),
                  pl.BlockSpec((None, grp, 2, R, fw), lambda b, g: (b, g, 0, 0, 0)),
                  const2((1, fw))],
        out_specs=pl.BlockSpec((None, R, None, grp, fw), lambda b, g: (b, 0, g, 0, 0)),
        out_shape=jax.ShapeDtypeStruct((batch, R, R // grp, grp, fw), BF16),
        compiler_params=_cparams(2),
        name="fft_stage2",
    )(m_cat, perm, z, fourier_b.reshape(1, fw).astype(F32))
    return out.reshape(batch, seq, fw)


def _mix_ffn_kernel(ap_ref, a_ref, an_ref, fp_ref, f_ref, fn_ref, xp_ref, x_ref, xn_ref,
                    ga_ref, gf_ref, wo_ref, g_ref, wg_ref, wv_ref, cw_ref, cb_ref, wd_ref, gfin_ref,
                    o_ref, act_ref, *, tiles_per_seq, fc, final_norm, n_hp):
    i = pl.program_id(0)
    tm = x_ref.shape[0]
    n_ext = tm + 2 * HALO
    ext = lambda prev, cur, nxt: jnp.concatenate([prev, cur, nxt], axis=0)

    parts = [ext(ap_ref[j], a_ref[j], an_ref[j]).astype(F32) for j in range(n_hp)]
    ssq = functools.reduce(lambda a, b: a + b,
                           [jnp.sum(p * p, axis=-1, keepdims=True) for p in parts])
    inv = lax.rsqrt(ssq / (n_hp * LANES) + EPS)
    ga = ga_ref[...]
    cols = [(p * inv * ga[:, j * LANES:(j + 1) * LANES]).astype(BF16) for j, p in enumerate(parts)]
    cols.append(_rms(ext(fp_ref[...], f_ref[...], fn_ref[...]).astype(F32), gf_ref[...]).astype(BF16))
    mixed = jnp.concatenate(cols, axis=1)
    x_ext = ext(xp_ref[...], x_ref[...], xn_ref[...]) + jnp.dot(mixed, wo_ref[...],
                                                                 preferred_element_type=F32)
    x = x_ext[HALO:HALO + tm]

    keep_prev = (i % tiles_per_seq != 0).astype(F32)
    keep_next = (i % tiles_per_seq != tiles_per_seq - 1).astype(F32)
    row = lax.broadcasted_iota(jnp.int32, (n_ext, 1), 0)
    keep = jnp.where(row < HALO, keep_prev, jnp.where(row >= HALO + tm, keep_next, 1.0))
    h_ext = (_rms(x_ext, g_ref[...]) * keep).astype(BF16)
    h = h_ext[HALO:HALO + tm]
    d_ff = wg_ref.shape[1]

    def up(c0):
        cs = slice(c0, c0 + fc)
        return (jnp.dot(h_ext, wg_ref[:, cs], preferred_element_type=F32),
                jnp.dot(h, wv_ref[:, cs], preferred_element_type=F32))

    nxt = up(0)
    for c0 in range(0, d_ff, fc):
        cs = slice(c0, c0 + fc)
        g, val = nxt
        if c0 + fc < d_ff:
            nxt = up(c0 + fc)
        g_prev = pltpu.roll(g, 1, axis=0)[HALO:HALO + tm]
        g_next = pltpu.roll(g, n_ext - 1, axis=0)[HALO:HALO + tm]
        cw = cw_ref[:, cs]
        conv = cw[0:1] * g_prev + cw[1:2] * g[HALO:HALO + tm] + cw[2:3] * g_next + cb_ref[:, cs]
        act_ref[:, cs] = (conv * (1.0 / (1.0 + jnp.exp(-conv))) * val).astype(BF16)
    y = x + jnp.dot(act_ref[...], wd_ref[...], preferred_element_type=F32)
    o_ref[...] = _rms(y, gfin_ref[...]) if final_norm else y


def _mix_ffn(attn, four, x2d, ga, gf, w_out, gain, wg, wv, conv_w, conv_b, wd, gfin, tm, fc, final_norm):
    batch, n_hp, seq, _ = attn.shape
    fw = four.shape[-1]
    n_tok, d_model = x2d.shape
    d_ff = wg.shape[1]
    nt = seq // tm
    per_tile = tm // HALO
    last = seq // HALO - 1
    prev_blk = lambda i: jnp.maximum((i % nt) * per_tile - 1, 0)
    next_blk = lambda i: jnp.minimum((i % nt + 1) * per_tile, last)
    seq_blk = lambda i: (i // nt) * (last + 1)
    const = lambda shape: pl.BlockSpec(shape, lambda i: (0,) * len(shape), pipeline_mode=pl.Buffered(1))

    def triple(cur_shape, halo_shape, index):
        return [pl.BlockSpec(halo_shape, lambda i: index(i // nt, prev_blk(i))),
                pl.BlockSpec(cur_shape, lambda i: index(i // nt, i % nt)),
                pl.BlockSpec(halo_shape, lambda i: index(i // nt, next_blk(i)))]

    in_specs = (triple((None, n_hp, tm, LANES), (None, n_hp, HALO, LANES), lambda b, t: (b, 0, t, 0))
                + triple((None, tm, fw), (None, HALO, fw), lambda b, t: (b, t, 0))
                + [pl.BlockSpec((HALO, d_model), lambda i: (seq_blk(i) + prev_blk(i), 0)),
                   pl.BlockSpec((tm, d_model), lambda i: (i, 0)),
                   pl.BlockSpec((HALO, d_model), lambda i: (seq_blk(i) + next_blk(i), 0))]
                + [const((1, n_hp * LANES)), const((1, fw)), const(w_out.shape), const((1, d_model)),
                   const(wg.shape), const(wv.shape), const(conv_w.shape), const((1, d_ff)), const(wd.shape),
                   const((1, d_model))])
    return pl.pallas_call(
        functools.partial(_mix_ffn_kernel, tiles_per_seq=nt, fc=fc, final_norm=final_norm, n_hp=n_hp),
        grid=(n_tok // tm,),
        in_specs=in_specs,
        out_specs=pl.BlockSpec((tm, d_model), lambda i: (i, 0)),
        out_shape=jax.ShapeDtypeStruct((n_tok, d_model), F32),
        scratch_shapes=[pltpu.VMEM((tm, d_ff), BF16)],
        compiler_params=_cparams(1),
        name="outproj_convglu_ffn",
    )(attn, attn, attn, four, four, four, x2d, x2d, x2d, ga, gf, w_out, gain, wg, wv, conv_w, conv_b, wd, gfin)


def kernel(x, norm_mix_gain, w_in, attn_out_gain, rel_bias_table, fourier_w, fourier_b, fourier_out_gain, w_out, norm_ffn_gain, w_gate, w_val, conv_w, conv_b, w_down, final_norm_gain):
    batch, seq, d_model = x.shape
    depth = w_in.shape[0]
    n_heads = rel_bias_table.shape[1]
    attn_w = n_heads * HEAD_DIM
    n_hp = attn_w // LANES
    fw = fourier_w.shape[1] * fourier_w.shape[2]
    assert all(w // (2 * d) == HALF_WINDOW for w, d in DILATED_PATTERNS)
    dilations = sorted((d for _, d in DILATED_PATTERNS), reverse=True)
    tm = 512
    row = lambda g: g.reshape(1, -1).astype(F32)
    perm = _group_permutation()
    bias = _bias_tiles(rel_bias_table, dilations)

    x2d = x.reshape(batch * seq, d_model)
    for layer in range(depth):
        w_l = w_in[layer]
        wp, wq = _fold_fourier_weights(w_l[:, 3 * attn_w:], fourier_w[layer], seq)
        w_all = jnp.concatenate([w_l[:, :attn_w] * (HEAD_DIM ** -0.5 * LOG2E), w_l[:, attn_w:3 * attn_w], wp, wq],
                                axis=1).astype(BF16)
        q, k, v, qr, kr, vr, pq = _inproj(x2d, row(norm_mix_gain[layer]), w_all, perm, batch, seq, n_hp, fw, tm)
        attn = _attention(q, k, v, qr, kr, vr, bias, perm, dilations)
        four = _fourier(pq, fourier_b[layer].reshape(-1), perm)
        x2d = _mix_ffn(attn, four, x2d, row(attn_out_gain[layer]), row(fourier_out_gain[layer]),
                       w_out[layer].astype(BF16), row(norm_ffn_gain[layer]), w_gate[layer].astype(BF16),
                       w_val[layer].astype(BF16), conv_w[layer].astype(F32), row(conv_b[layer]),
                       w_down[layer].astype(BF16), row(final_norm_gain), 2 * tm, 256, layer == depth - 1)
    return x2d.reshape(batch, seq, d_model)
```

```python
import functools
import math

import numpy as np
import jax
import jax.numpy as jnp
from jax import lax
from jax.experimental import pallas as pl
from jax.experimental.pallas import tpu as pltpu

EPS = 1e-6
NEG_INF = -1e30
LOG2E = math.log2(math.e)
HEAD_DIM = 64
DILATED_PATTERNS = ((128, 1), (512, 4), (2048, 16))
N_REL_BUCKETS = 32
REL_MAX_DISTANCE = 1024

LANES = 128
BF16_ROWS = 16
HALF_WINDOW = 64
TQ = 2 * HALF_WINDOW
TK = 4 * HALF_WINDOW
MAX_DIL = max(d for _, d in DILATED_PATTERNS)
GROUP = MAX_DIL * BF16_ROWS
BLOCKS_IN_FLIGHT = 32
SLABS_PER_STEP = 2
FFT_RADIX = 64
HALO = BF16_ROWS
VMEM_LIMIT = 56 * 1024 * 1024

BF16 = jnp.bfloat16
F32 = jnp.float32


def _cparams(n_axes):
    return pltpu.CompilerParams(dimension_semantics=("arbitrary",) * n_axes,
                                vmem_limit_bytes=VMEM_LIMIT)


def _rms(x, gain):
    ms = jnp.mean(x * x, axis=-1, keepdims=True)
    return x * lax.rsqrt(ms + EPS) * gain


def _group_permutation():
    t = np.arange(GROUP)
    swapped = (t % MAX_DIL) * BF16_ROWS + t // MAX_DIL
    perm = np.zeros((GROUP, GROUP), np.float32)
    perm[swapped, t] = 1.0
    assert MAX_DIL == BF16_ROWS and np.array_equal(perm, perm.T)
    return jnp.asarray(perm, BF16)


def _prep_kernel(wu_ref, fw_ref, cc_ref, sc_ref, wp_ref, wq_ref):
    hi = lax.Precision.HIGHEST
    fw = fw_ref[...]
    a = jnp.dot(cc_ref[...], fw, precision=hi, preferred_element_type=F32)
    b = jnp.dot(sc_ref[...], fw, precision=hi, preferred_element_type=F32)
    wu = wu_ref[...]
    wp_ref[...] = jnp.dot(wu, a, precision=hi, preferred_element_type=F32)
    wq_ref[...] = jnp.dot(wu, b, precision=hi, preferred_element_type=F32)


def _fold_fourier_weights(w_u, fourier_w, seq):
    d_model = w_u.shape[0]
    groups, gd, _ = fourier_w.shape
    ang = 2.0 * np.pi * np.outer(np.arange(gd), np.arange(gd)) / gd
    scale = 1.0 / math.sqrt(seq * gd)
    cc = jnp.asarray(np.cos(ang) * scale, F32)
    sc = jnp.asarray(np.sin(ang) * scale, F32)
    wu_g = w_u.reshape(d_model, groups, gd).transpose(1, 0, 2)
    wp, wq = pl.pallas_call(
        _prep_kernel,
        grid=(groups,),
        in_specs=[pl.BlockSpec((None, d_model, gd), lambda g: (g, 0, 0)),
                  pl.BlockSpec((None, gd, gd), lambda g: (g, 0, 0)),
                  pl.BlockSpec((gd, gd), lambda g: (0, 0)),
                  pl.BlockSpec((gd, gd), lambda g: (0, 0))],
        out_specs=[pl.BlockSpec((None, d_model, gd), lambda g: (g, 0, 0))] * 2,
        out_shape=[jax.ShapeDtypeStruct((groups, d_model, gd), F32)] * 2,
        compiler_params=_cparams(1),
        name="fourier_weight_fold",
    )(wu_g, fourier_w, cc, sc)
    unfold = lambda w: w.transpose(1, 0, 2).reshape(d_model, groups * gd)
    return unfold(wp), unfold(wq)


def _inproj_kernel(x_ref, g_ref, w_ref, perm_ref, fperm_ref, q_ref, k_ref, v_ref, qr_ref, kr_ref, vr_ref,
                   pq_ref, planes_ref, *, n_hp, fw):
    h = _rms(x_ref[...], g_ref[...]).astype(BF16)
    tm = h.shape[0]
    perm = perm_ref[...]

    def proj(c0, n):
        return jnp.dot(h, w_ref[:, c0:c0 + n], preferred_element_type=F32)

    aw = n_hp * LANES
    pairs = ((q_ref, qr_ref), (k_ref, kr_ref), (v_ref, vr_ref))
    planes_ref[...] = proj(3 * aw, 2 * fw).astype(BF16)
    for t, (tok_ref, _) in enumerate(pairs):
        for c in range(n_hp // 2):
            res = proj(t * aw + c * 2 * LANES, 2 * LANES).astype(BF16)
            tok_ref[2 * c] = res[:, :LANES]
            tok_ref[2 * c + 1] = res[:, LANES:]
    planes = jnp.dot(fperm_ref[...], planes_ref[...], preferred_element_type=F32)
    planes = planes.reshape(FFT_RADIX, tm // FFT_RADIX, 2 * fw)
    for t in range(2):
        pq_ref[t] = planes[:, :, t * fw:(t + 1) * fw]
    for tok_ref, res_ref in pairs:
        for c in range(n_hp // 2):
            for g in range(tm // GROUP):
                tok = jnp.concatenate([tok_ref[2 * c, g * GROUP:(g + 1) * GROUP, :],
                                       tok_ref[2 * c + 1, g * GROUP:(g + 1) * GROUP, :]], axis=1)
                rows = jnp.dot(perm, tok, preferred_element_type=F32).astype(BF16)
                for half in range(2):
                    blk = rows[:, half * LANES:(half + 1) * LANES].reshape(MAX_DIL, BF16_ROWS, LANES)
                    res_ref[2 * c + half, :, g * BF16_ROWS:(g + 1) * BF16_ROWS, :] = blk


def _inproj(x2d, gain, w_all, perm, batch, seq, n_hp, fw, tm):
    n_tok, d_model = x2d.shape
    nt = seq // tm
    tok_map = lambda i: (i // nt, 0, i % nt, 0)
    res_map = lambda i: (i // nt, 0, 0, i % nt, 0)
    tok_shape = jax.ShapeDtypeStruct((batch, n_hp, seq, LANES), BF16)
    res_shape = jax.ShapeDtypeStruct((batch, n_hp, MAX_DIL, seq // MAX_DIL, LANES), BF16)
    R = FFT_RADIX
    t = np.arange(tm)
    fperm = np.zeros((tm, tm), np.float32)
    fperm[(t % R) * (tm // R) + t // R, t] = 1.0
    fperm = jnp.asarray(fperm, BF16)
    const = lambda shape: pl.BlockSpec(shape, lambda i: (0,) * len(shape), pipeline_mode=pl.Buffered(1))
    return pl.pallas_call(
        functools.partial(_inproj_kernel, n_hp=n_hp, fw=fw),
        grid=(n_tok // tm,),
        in_specs=[pl.BlockSpec((tm, d_model), lambda i: (i, 0)),
                  const((1, d_model)), const(w_all.shape), const(perm.shape), const(fperm.shape)],
        out_specs=[pl.BlockSpec((None, n_hp, tm, LANES), tok_map)] * 3
        + [pl.BlockSpec((None, n_hp, MAX_DIL, tm // MAX_DIL, LANES), res_map)] * 3
        + [pl.BlockSpec((None, 2, R, tm // R, fw), res_map)],
        out_shape=[tok_shape] * 3 + [res_shape] * 3
        + [jax.ShapeDtypeStruct((batch, 2, R, seq // R, fw), F32)],
        scratch_shapes=[pltpu.VMEM((tm, 2 * fw), BF16)],
        compiler_params=_cparams(1),
        name="rmsnorm_inproj",
    )(x2d, gain, w_all, perm, fperm)


def _t5_bucket_static(rel, dtype):
    nb = N_REL_BUCKETS // 2
    max_exact = nb // 2
    n = np.abs(rel)
    nf = np.maximum(n, 1).astype(dtype)
    large = max_exact + (np.log(nf / dtype(max_exact)) / dtype(math.log(REL_MAX_DISTANCE / max_exact))
                         * dtype(nb - max_exact)).astype(np.int32)
    large = np.minimum(large, nb - 1)
    return np.where(rel > 0, nb, 0) + np.where(n < max_exact, n, large)


def _bucket_tiles(dilations):
    qi = np.arange(TQ)[:, None]
    kc = np.arange(TK)[None, :]
    offsets = np.array([0, HALF_WINDOW, 2 * HALF_WINDOW])
    rel = kc[None] - offsets[:, None, None] - qi[None]
    tiles = []
    for d in dilations:
        bkt = _t5_bucket_static(rel * d, np.float32)
        assert np.array_equal(bkt, _t5_bucket_static(rel * d, np.float64))
        tile = np.where(np.abs(rel) <= HALF_WINDOW, bkt, -1)
        pieces = MAX_DIL // d if 1 < d < MAX_DIL else 1
        q_order = (np.arange(TQ) % (TQ // pieces)) * pieces + np.arange(TQ) // (TQ // pieces)
        k_order = (np.arange(TK) % (TK // pieces)) * pieces + np.arange(TK) // (TK // pieces)
        tiles.append(tile[:, q_order][:, :, k_order])
    return np.concatenate(tiles, axis=0).astype(np.int32)


def _bias_kernel(table_ref, bkt_ref, out_ref, *, n_heads):
    bkt = bkt_ref[...]
    for h in range(n_heads):
        acc = jnp.full(bkt.shape, NEG_INF, F32)
        for b in range(N_REL_BUCKETS):
            acc = jnp.where(bkt == b, table_ref[b, h], acc)
        out_ref[h] = acc * LOG2E


def _bias_tiles(rel_table, dilations):
    n_heads = rel_table.shape[1]
    bkt = jnp.asarray(_bucket_tiles(dilations))
    return pl.pallas_call(
        functools.partial(_bias_kernel, n_heads=n_heads),
        grid=(bkt.shape[0],),
        in_specs=[pl.BlockSpec(memory_space=pltpu.SMEM),
                  pl.BlockSpec((None, TQ, TK), lambda t: (t, 0, 0))],
        out_specs=pl.BlockSpec((None, n_heads, TQ, TK), lambda t: (t, 0, 0, 0)),
        out_shape=jax.ShapeDtypeStruct((bkt.shape[0], n_heads, TQ, TK), F32),
        compiler_params=_cparams(1),
        name="rel_bias_tiles",
    )(rel_table.astype(F32), bkt)


def _block_attention(qb, kb, vb, bias2, first_head):
    nt_dims = (((1,), (1,)), ((), ()))
    zero = jnp.zeros_like(qb)
    q2 = jnp.concatenate([jnp.where(first_head, qb, zero), jnp.where(first_head, zero, qb)], axis=0)
    s = lax.dot_general(q2, kb, nt_dims, preferred_element_type=F32) + bias2
    m = jnp.max(s, axis=-1, keepdims=True)
    p = jnp.exp2(s - m)
    l = jnp.sum(p, axis=-1, keepdims=True)
    pv = jnp.dot(p.astype(BF16), vb, preferred_element_type=F32)
    num = jnp.where(first_head, pv[:TQ], pv[TQ:])
    den = jnp.where(first_head, l[:TQ], l[TQ:])
    top = jnp.where(first_head, m[:TQ], m[TQ:])
    return num * (1.0 / den), top + jnp.log2(den)


def _merge(o_a, lse_a, o_b, lse_b):
    top = jnp.maximum(lse_a, lse_b)
    w_a = jnp.exp2(lse_a - top)
    w_b = jnp.exp2(lse_b - top)
    den = w_a + w_b
    return (w_a * o_a + w_b * o_b) * (1.0 / den), top + jnp.log2(den)


def _split_f32(x):
    hi = x.astype(BF16)
    return hi, (x - hi.astype(F32)).astype(BF16)


def _attn_kernel(*refs, dilation, merge_in, split_out):
    q_ref, k_ref, v_ref, bias_ref = refs[:4]
    pos = 4
    if merge_in is not None:
        po_ref, pl_ref = refs[pos:pos + 2]
        pos += 2
    o_ref = refs[pos]
    lse_refs = refs[pos + 1:]
    n_slabs = q_ref.shape[0]
    pieces = MAX_DIL // dilation if dilation > 1 else 1
    sub_len = q_ref.shape[-2] * pieces
    n_blk = sub_len // TQ
    tq_p, tk_p = TQ // pieces, TK // pieces
    first_head = lax.broadcasted_iota(jnp.int32, (TQ, LANES), 1) < HEAD_DIM

    def load(ref, hh, r, row, n_rows):
        if dilation == 1:
            return ref[hh, pl.ds(row, n_rows), :]
        parts = [ref[hh, a * dilation + r, pl.ds(row, n_rows), :] for a in range(pieces)]
        return parts[0] if pieces == 1 else jnp.concatenate(parts, axis=0)

    def store(ref, hh, r, row, n_rows, val):
        if dilation == 1:
            ref[hh, pl.ds(row, n_rows), :] = val
        else:
            for a in range(pieces):
                ref[hh, a * dilation + r, pl.ds(row, n_rows), :] = val[a * n_rows:(a + 1) * n_rows]

    def block(n, carry):
        q_row = pl.multiple_of(n * tq_p, tq_p)
        k_row = pl.multiple_of(jnp.clip(n * tq_p - tk_p // 4, 0, sub_len // pieces - tk_p), tk_p // 4)
        edge = jnp.where(n > 0, 1, 0) + jnp.where(n == n_blk - 1, 1, 0)
        for r in range(dilation):
            for hh in range(n_slabs):
                qb = load(q_ref, hh, r, q_row, tq_p)
                kb = load(k_ref, hh, r, k_row, tk_p)
                vb = load(v_ref, hh, r, k_row, tk_p)
                bias2 = bias_ref[edge, 2 * hh:2 * hh + 2].reshape(2 * TQ, TK)
                o, lse = _block_attention(qb, kb, vb, bias2, first_head)
                if merge_in is not None:
                    o, lse = _merge(o, lse, load(po_ref, hh, r, q_row, tq_p).astype(F32),
                                    load(pl_ref, hh, r, q_row, tq_p))
                store(o_ref, hh, r, q_row, tq_p, o.astype(BF16))
                if split_out:
                    hi, lo = _split_f32(lse)
                    store(lse_refs[0], hh, r, q_row, tq_p, hi)
                    store(lse_refs[1], hh, r, q_row, tq_p, lo)
                elif lse_refs:
                    store(lse_refs[0], hh, r, q_row, tq_p, lse)
        return carry

    lax.fori_loop(0, n_blk, block, 0, unroll=max(1, BLOCKS_IN_FLIGHT // dilation))


def _attn_token_kernel(q_ref, k_ref, v_ref, bias_ref, perm_ref, po_ref, hi_ref, lo_ref, o_ref,
                       o_tok, lse_tok):
    perm = perm_ref[...]
    for hh in range(po_ref.shape[0]):
        for g in range(po_ref.shape[2] // BF16_ROWS):
            rows = slice(g * BF16_ROWS, (g + 1) * BF16_ROWS)
            o_hi = jnp.concatenate([po_ref[hh, :, rows, :].reshape(GROUP, LANES),
                                    hi_ref[hh, :, rows, :].reshape(GROUP, LANES)], axis=1)
            tok = jnp.dot(perm, o_hi, preferred_element_type=F32)
            lo = jnp.dot(perm, lo_ref[hh, :, rows, :].reshape(GROUP, LANES), preferred_element_type=F32)
            o_tok[hh, g * GROUP:(g + 1) * GROUP, :] = tok[:, :LANES].astype(BF16)
            lse_tok[hh, g * GROUP:(g + 1) * GROUP, :] = tok[:, LANES:] + lo
    _attn_kernel(q_ref, k_ref, v_ref, bias_ref, o_tok, lse_tok, o_ref,
                 dilation=1, merge_in='token', split_out=False)


def _attention(q, k, v, qr, kr, vr, bias, perm, dilations):
    batch, n_hp, seq, _ = q.shape
    assert dilations[0] == MAX_DIL and dilations[-1] == 1 and n_hp % SLABS_PER_STEP == 0
    ns = SLABS_PER_STEP
    res_slab = pl.BlockSpec((None, ns, MAX_DIL, seq // MAX_DIL, LANES), lambda b, h: (b, h, 0, 0, 0))
    tok_slab = pl.BlockSpec((None, ns, seq, LANES), lambda b, h: (b, h, 0, 0))
    res_shape = lambda dt: jax.ShapeDtypeStruct((batch, n_hp, MAX_DIL, seq // MAX_DIL, LANES), dt)
    bias_spec = lambda i: pl.BlockSpec((3, 2 * ns, TQ, TK), lambda b, h: (i, h, 0, 0))

    merged = None
    for i, d in enumerate(dilations[:-1]):
        final_res = i == len(dilations) - 2
        args = [qr, kr, vr, bias] + (list(merged) if merged else [])
        in_specs = [res_slab] * 3 + [bias_spec(i)] + ([res_slab] * 2 if merged else [])
        out_shape = [res_shape(BF16)] + ([res_shape(BF16)] * 2 if final_res else [res_shape(F32)])
        merged = pl.pallas_call(
            functools.partial(_attn_kernel, dilation=d, merge_in='f32' if merged else None,
                              split_out=final_res),
            grid=(batch, n_hp // ns),
            in_specs=in_specs,
            out_specs=[res_slab] * len(out_shape),
            out_shape=out_shape,
            compiler_params=_cparams(2),
            name=f"dilated_attn_d{d}",
        )(*args)

    return pl.pallas_call(
        _attn_token_kernel,
        grid=(batch, n_hp // ns),
        in_specs=[tok_slab] * 3 + [bias_spec(len(dilations) - 1), pl.BlockSpec(perm.shape, lambda b, h: (0, 0))]
        + [res_slab] * 3,
        out_specs=tok_slab,
        out_shape=jax.ShapeDtypeStruct((batch, n_hp, seq, LANES), BF16),
        scratch_shapes=[pltpu.VMEM((ns, seq, LANES), BF16), pltpu.VMEM((ns, seq, LANES), F32)],
        compiler_params=_cparams(2),
        name="dilated_attn_d1",
    )(q, k, v, bias, perm, *merged)


def _swap_16x16(mats, perm):
    out = []
    for g in range(mats[0].shape[0] // BF16_ROWS):
        rows = jnp.concatenate([m[g * BF16_ROWS:(g + 1) * BF16_ROWS] for m in mats], axis=0)
        swapped = jnp.dot(perm, rows, preferred_element_type=F32).astype(BF16)
        out.append(swapped.reshape(BF16_ROWS, MAX_DIL, rows.shape[-1]))
    return out


def _fft_kernel(w_ref, m_ref, perm_ref, b_ref, v_ref, o_ref, z_ref):
    w = w_ref[...]
    perm = perm_ref[...]
    R = v_ref.shape[2]
    grp = BF16_ROWS
    for bg in range(R // grp):
        zs = []
        for j in range(grp):
            b = bg * grp + j
            rhs = jnp.concatenate([v_ref[0, b], v_ref[1, b]], axis=0).astype(BF16)
            zs.append(jnp.dot(w, rhs, preferred_element_type=F32).astype(BF16))
        for plane in range(2):
            groups = _swap_16x16([z[plane * R:(plane + 1) * R] for z in zs], perm)
            for g, blk in enumerate(groups):
                z_ref[g * grp:(g + 1) * grp, plane, bg * grp:(bg + 1) * grp, :] = blk
    for kg in range(R // grp):
        xs = []
        for j in range(grp):
            k1 = kg * grp + j
            zc = z_ref[k1].reshape(2 * R, z_ref.shape[3])
            xs.append((jnp.dot(m_ref[k1], zc, preferred_element_type=F32) + b_ref[...]).astype(BF16))
        for g, blk in enumerate(_swap_16x16(xs, perm)):
            o_ref[g * grp:(g + 1) * grp, kg, :, :] = blk


def _fourier(pq, fourier_b, perm):
    batch, _, R, _, fw = pq.shape
    seq = R * R
    assert R == FFT_RADIX and R % BF16_ROWS == 0 and MAX_DIL == BF16_ROWS
    i = np.arange(R)
    ang1 = 2.0 * np.pi * np.outer(i, i) / R
    c1, s1 = np.cos(ang1), np.sin(ang1)
    w_cat = jnp.asarray(np.block([[c1, -s1], [-s1, -c1]]), BF16)
    k_all = i[:, None, None] + R * i[None, :, None]
    ang2 = 2.0 * np.pi * ((k_all * i[None, None, :]) % seq) / seq
    m_cat = jnp.asarray(np.concatenate([np.cos(ang2), np.sin(ang2)], axis=-1), BF16)
    grp = BF16_ROWS
    const = lambda shape: pl.BlockSpec(shape, lambda b: (0,) * len(shape), pipeline_mode=pl.Buffered(1))
    out = pl.pallas_call(
        _fft_kernel,
        grid=(batch,),
        in_specs=[const((2 * R, 2 * R)), const(m_cat.shape), const(perm.shape), const((1, fw)),
                  pl.BlockSpec((None, 2, R, R, fw), lambda b: (b, 0, 0, 0, 0))],
        out_specs=pl.BlockSpec((None, R, R // grp, grp, fw), lambda b: (b, 0, 0, 0, 0)),
        out_shape=jax.ShapeDtypeStruct((batch, R, R // grp, grp, fw), BF16),
        scratch_shapes=[pltpu.VMEM((R, 2, R, fw), BF16)],
        compiler_params=_cparams(1),
        name="fft_positions",
    )(w_cat, m_cat, perm, fourier_b.reshape(1, fw).astype(F32), pq)
    return out.reshape(batch, seq, fw)


def _mix_ffn_kernel(ap_ref, a_ref, an_ref, fp_ref, f_ref, fn_ref, xp_ref, x_ref, xn_ref,
                    ga_ref, gf_ref, wo_ref, g_ref, wg_ref, wv_ref, cw_ref, cb_ref, wd_ref, gfin_ref,
                    o_ref, act_ref, *, tiles_per_seq, fc, final_norm, n_hp):
    i = pl.program_id(0)
    tm = x_ref.shape[0]
    n_ext = tm + 2 * HALO
    ext = lambda prev, cur, nxt: jnp.concatenate([prev, cur, nxt], axis=0)

    parts = [ext(ap_ref[j], a_ref[j], an_ref[j]).astype(F32) for j in range(n_hp)]
    ssq = functools.reduce(lambda a, b: a + b,
                           [jnp.sum(p * p, axis=-1, keepdims=True) for p in parts])
    inv = lax.rsqrt(ssq / (n_hp * LANES) + EPS)
    ga = ga_ref[...]
    cols = [(p * inv * ga[:, j * LANES:(j + 1) * LANES]).astype(BF16) for j, p in enumerate(parts)]
    cols.append(_rms(ext(fp_ref[...], f_ref[...], fn_ref[...]).astype(F32), gf_ref[...]).astype(BF16))
    mixed = jnp.concatenate(cols, axis=1)
    x_ext = ext(xp_ref[...], x_ref[...], xn_ref[...]) + jnp.dot(mixed, wo_ref[...],
                                                                 preferred_element_type=F32)
    x = x_ext[HALO:HALO + tm]

    keep_prev = (i % tiles_per_seq != 0).astype(F32)
    keep_next = (i % tiles_per_seq != tiles_per_seq - 1).astype(F32)
    row = lax.broadcasted_iota(jnp.int32, (n_ext, 1), 0)
    keep = jnp.where(row < HALO, keep_prev, jnp.where(row >= HALO + tm, keep_next, 1.0))
    h_ext = (_rms(x_ext, g_ref[...]) * keep).astype(BF16)
    h = h_ext[HALO:HALO + tm]
    d_ff = wg_ref.shape[1]

    def up(c0):
        cs = slice(c0, c0 + fc)
        return (jnp.dot(h_ext, wg_ref[:, cs], preferred_element_type=F32),
                jnp.dot(h, wv_ref[:, cs], preferred_element_type=F32))

    nxt = up(0)
    for c0 in range(0, d_ff, fc):
        cs = slice(c0, c0 + fc)
        g, val = nxt
        if c0 + fc < d_ff:
            nxt = up(c0 + fc)
        g_prev = pltpu.roll(g, 1, axis=0)[HALO:HALO + tm]
        g_next = pltpu.roll(g, n_ext - 1, axis=0)[HALO:HALO + tm]
        cw = cw_ref[:, cs]
        conv = cw[0:1] * g_prev + cw[1:2] * g[HALO:HALO + tm] + cw[2:3] * g_next + cb_ref[:, cs]
        act_ref[:, cs] = (conv * (1.0 / (1.0 + jnp.exp(-conv))) * val).astype(BF16)
    y = x + jnp.dot(act_ref[...], wd_ref[...], preferred_element_type=F32)
    o_ref[...] = _rms(y, gfin_ref[...]) if final_norm else y


def _mix_ffn(attn, four, x2d, ga, gf, w_out, gain, wg, wv, conv_w, conv_b, wd, gfin, tm, fc, final_norm):
    batch, n_hp, seq, _ = attn.shape
    fw = four.shape[-1]
    n_tok, d_model = x2d.shape
    d_ff = wg.shape[1]
    nt = seq // tm
    per_tile = tm // HALO
    last = seq // HALO - 1
    prev_blk = lambda i: jnp.maximum((i % nt) * per_tile - 1, 0)
    next_blk = lambda i: jnp.minimum((i % nt + 1) * per_tile, last)
    seq_blk = lambda i: (i // nt) * (last + 1)
    const = lambda shape: pl.BlockSpec(shape, lambda i: (0,) * len(shape), pipeline_mode=pl.Buffered(1))

    def triple(cur_shape, halo_shape, index):
        return [pl.BlockSpec(halo_shape, lambda i: index(i // nt, prev_blk(i))),
                pl.BlockSpec(cur_shape, lambda i: index(i // nt, i % nt)),
                pl.BlockSpec(halo_shape, lambda i: index(i // nt, next_blk(i)))]

    in_specs = (triple((None, n_hp, tm, LANES), (None, n_hp, HALO, LANES), lambda b, t: (b, 0, t, 0))
                + triple((None, tm, fw), (None, HALO, fw), lambda b, t: (b, t, 0))
                + [pl.BlockSpec((HALO, d_model), lambda i: (seq_blk(i) + prev_blk(i), 0)),
                   pl.BlockSpec((tm, d_model), lambda i: (i, 0)),
                   pl.BlockSpec((HALO, d_model), lambda i: (seq_blk(i) + next_blk(i), 0))]
                + [const((1, n_hp * LANES)), const((1, fw)), const(w_out.shape), const((1, d_model)),
                   const(wg.shape), const(wv.shape), const(conv_w.shape), const((1, d_ff)), const(wd.shape),
                   const((1, d_model))])
    return pl.pallas_call(
        functools.partial(_mix_ffn_kernel, tiles_per_seq=nt, fc=fc, final_norm=final_norm, n_hp=n_hp),
        grid=(n_tok // tm,),
        in_specs=in_specs,
        out_specs=pl.BlockSpec((tm, d_model), lambda i: (i, 0)),
        out_shape=jax.ShapeDtypeStruct((n_tok, d_model), F32),
        scratch_shapes=[pltpu.VMEM((tm, d_ff), BF16)],
        compiler_params=_cparams(1),
        name="outproj_convglu_ffn",
    )(attn, attn, attn, four, four, four, x2d, x2d, x2d, ga, gf, w_out, gain, wg, wv, conv_w, conv_b, wd, gfin)


def kernel(x, norm_mix_gain, w_in, attn_out_gain, rel_bias_table, fourier_w, fourier_b, fourier_out_gain, w_out, norm_ffn_gain, w_gate, w_val, conv_w, conv_b, w_down, final_norm_gain):
    batch, seq, d_model = x.shape
    depth = w_in.shape[0]
    n_heads = rel_bias_table.shape[1]
    attn_w = n_heads * HEAD_DIM
    n_hp = attn_w // LANES
    fw = fourier_w.shape[1] * fourier_w.shape[2]
    assert all(w // (2 * d) == HALF_WINDOW for w, d in DILATED_PATTERNS)
    dilations = sorted((d for _, d in DILATED_PATTERNS), reverse=True)
    tm = 512
    row = lambda g: g.reshape(1, -1).astype(F32)
    perm = _group_permutation()
    bias = _bias_tiles(rel_bias_table, dilations)

    x2d = x.reshape(batch * seq, d_model)
    for layer in range(depth):
        w_l = w_in[layer]
        wp, wq = _fold_fourier_weights(w_l[:, 3 * attn_w:], fourier_w[layer], seq)
        w_all = jnp.concatenate([w_l[:, :attn_w] * (HEAD_DIM ** -0.5 * LOG2E), w_l[:, attn_w:3 * attn_w], wp, wq],
                                axis=1).astype(BF16)
        q, k, v, qr, kr, vr, pq = _inproj(x2d, row(norm_mix_gain[layer]), w_all, perm, batch, seq, n_hp, fw, tm)
        attn = _attention(q, k, v, qr, kr, vr, bias, perm, dilations)
        four = _fourier(pq, fourier_b[layer].reshape(-1), perm)
        x2d = _mix_ffn(attn, four, x2d, row(attn_out_gain[layer]), row(fourier_out_gain[layer]),
                       w_out[layer].astype(BF16), row(norm_ffn_gain[layer]), w_gate[layer].astype(BF16),
                       w_val[layer].astype(BF16), conv_w[layer].astype(F32), row(conv_b[layer]),
                       w_down[layer].astype(BF16), row(final_norm_gain), 2 * tm, 256, layer == depth - 1)
    return x2d.reshape(batch, seq, d_model)
```

```python
import functools
import math

import numpy as np
import jax
import jax.numpy as jnp
from jax import lax
from jax.experimental import pallas as pl
from jax.experimental.pallas import tpu as pltpu

EPS = 1e-6
NEG_INF = -1e30
LOG2E = math.log2(math.e)
HEAD_DIM = 64
DILATED_PATTERNS = ((128, 1), (512, 4), (2048, 16))
N_REL_BUCKETS = 32
REL_MAX_DISTANCE = 1024

LANES = 128
BF16_ROWS = 16
HALF_WINDOW = 64
TQ = 2 * HALF_WINDOW
TK = 4 * HALF_WINDOW
MAX_DIL = max(d for _, d in DILATED_PATTERNS)
GROUP = MAX_DIL * BF16_ROWS
BLOCKS_IN_FLIGHT = 32
SLABS_PER_STEP = 2
FFT_RADIX = 64
HALO = BF16_ROWS
VMEM_LIMIT = 56 * 1024 * 1024

BF16 = jnp.bfloat16
F32 = jnp.float32


def _cparams(n_axes):
    return pltpu.CompilerParams(dimension_semantics=("arbitrary",) * n_axes,
                                vmem_limit_bytes=VMEM_LIMIT)


def _rms(x, gain):
    ms = jnp.mean(x * x, axis=-1, keepdims=True)
    return x * lax.rsqrt(ms + EPS) * gain


def _group_permutation():
    t = np.arange(GROUP)
    swapped = (t % MAX_DIL) * BF16_ROWS + t // MAX_DIL
    perm = np.zeros((GROUP, GROUP), np.float32)
    perm[swapped, t] = 1.0
    assert MAX_DIL == BF16_ROWS and np.array_equal(perm, perm.T)
    return jnp.asarray(perm, BF16)


def _prep_kernel(fw_ref, cc_ref, sc_ref, ab_ref, *, groups):
    hi = lax.Precision.HIGHEST
    g = pl.program_id(0)
    fw = fw_ref[...]
    a = jnp.dot(cc_ref[...], fw, precision=hi, preferred_element_type=F32)
    b = jnp.dot(sc_ref[...], fw, precision=hi, preferred_element_type=F32)
    gd = fw.shape[0]
    wide = jnp.concatenate([a] * groups + [b] * groups, axis=1)
    col_group = (lax.broadcasted_iota(jnp.int32, wide.shape, 1) // gd) % groups
    ab_ref[...] = jnp.where(col_group == g, wide, 0.0)


def _fold_fourier_weights(fourier_w, seq):
    groups, gd, _ = fourier_w.shape
    ang = 2.0 * np.pi * np.outer(np.arange(gd), np.arange(gd)) / gd
    scale = 1.0 / math.sqrt(seq * gd)
    cc = jnp.asarray(np.cos(ang) * scale, F32)
    sc = jnp.asarray(np.sin(ang) * scale, F32)
    return pl.pallas_call(
        functools.partial(_prep_kernel, groups=groups),
        grid=(groups,),
        in_specs=[pl.BlockSpec((None, gd, gd), lambda g: (g, 0, 0)),
                  pl.BlockSpec((gd, gd), lambda g: (0, 0)),
                  pl.BlockSpec((gd, gd), lambda g: (0, 0))],
        out_specs=pl.BlockSpec((gd, 2 * groups * gd), lambda g: (g, 0)),
        out_shape=jax.ShapeDtypeStruct((groups * gd, 2 * groups * gd), F32),
        compiler_params=_cparams(1),
        name="fourier_weight_fold",
    )(fourier_w, cc, sc)


def _inproj_kernel(x_ref, g_ref, w_ref, perm_ref, fperm_ref, ab_ref, q_ref, k_ref, v_ref, qr_ref, kr_ref,
                   vr_ref, pq_ref, u_ref, *, n_hp, fw):
    h = _rms(x_ref[...], g_ref[...]).astype(BF16)
    tm = h.shape[0]
    perm = perm_ref[...]

    def proj(c0, n):
        return jnp.dot(h, w_ref[:, c0:c0 + n], preferred_element_type=F32)

    aw = n_hp * LANES

    def project(t, tok_ref):
        for c in range(n_hp // 2):
            res = proj(t * aw + c * 2 * LANES, 2 * LANES).astype(BF16)
            tok_ref[2 * c] = res[:, :LANES]
            tok_ref[2 * c + 1] = res[:, LANES:]

    def permute(tok_ref, res_ref):
        for c in range(n_hp // 2):
            for g in range(tm // GROUP):
                tok = jnp.concatenate([tok_ref[2 * c, g * GROUP:(g + 1) * GROUP, :],
                                       tok_ref[2 * c + 1, g * GROUP:(g + 1) * GROUP, :]], axis=1)
                rows = jnp.dot(perm, tok, preferred_element_type=F32).astype(BF16)
                for half in range(2):
                    blk = rows[:, half * LANES:(half + 1) * LANES].reshape(MAX_DIL, BF16_ROWS, LANES)
                    res_ref[2 * c + half, :, g * BF16_ROWS:(g + 1) * BF16_ROWS, :] = blk

    u_ref[...] = proj(3 * aw, fw).astype(BF16)
    project(0, q_ref)
    project(1, k_ref)
    permute(q_ref, qr_ref)
    u_rows = jnp.dot(fperm_ref[...], u_ref[...], preferred_element_type=F32).astype(BF16)
    project(2, v_ref)
    permute(k_ref, kr_ref)
    planes = jnp.dot(u_rows, ab_ref[...], preferred_element_type=F32)
    planes = planes.reshape(FFT_RADIX, tm // FFT_RADIX, 2 * fw)
    for t in range(2):
        pq_ref[t] = planes[:, :, t * fw:(t + 1) * fw]
    permute(v_ref, vr_ref)


def _inproj(x2d, gain, w_all, perm, ab, batch, seq, n_hp, fw, tm):
    n_tok, d_model = x2d.shape
    nt = seq // tm
    tok_map = lambda i: (i // nt, 0, i % nt, 0)
    res_map = lambda i: (i // nt, 0, 0, i % nt, 0)
    tok_shape = jax.ShapeDtypeStruct((batch, n_hp, seq, LANES), BF16)
    res_shape = jax.ShapeDtypeStruct((batch, n_hp, MAX_DIL, seq // MAX_DIL, LANES), BF16)
    R = FFT_RADIX
    t = np.arange(tm)
    fperm = np.zeros((tm, tm), np.float32)
    fperm[(t % R) * (tm // R) + t // R, t] = 1.0
    fperm = jnp.asarray(fperm, BF16)
    const = lambda shape: pl.BlockSpec(shape, lambda i: (0,) * len(shape), pipeline_mode=pl.Buffered(1))
    return pl.pallas_call(
        functools.partial(_inproj_kernel, n_hp=n_hp, fw=fw),
        grid=(n_tok // tm,),
        in_specs=[pl.BlockSpec((tm, d_model), lambda i: (i, 0)),
                  const((1, d_model)), const(w_all.shape), const(perm.shape), const(fperm.shape),
                  const(ab.shape)],
        out_specs=[pl.BlockSpec((None, n_hp, tm, LANES), tok_map)] * 3
        + [pl.BlockSpec((None, n_hp, MAX_DIL, tm // MAX_DIL, LANES), res_map)] * 3
        + [pl.BlockSpec((None, 2, R, tm // R, fw), res_map)],
        out_shape=[tok_shape] * 3 + [res_shape] * 3
        + [jax.ShapeDtypeStruct((batch, 2, R, seq // R, fw), F32)],
        scratch_shapes=[pltpu.VMEM((tm, fw), BF16)],
        compiler_params=_cparams(1),
        name="rmsnorm_inproj",
    )(x2d, gain, w_all, perm, fperm, ab)


def _t5_bucket_static(rel, dtype):
    nb = N_REL_BUCKETS // 2
    max_exact = nb // 2
    n = np.abs(rel)
    nf = np.maximum(n, 1).astype(dtype)
    large = max_exact + (np.log(nf / dtype(max_exact)) / dtype(math.log(REL_MAX_DISTANCE / max_exact))
                         * dtype(nb - max_exact)).astype(np.int32)
    large = np.minimum(large, nb - 1)
    return np.where(rel > 0, nb, 0) + np.where(n < max_exact, n, large)


def _bucket_tiles(dilations):
    qi = np.arange(TQ)[:, None]
    kc = np.arange(TK)[None, :]
    offsets = np.array([0, HALF_WINDOW, 2 * HALF_WINDOW])
    rel = kc[None] - offsets[:, None, None] - qi[None]
    tiles = []
    for d in dilations:
        bkt = _t5_bucket_static(rel * d, np.float32)
        assert np.array_equal(bkt, _t5_bucket_static(rel * d, np.float64))
        tile = np.where(np.abs(rel) <= HALF_WINDOW, bkt, -1)
        pieces = MAX_DIL // d if 1 < d < MAX_DIL else 1
        q_order = (np.arange(TQ) % (TQ // pieces)) * pieces + np.arange(TQ) // (TQ // pieces)
        k_order = (np.arange(TK) % (TK // pieces)) * pieces + np.arange(TK) // (TK // pieces)
        tiles.append(tile[:, q_order][:, :, k_order])
    return np.concatenate(tiles, axis=0).astype(np.int32)


def _bias_kernel(table_ref, bkt_ref, out_ref, *, n_heads):
    bkt = bkt_ref[...]
    for h in range(n_heads):
        acc = jnp.full(bkt.shape, NEG_INF, F32)
        for b in range(N_REL_BUCKETS):
            acc = jnp.where(bkt == b, table_ref[b, h], acc)
        out_ref[h] = acc * LOG2E


def _bias_tiles(rel_table, dilations):
    n_heads = rel_table.shape[1]
    bkt = jnp.asarray(_bucket_tiles(dilations))
    return pl.pallas_call(
        functools.partial(_bias_kernel, n_heads=n_heads),
        grid=(bkt.shape[0],),
        in_specs=[pl.BlockSpec(memory_space=pltpu.SMEM),
                  pl.BlockSpec((None, TQ, TK), lambda t: (t, 0, 0))],
        out_specs=pl.BlockSpec((None, n_heads, TQ, TK), lambda t: (t, 0, 0, 0)),
        out_shape=jax.ShapeDtypeStruct((bkt.shape[0], n_heads, TQ, TK), F32),
        compiler_params=_cparams(1),
        name="rel_bias_tiles",
    )(rel_table.astype(F32), bkt)


def _block_attention(qb, kb, vb, bias2, first_head):
    nt_dims = (((1,), (1,)), ((), ()))
    zero = jnp.zeros_like(qb)
    q2 = jnp.concatenate([jnp.where(first_head, qb, zero), jnp.where(first_head, zero, qb)], axis=0)
    s = lax.dot_general(q2, kb, nt_dims, preferred_element_type=F32) + bias2
    m = jnp.max(s, axis=-1, keepdims=True)
    p = jnp.exp2(s - m)
    l = jnp.sum(p, axis=-1, keepdims=True)
    pv = jnp.dot(p.astype(BF16), vb, preferred_element_type=F32)
    num = jnp.where(first_head, pv[:TQ], pv[TQ:])
    den = jnp.where(first_head, l[:TQ], l[TQ:])
    top = jnp.where(first_head, m[:TQ], m[TQ:])
    return num * (1.0 / den), top + jnp.log2(den)


def _merge(o_a, lse_a, o_b, lse_b):
    top = jnp.maximum(lse_a, lse_b)
    w_a = jnp.exp2(lse_a - top)
    w_b = jnp.exp2(lse_b - top)
    den = w_a + w_b
    return (w_a * o_a + w_b * o_b) * (1.0 / den), top + jnp.log2(den)


def _split_f32(x):
    hi = x.astype(BF16)
    return hi, (x - hi.astype(F32)).astype(BF16)


def _attn_kernel(*refs, dilation, merge_in, split_out):
    q_ref, k_ref, v_ref, bias_ref = refs[:4]
    pos = 4
    if merge_in is not None:
        po_ref, pl_ref = refs[pos:pos + 2]
        pos += 2
    o_ref = refs[pos]
    lse_refs = refs[pos + 1:]
    n_slabs = q_ref.shape[0]
    pieces = MAX_DIL // dilation if dilation > 1 else 1
    sub_len = q_ref.shape[-2] * pieces
    n_blk = sub_len // TQ
    tq_p, tk_p = TQ // pieces, TK // pieces
    first_head = lax.broadcasted_iota(jnp.int32, (TQ, LANES), 1) < HEAD_DIM

    def load(ref, hh, r, row, n_rows):
        if dilation == 1:
            return ref[hh, pl.ds(row, n_rows), :]
        parts = [ref[hh, a * dilation + r, pl.ds(row, n_rows), :] for a in range(pieces)]
        return parts[0] if pieces == 1 else jnp.concatenate(parts, axis=0)

    def store(ref, hh, r, row, n_rows, val):
        if dilation == 1:
            ref[hh, pl.ds(row, n_rows), :] = val
        else:
            for a in range(pieces):
                ref[hh, a * dilation + r, pl.ds(row, n_rows), :] = val[a * n_rows:(a + 1) * n_rows]

    def block(n, carry):
        q_row = pl.multiple_of(n * tq_p, tq_p)
        k_row = pl.multiple_of(jnp.clip(n * tq_p - tk_p // 4, 0, sub_len // pieces - tk_p), tk_p // 4)
        edge = jnp.where(n > 0, 1, 0) + jnp.where(n == n_blk - 1, 1, 0)
        for r in range(dilation):
            for hh in range(n_slabs):
                qb = load(q_ref, hh, r, q_row, tq_p)
                kb = load(k_ref, hh, r, k_row, tk_p)
                vb = load(v_ref, hh, r, k_row, tk_p)
                bias2 = bias_ref[edge, 2 * hh:2 * hh + 2].reshape(2 * TQ, TK)
                o, lse = _block_attention(qb, kb, vb, bias2, first_head)
                if merge_in is not None:
                    o, lse = _merge(o, lse, load(po_ref, hh, r, q_row, tq_p).astype(F32),
                                    load(pl_ref, hh, r, q_row, tq_p))
                store(o_ref, hh, r, q_row, tq_p, o.astype(BF16))
                if split_out:
                    hi, lo = _split_f32(lse)
                    store(lse_refs[0], hh, r, q_row, tq_p, hi)
                    store(lse_refs[1], hh, r, q_row, tq_p, lo)
                elif lse_refs:
                    store(lse_refs[0], hh, r, q_row, tq_p, lse)
        return carry

    lax.fori_loop(0, n_blk, block, 0, unroll=max(1, BLOCKS_IN_FLIGHT // dilation))


def _attn_token_kernel(q_ref, k_ref, v_ref, bias_ref, perm_ref, po_ref, hi_ref, lo_ref, o_ref,
                       o_tok, lse_tok):
    perm = perm_ref[...]
    for hh in range(po_ref.shape[0]):
        for g in range(po_ref.shape[2] // BF16_ROWS):
            rows = slice(g * BF16_ROWS, (g + 1) * BF16_ROWS)
            o_hi = jnp.concatenate([po_ref[hh, :, rows, :].reshape(GROUP, LANES),
                                    hi_ref[hh, :, rows, :].reshape(GROUP, LANES)], axis=1)
            tok = jnp.dot(perm, o_hi, preferred_element_type=F32)
            lo = jnp.dot(perm, lo_ref[hh, :, rows, :].reshape(GROUP, LANES), preferred_element_type=F32)
            o_tok[hh, g * GROUP:(g + 1) * GROUP, :] = tok[:, :LANES].astype(BF16)
            lse_tok[hh, g * GROUP:(g + 1) * GROUP, :] = tok[:, LANES:] + lo
    _attn_kernel(q_ref, k_ref, v_ref, bias_ref, o_tok, lse_tok, o_ref,
                 dilation=1, merge_in='token', split_out=False)


def _attention(q, k, v, qr, kr, vr, bias, perm, dilations):
    batch, n_hp, seq, _ = q.shape
    assert dilations[0] == MAX_DIL and dilations[-1] == 1 and n_hp % SLABS_PER_STEP == 0
    ns = SLABS_PER_STEP
    res_slab = pl.BlockSpec((None, ns, MAX_DIL, seq // MAX_DIL, LANES), lambda b, h: (b, h, 0, 0, 0))
    tok_slab = pl.BlockSpec((None, ns, seq, LANES), lambda b, h: (b, h, 0, 0))
    res_shape = lambda dt: jax.ShapeDtypeStruct((batch, n_hp, MAX_DIL, seq // MAX_DIL, LANES), dt)
    bias_spec = lambda i: pl.BlockSpec((3, 2 * ns, TQ, TK), lambda b, h: (i, h, 0, 0))

    merged = None
    for i, d in enumerate(dilations[:-1]):
        final_res = i == len(dilations) - 2
        args = [qr, kr, vr, bias] + (list(merged) if merged else [])
        in_specs = [res_slab] * 3 + [bias_spec(i)] + ([res_slab] * 2 if merged else [])
        out_shape = [res_shape(BF16)] + ([res_shape(BF16)] * 2 if final_res else [res_shape(F32)])
        merged = pl.pallas_call(
            functools.partial(_attn_kernel, dilation=d, merge_in='f32' if merged else None,
                              split_out=final_res),
            grid=(batch, n_hp // ns),
            in_specs=in_specs,
            out_specs=[res_slab] * len(out_shape),
            out_shape=out_shape,
            compiler_params=_cparams(2),
            name=f"dilated_attn_d{d}",
        )(*args)

    return pl.pallas_call(
        _attn_token_kernel,
        grid=(batch, n_hp // ns),
        in_specs=[tok_slab] * 3 + [bias_spec(len(dilations) - 1), pl.BlockSpec(perm.shape, lambda b, h: (0, 0))]
        + [res_slab] * 3,
        out_specs=tok_slab,
        out_shape=jax.ShapeDtypeStruct((batch, n_hp, seq, LANES), BF16),
        scratch_shapes=[pltpu.VMEM((ns, seq, LANES), BF16), pltpu.VMEM((ns, seq, LANES), F32)],
        compiler_params=_cparams(2),
        name="dilated_attn_d1",
    )(q, k, v, bias, perm, *merged)


def _swap_16x16(mats, perm):
    out = []
    for g in range(mats[0].shape[0] // BF16_ROWS):
        rows = jnp.concatenate([m[g * BF16_ROWS:(g + 1) * BF16_ROWS] for m in mats], axis=0)
        swapped = jnp.dot(perm, rows, preferred_element_type=F32).astype(BF16)
        out.append(swapped.reshape(BF16_ROWS, MAX_DIL, rows.shape[-1]))
    return out


def _fft_kernel(w_ref, m_ref, perm_ref, b_ref, v_ref, o_ref, z_ref):
    w = w_ref[...]
    perm = perm_ref[...]
    R = v_ref.shape[2]
    grp = BF16_ROWS
    for bg in range(R // grp):
        zs = []
        for j in range(grp):
            b = bg * grp + j
            rhs = jnp.concatenate([v_ref[0, b], v_ref[1, b]], axis=0).astype(BF16)
            zs.append(jnp.dot(w, rhs, preferred_element_type=F32).astype(BF16))
        for plane in range(2):
            groups = _swap_16x16([z[plane * R:(plane + 1) * R] for z in zs], perm)
            for g, blk in enumerate(groups):
                z_ref[g * grp:(g + 1) * grp, plane, bg * grp:(bg + 1) * grp, :] = blk
    for kg in range(R // grp):
        xs = []
        for j in range(grp):
            k1 = kg * grp + j
            zc = z_ref[k1].reshape(2 * R, z_ref.shape[3])
            xs.append((jnp.dot(m_ref[k1], zc, preferred_element_type=F32) + b_ref[...]).astype(BF16))
        for g, blk in enumerate(_swap_16x16(xs, perm)):
            o_ref[g * grp:(g + 1) * grp, kg, :, :] = blk


def _fourier(pq, fourier_b, perm):
    batch, _, R, _, fw = pq.shape
    seq = R * R
    assert R == FFT_RADIX and R % BF16_ROWS == 0 and MAX_DIL == BF16_ROWS
    i = np.arange(R)
    ang1 = 2.0 * np.pi * np.outer(i, i) / R
    c1, s1 = np.cos(ang1), np.sin(ang1)
    w_cat = jnp.asarray(np.block([[c1, -s1], [-s1, -c1]]), BF16)
    k_all = i[:, None, None] + R * i[None, :, None]
    ang2 = 2.0 * np.pi * ((k_all * i[None, None, :]) % seq) / seq
    m_cat = jnp.asarray(np.concatenate([np.cos(ang2), np.sin(ang2)], axis=-1), BF16)
    grp = BF16_ROWS
    const = lambda shape: pl.BlockSpec(shape, lambda b: (0,) * len(shape), pipeline_mode=pl.Buffered(1))
    out = pl.pallas_call(
        _fft_kernel,
        grid=(batch,),
        in_specs=[const((2 * R, 2 * R)), const(m_cat.shape), const(perm.shape), const((1, fw)),
                  pl.BlockSpec((None, 2, R, R, fw), lambda b: (b, 0, 0, 0, 0))],
        out_specs=pl.BlockSpec((None, R, R // grp, grp, fw), lambda b: (b, 0, 0, 0, 0)),
        out_shape=jax.ShapeDtypeStruct((batch, R, R // grp, grp, fw), BF16),
        scratch_shapes=[pltpu.VMEM((R, 2, R, fw), BF16)],
        compiler_params=_cparams(1),
        name="fft_positions",
    )(w_cat, m_cat, perm, fourier_b.reshape(1, fw).astype(F32), pq)
    return out.reshape(batch, seq, fw)


def _mix_ffn_kernel(ap_ref, a_ref, an_ref, fp_ref, f_ref, fn_ref, xp_ref, x_ref, xn_ref,
                    ga_ref, gf_ref, wo_ref, g_ref, wg_ref, wv_ref, cw_ref, cb_ref, wd_ref, gfin_ref,
                    o_ref, act_ref, *, tiles_per_seq, fc, final_norm, n_hp):
    i = pl.program_id(0)
    tm = x_ref.shape[0]
    n_ext = tm + 2 * HALO
    ext = lambda prev, cur, nxt: jnp.concatenate([prev, cur, nxt], axis=0)

    parts = [ext(ap_ref[j], a_ref[j], an_ref[j]).astype(F32) for j in range(n_hp)]
    ssq = functools.reduce(lambda a, b: a + b,
                           [jnp.sum(p * p, axis=-1, keepdims=True) for p in parts])
    inv = lax.rsqrt(ssq / (n_hp * LANES) + EPS)
    ga = ga_ref[...]
    cols = [(p * inv * ga[:, j * LANES:(j + 1) * LANES]).astype(BF16) for j, p in enumerate(parts)]
    cols.append(_rms(ext(fp_ref[...], f_ref[...], fn_ref[...]).astype(F32), gf_ref[...]).astype(BF16))
    mixed = jnp.concatenate(cols, axis=1)
    x_ext = ext(xp_ref[...], x_ref[...], xn_ref[...]) + jnp.dot(mixed, wo_ref[...],
                                                                 preferred_element_type=F32)
    x = x_ext[HALO:HALO + tm]

    keep_prev = (i % tiles_per_seq != 0).astype(F32)
    keep_next = (i % tiles_per_seq != tiles_per_seq - 1).astype(F32)
    row = lax.broadcasted_iota(jnp.int32, (n_ext, 1), 0)
    keep = jnp.where(row < HALO, keep_prev, jnp.where(row >= HALO + tm, keep_next, 1.0))
    h_ext = (_rms(x_ext, g_ref[...]) * keep).astype(BF16)
    h = h_ext[HALO:HALO + tm]
    d_ff = wg_ref.shape[1]

    def up(c0):
        cs = slice(c0, c0 + fc)
        return (jnp.dot(h_ext, wg_ref[:, cs], preferred_element_type=F32),
                jnp.dot(h, wv_ref[:, cs], preferred_element_type=F32))

    nxt = up(0)
    for c0 in range(0, d_ff, fc):
        cs = slice(c0, c0 + fc)
        g, val = nxt
        if c0 + fc < d_ff:
            nxt = up(c0 + fc)
        g_prev = pltpu.roll(g, 1, axis=0)[HALO:HALO + tm]
        g_next = pltpu.roll(g, n_ext - 1, axis=0)[HALO:HALO + tm]
        cw = cw_ref[:, cs]
        conv = cw[0:1] * g_prev + cw[1:2] * g[HALO:HALO + tm] + cw[2:3] * g_next + cb_ref[:, cs]
        act_ref[:, cs] = (conv * (1.0 / (1.0 + jnp.exp(-conv))) * val).astype(BF16)
    y = x + jnp.dot(act_ref[...], wd_ref[...], preferred_element_type=F32)
    o_ref[...] = _rms(y, gfin_ref[...]) if final_norm else y


def _mix_ffn(attn, four, x2d, ga, gf, w_out, gain, wg, wv, conv_w, conv_b, wd, gfin, tm, fc, final_norm):
    batch, n_hp, seq, _ = attn.shape
    fw = four.shape[-1]
    n_tok, d_model = x2d.shape
    d_ff = wg.shape[1]
    nt = seq // tm
    per_tile = tm // HALO
    last = seq // HALO - 1
    prev_blk = lambda i: jnp.maximum((i % nt) * per_tile - 1, 0)
    next_blk = lambda i: jnp.minimum((i % nt + 1) * per_tile, last)
    seq_blk = lambda i: (i // nt) * (last + 1)
    const = lambda shape: pl.BlockSpec(shape, lambda i: (0,) * len(shape), pipeline_mode=pl.Buffered(1))

    def triple(cur_shape, halo_shape, index):
        return [pl.BlockSpec(halo_shape, lambda i: index(i // nt, prev_blk(i))),
                pl.BlockSpec(cur_shape, lambda i: index(i // nt, i % nt)),
                pl.BlockSpec(halo_shape, lambda i: index(i // nt, next_blk(i)))]

    in_specs = (triple((None, n_hp, tm, LANES), (None, n_hp, HALO, LANES), lambda b, t: (b, 0, t, 0))
                + triple((None, tm, fw), (None, HALO, fw), lambda b, t: (b, t, 0))
                + [pl.BlockSpec((HALO, d_model), lambda i: (seq_blk(i) + prev_blk(i), 0)),
                   pl.BlockSpec((tm, d_model), lambda i: (i, 0)),
                   pl.BlockSpec((HALO, d_model), lambda i: (seq_blk(i) + next_blk(i), 0))]
                + [const((1, n_hp * LANES)), const((1, fw)), const(w_out.shape), const((1, d_model)),
                   const(wg.shape), const(wv.shape), const(conv_w.shape), const((1, d_ff)), const(wd.shape),
                   const((1, d_model))])
    return pl.pallas_call(
        functools.partial(_mix_ffn_kernel, tiles_per_seq=nt, fc=fc, final_norm=final_norm, n_hp=n_hp),
        grid=(n_tok // tm,),
        in_specs=in_specs,
        out_specs=pl.BlockSpec((tm, d_model), lambda i: (i, 0)),
        out_shape=jax.ShapeDtypeStruct((n_tok, d_model), F32),
        scratch_shapes=[pltpu.VMEM((tm, d_ff), BF16)],
        compiler_params=_cparams(1),
        name="outproj_convglu_ffn",
    )(attn, attn, attn, four, four, four, x2d, x2d, x2d, ga, gf, w_out, gain, wg, wv, conv_w, conv_b, wd, gfin)


def kernel(x, norm_mix_gain, w_in, attn_out_gain, rel_bias_table, fourier_w, fourier_b, fourier_out_gain, w_out, norm_ffn_gain, w_gate, w_val, conv_w, conv_b, w_down, final_norm_gain):
    batch, seq, d_model = x.shape
    depth = w_in.shape[0]
    n_heads = rel_bias_table.shape[1]
    attn_w = n_heads * HEAD_DIM
    n_hp = attn_w // LANES
    fw = fourier_w.shape[1] * fourier_w.shape[2]
    assert all(w // (2 * d) == HALF_WINDOW for w, d in DILATED_PATTERNS)
    dilations = sorted((d for _, d in DILATED_PATTERNS), reverse=True)
    tm = 512
    row = lambda g: g.reshape(1, -1).astype(F32)
    perm = _group_permutation()
    bias = _bias_tiles(rel_bias_table, dilations)

    x2d = x.reshape(batch * seq, d_model)
    for layer in range(depth):
        w_l = w_in[layer]
        ab = _fold_fourier_weights(fourier_w[layer], seq).astype(BF16)
        w_all = jnp.concatenate([w_l[:, :attn_w] * (HEAD_DIM ** -0.5 * LOG2E), w_l[:, attn_w:]],
                                axis=1).astype(BF16)
        q, k, v, qr, kr, vr, pq = _inproj(x2d, row(norm_mix_gain[layer]), w_all, perm, ab, batch, seq, n_hp,
                                          fw, tm)
        attn = _attention(q, k, v, qr, kr, vr, bias, perm, dilations)
        four = _fourier(pq, fourier_b[layer].reshape(-1), perm)
        x2d = _mix_ffn(attn, four, x2d, row(attn_out_gain[layer]), row(fourier_out_gain[layer]),
                       w_out[layer].astype(BF16), row(norm_ffn_gain[layer]), w_gate[layer].astype(BF16),
                       w_val[layer].astype(BF16), conv_w[layer].astype(F32), row(conv_b[layer]),
                       w_down[layer].astype(BF16), row(final_norm_gain), 2 * tm, 256, layer == depth - 1)
    return x2d.reshape(batch, seq, d_model)
```

```python
import functools
import math

import numpy as np
import jax
import jax.numpy as jnp
from jax import lax
from jax.experimental import pallas as pl
from jax.experimental.pallas import tpu as pltpu

EPS = 1e-6
NEG_INF = -1e30
LOG2E = math.log2(math.e)
HEAD_DIM = 64
DILATED_PATTERNS = ((128, 1), (512, 4), (2048, 16))
N_REL_BUCKETS = 32
REL_MAX_DISTANCE = 1024

LANES = 128
BF16_ROWS = 16
HALF_WINDOW = 64
TQ = 2 * HALF_WINDOW
TK = 4 * HALF_WINDOW
MAX_DIL = max(d for _, d in DILATED_PATTERNS)
GROUP = MAX_DIL * BF16_ROWS
SLABS_PER_STEP = 2
FFT_RADIX = 64
HALO = BF16_ROWS
VMEM_LIMIT = 56 * 1024 * 1024

BF16 = jnp.bfloat16
F32 = jnp.float32


def _cparams(n_axes):
    return pltpu.CompilerParams(dimension_semantics=("arbitrary",) * n_axes,
                                vmem_limit_bytes=VMEM_LIMIT)


def _rms(x, gain):
    ms = jnp.mean(x * x, axis=-1, keepdims=True)
    return x * lax.rsqrt(ms + EPS) * gain


def _group_permutation():
    t = np.arange(GROUP)
    swapped = (t % MAX_DIL) * BF16_ROWS + t // MAX_DIL
    perm = np.zeros((GROUP, GROUP), np.float32)
    perm[swapped, t] = 1.0
    assert MAX_DIL == BF16_ROWS and np.array_equal(perm, perm.T)
    return jnp.asarray(perm, BF16)


def _prep_kernel(fw_ref, cc_ref, sc_ref, ab_ref, *, groups):
    hi = lax.Precision.HIGHEST
    g = pl.program_id(0)
    fw = fw_ref[...]
    a = jnp.dot(cc_ref[...], fw, precision=hi, preferred_element_type=F32)
    b = jnp.dot(sc_ref[...], fw, precision=hi, preferred_element_type=F32)
    gd = fw.shape[0]
    wide = jnp.concatenate([a] * groups + [b] * groups, axis=1)
    col_group = (lax.broadcasted_iota(jnp.int32, wide.shape, 1) // gd) % groups
    ab_ref[...] = jnp.where(col_group == g, wide, 0.0)


def _fold_fourier_weights(fourier_w, seq):
    groups, gd, _ = fourier_w.shape
    ang = 2.0 * np.pi * np.outer(np.arange(gd), np.arange(gd)) / gd
    scale = 1.0 / math.sqrt(seq * gd)
    cc = jnp.asarray(np.cos(ang) * scale, F32)
    sc = jnp.asarray(np.sin(ang) * scale, F32)
    return pl.pallas_call(
        functools.partial(_prep_kernel, groups=groups),
        grid=(groups,),
        in_specs=[pl.BlockSpec((None, gd, gd), lambda g: (g, 0, 0)),
                  pl.BlockSpec((gd, gd), lambda g: (0, 0)),
                  pl.BlockSpec((gd, gd), lambda g: (0, 0))],
        out_specs=pl.BlockSpec((gd, 2 * groups * gd), lambda g: (g, 0)),
        out_shape=jax.ShapeDtypeStruct((groups * gd, 2 * groups * gd), F32),
        compiler_params=_cparams(1),
        name="fourier_weight_fold",
    )(fourier_w, cc, sc)


def _inproj_kernel(x_ref, g_ref, w_ref, perm_ref, fperm_ref, ab_ref, q_ref, k_ref, v_ref, qr_ref, kr_ref,
                   vr_ref, pq_ref, u_ref, *, n_hp, fw):
    h = _rms(x_ref[...], g_ref[...]).astype(BF16)
    tm = h.shape[0]
    perm = perm_ref[...]

    def proj(c0, n):
        return jnp.dot(h, w_ref[:, c0:c0 + n], preferred_element_type=F32)

    aw = n_hp * LANES

    def project(t, tok_ref):
        for c in range(n_hp // 2):
            res = proj(t * aw + c * 2 * LANES, 2 * LANES).astype(BF16)
            tok_ref[2 * c] = res[:, :LANES]
            tok_ref[2 * c + 1] = res[:, LANES:]

    def permute(tok_ref, res_ref):
        for c in range(n_hp // 2):
            for g in range(tm // GROUP):
                tok = jnp.concatenate([tok_ref[2 * c, g * GROUP:(g + 1) * GROUP, :],
                                       tok_ref[2 * c + 1, g * GROUP:(g + 1) * GROUP, :]], axis=1)
                rows = jnp.dot(perm, tok, preferred_element_type=F32).astype(BF16)
                for half in range(2):
                    blk = rows[:, half * LANES:(half + 1) * LANES].reshape(MAX_DIL, BF16_ROWS, LANES)
                    res_ref[2 * c + half, :, g * BF16_ROWS:(g + 1) * BF16_ROWS, :] = blk

    u_ref[...] = proj(3 * aw, fw).astype(BF16)
    project(0, q_ref)
    project(1, k_ref)
    permute(q_ref, qr_ref)
    u_rows = jnp.dot(fperm_ref[...], u_ref[...], preferred_element_type=F32).astype(BF16)
    project(2, v_ref)
    permute(k_ref, kr_ref)
    planes = jnp.dot(u_rows, ab_ref[...], preferred_element_type=F32)
    planes = planes.reshape(FFT_RADIX, tm // FFT_RADIX, 2 * fw)
    for t in range(2):
        pq_ref[t] = planes[:, :, t * fw:(t + 1) * fw]
    permute(v_ref, vr_ref)


def _inproj(x2d, gain, w_all, perm, ab, batch, seq, n_hp, fw, tm):
    n_tok, d_model = x2d.shape
    nt = seq // tm
    tok_map = lambda i: (i // nt, 0, i % nt, 0)
    res_map = lambda i: (i // nt, 0, 0, i % nt, 0)
    tok_shape = jax.ShapeDtypeStruct((batch, n_hp, seq, LANES), BF16)
    res_shape = jax.ShapeDtypeStruct((batch, n_hp, MAX_DIL, seq // MAX_DIL, LANES), BF16)
    R = FFT_RADIX
    t = np.arange(tm)
    fperm = np.zeros((tm, tm), np.float32)
    fperm[(t % R) * (tm // R) + t // R, t] = 1.0
    fperm = jnp.asarray(fperm, BF16)
    const = lambda shape: pl.BlockSpec(shape, lambda i: (0,) * len(shape), pipeline_mode=pl.Buffered(1))
    return pl.pallas_call(
        functools.partial(_inproj_kernel, n_hp=n_hp, fw=fw),
        grid=(n_tok // tm,),
        in_specs=[pl.BlockSpec((tm, d_model), lambda i: (i, 0)),
                  const((1, d_model)), const(w_all.shape), const(perm.shape), const(fperm.shape),
                  const(ab.shape)],
        out_specs=[pl.BlockSpec((None, n_hp, tm, LANES), tok_map)] * 3
        + [pl.BlockSpec((None, n_hp, MAX_DIL, tm // MAX_DIL, LANES), res_map)] * 3
        + [pl.BlockSpec((None, 2, R, tm // R, fw), res_map)],
        out_shape=[tok_shape] * 3 + [res_shape] * 3
        + [jax.ShapeDtypeStruct((batch, 2, R, seq // R, fw), F32)],
        scratch_shapes=[pltpu.VMEM((tm, fw), BF16)],
        compiler_params=_cparams(1),
        name="rmsnorm_inproj",
    )(x2d, gain, w_all, perm, fperm, ab)


def _t5_bucket_static(rel, dtype):
    nb = N_REL_BUCKETS // 2
    max_exact = nb // 2
    n = np.abs(rel)
    nf = np.maximum(n, 1).astype(dtype)
    large = max_exact + (np.log(nf / dtype(max_exact)) / dtype(math.log(REL_MAX_DISTANCE / max_exact))
                         * dtype(nb - max_exact)).astype(np.int32)
    large = np.minimum(large, nb - 1)
    return np.where(rel > 0, nb, 0) + np.where(n < max_exact, n, large)


def _bucket_tiles(dilations):
    qi = np.arange(TQ)[:, None]
    kc = np.arange(TK)[None, :]
    offsets = np.array([0, HALF_WINDOW, 2 * HALF_WINDOW])
    rel = kc[None] - offsets[:, None, None] - qi[None]
    tiles = []
    for d in dilations:
        bkt = _t5_bucket_static(rel * d, np.float32)
        assert np.array_equal(bkt, _t5_bucket_static(rel * d, np.float64))
        tile = np.where(np.abs(rel) <= HALF_WINDOW, bkt, -1)
        pieces = MAX_DIL // d if 1 < d < MAX_DIL else 1
        q_order = (np.arange(TQ) % (TQ // pieces)) * pieces + np.arange(TQ) // (TQ // pieces)
        k_order = (np.arange(TK) % (TK // pieces)) * pieces + np.arange(TK) // (TK // pieces)
        tiles.append(tile[:, q_order][:, :, k_order])
    return np.concatenate(tiles, axis=0).astype(np.int32)


def _bias_kernel(table_ref, bkt_ref, out_ref, *, n_heads):
    bkt = bkt_ref[...]
    for h in range(n_heads):
        acc = jnp.full(bkt.shape, NEG_INF, F32)
        for b in range(N_REL_BUCKETS):
            acc = jnp.where(bkt == b, table_ref[b, h], acc)
        out_ref[h] = acc * LOG2E


def _bias_tiles(rel_table, dilations):
    n_heads = rel_table.shape[1]
    bkt = jnp.asarray(_bucket_tiles(dilations))
    return pl.pallas_call(
        functools.partial(_bias_kernel, n_heads=n_heads),
        grid=(bkt.shape[0],),
        in_specs=[pl.BlockSpec(memory_space=pltpu.SMEM),
                  pl.BlockSpec((None, TQ, TK), lambda t: (t, 0, 0))],
        out_specs=pl.BlockSpec((None, n_heads, TQ, TK), lambda t: (t, 0, 0, 0)),
        out_shape=jax.ShapeDtypeStruct((bkt.shape[0], n_heads, TQ, TK), F32),
        compiler_params=_cparams(1),
        name="rel_bias_tiles",
    )(rel_table.astype(F32), bkt)


def _block_attention(qb, kb, vb, bias2, first_head):
    nt_dims = (((1,), (1,)), ((), ()))
    zero = jnp.zeros_like(qb)
    q2 = jnp.concatenate([jnp.where(first_head, qb, zero), jnp.where(first_head, zero, qb)], axis=0)
    s = lax.dot_general(q2, kb, nt_dims, preferred_element_type=F32) + bias2
    m = jnp.max(s, axis=-1, keepdims=True)
    p = jnp.exp2(s - m)
    l = jnp.sum(p, axis=-1, keepdims=True)
    pv = jnp.dot(p.astype(BF16), vb, preferred_element_type=F32)
    num = jnp.where(first_head, pv[:TQ], pv[TQ:])
    den = jnp.where(first_head, l[:TQ], l[TQ:])
    top = jnp.where(first_head, m[:TQ], m[TQ:])
    return num, top, den


def _merge(a, b):
    num_a, top_a, den_a = a
    num_b, top_b, den_b = b
    top = jnp.maximum(top_a, top_b)
    w_a = jnp.exp2(top_a - top)
    w_b = jnp.exp2(top_b - top)
    den = (w_a if den_a is None else w_a * den_a) + (w_b if den_b is None else w_b * den_b)
    return w_a * num_a + w_b * num_b, top, den


def _split_f32(x):
    hi = x.astype(BF16)
    return hi, (x - hi.astype(F32)).astype(BF16)


def _attn_kernel(*refs, dilation, merge_in, out):
    q_ref, k_ref, v_ref, bias_ref = refs[:4]
    n_prev = {None: 0, 'raw': 3, 'norm': 2}[merge_in]
    prev_refs = refs[4:4 + n_prev]
    out_refs = refs[4 + n_prev:]
    n_slabs = q_ref.shape[0]
    pieces = MAX_DIL // dilation if dilation > 1 else 1
    sub_len = q_ref.shape[-2] * pieces
    n_blk = sub_len // TQ
    tq_p, tk_p = TQ // pieces, TK // pieces
    first_head = lax.broadcasted_iota(jnp.int32, (TQ, LANES), 1) < HEAD_DIM

    def load(ref, hh, r, row, n_rows):
        if dilation == 1:
            return ref[hh, row:row + n_rows, :]
        parts = [ref[hh, a * dilation + r, row:row + n_rows, :] for a in range(pieces)]
        return parts[0] if pieces == 1 else jnp.concatenate(parts, axis=0)

    def store(ref, hh, r, row, n_rows, val):
        if dilation == 1:
            ref[hh, row:row + n_rows, :] = val
        else:
            for a in range(pieces):
                ref[hh, a * dilation + r, row:row + n_rows, :] = val[a * n_rows:(a + 1) * n_rows]

    for n in range(n_blk):
        q_row = n * tq_p
        k_row = min(max(q_row - tk_p // 4, 0), sub_len // pieces - tk_p)
        edge = (1 if n > 0 else 0) + (1 if n == n_blk - 1 else 0)
        for r in range(dilation):
            for hh in range(n_slabs):
                qb = load(q_ref, hh, r, q_row, tq_p)
                kb = load(k_ref, hh, r, k_row, tk_p)
                vb = load(v_ref, hh, r, k_row, tk_p)
                bias2 = bias_ref[edge, 2 * hh:2 * hh + 2].reshape(2 * TQ, TK)
                part = _block_attention(qb, kb, vb, bias2, first_head)
                if merge_in:
                    prev = [load(ref, hh, r, q_row, tq_p).astype(F32) for ref in prev_refs]
                    part = _merge(part, prev if merge_in == 'raw' else prev + [None])
                num, top, den = part
                if out == 'raw':
                    vals = (num.astype(BF16), top, den)
                else:
                    o = (num * (1.0 / den)).astype(BF16)
                    vals = (o,) + _split_f32(top + jnp.log2(den)) if out == 'split' else (o,)
                for ref, val in zip(out_refs, vals):
                    store(ref, hh, r, q_row, tq_p, val)


def _attn_token_kernel(q_ref, k_ref, v_ref, bias_ref, perm_ref, po_ref, hi_ref, lo_ref, o_ref,
                       o_tok, lse_tok):
    perm = perm_ref[...]
    for hh in range(po_ref.shape[0]):
        for g in range(po_ref.shape[2] // BF16_ROWS):
            rows = slice(g * BF16_ROWS, (g + 1) * BF16_ROWS)
            o_hi = jnp.concatenate([po_ref[hh, :, rows, :].reshape(GROUP, LANES),
                                    hi_ref[hh, :, rows, :].reshape(GROUP, LANES)], axis=1)
            tok = jnp.dot(perm, o_hi, preferred_element_type=F32)
            lo = jnp.dot(perm, lo_ref[hh, :, rows, :].reshape(GROUP, LANES), preferred_element_type=F32)
            o_tok[hh, g * GROUP:(g + 1) * GROUP, :] = tok[:, :LANES].astype(BF16)
            lse_tok[hh, g * GROUP:(g + 1) * GROUP, :] = tok[:, LANES:] + lo
    _attn_kernel(q_ref, k_ref, v_ref, bias_ref, o_tok, lse_tok, o_ref,
                 dilation=1, merge_in='norm', out='final')


def _attention(q, k, v, qr, kr, vr, bias, perm, dilations):
    batch, n_hp, seq, _ = q.shape
    assert dilations[0] == MAX_DIL and dilations[-1] == 1 and n_hp % SLABS_PER_STEP == 0
    ns = SLABS_PER_STEP
    res_slab = pl.BlockSpec((None, ns, MAX_DIL, seq // MAX_DIL, LANES), lambda b, h: (b, h, 0, 0, 0))
    tok_slab = pl.BlockSpec((None, ns, seq, LANES), lambda b, h: (b, h, 0, 0))
    res_shape = lambda dt: jax.ShapeDtypeStruct((batch, n_hp, MAX_DIL, seq // MAX_DIL, LANES), dt)
    bias_spec = lambda i: pl.BlockSpec((3, 2 * ns, TQ, TK), lambda b, h: (i, h, 0, 0))

    merged = None
    for i, d in enumerate(dilations[:-1]):
        final_res = i == len(dilations) - 2
        args = [qr, kr, vr, bias] + (list(merged) if merged else [])
        in_specs = [res_slab] * 3 + [bias_spec(i)] + ([res_slab] * 3 if merged else [])
        out_shape = [res_shape(BF16)] + [res_shape(BF16 if final_res else F32)] * 2
        merged = pl.pallas_call(
            functools.partial(_attn_kernel, dilation=d, merge_in='raw' if merged else None,
                              out='split' if final_res else 'raw'),
            grid=(batch, n_hp // ns),
            in_specs=in_specs,
            out_specs=[res_slab] * len(out_shape),
            out_shape=out_shape,
            compiler_params=_cparams(2),
            name=f"dilated_attn_d{d}",
        )(*args)

    return pl.pallas_call(
        _attn_token_kernel,
        grid=(batch, n_hp // ns),
        in_specs=[tok_slab] * 3 + [bias_spec(len(dilations) - 1), pl.BlockSpec(perm.shape, lambda b, h: (0, 0))]
        + [res_slab] * 3,
        out_specs=tok_slab,
        out_shape=jax.ShapeDtypeStruct((batch, n_hp, seq, LANES), BF16),
        scratch_shapes=[pltpu.VMEM((ns, seq, LANES), BF16), pltpu.VMEM((ns, seq, LANES), F32)],
        compiler_params=_cparams(2),
        name="dilated_attn_d1",
    )(q, k, v, bias, perm, *merged)


def _swap_16x16(mats, perm):
    out = []
    for g in range(mats[0].shape[0] // BF16_ROWS):
        rows = jnp.concatenate([m[g * BF16_ROWS:(g + 1) * BF16_ROWS] for m in mats], axis=0)
        swapped = jnp.dot(perm, rows, preferred_element_type=F32).astype(BF16)
        out.append(swapped.reshape(BF16_ROWS, MAX_DIL, rows.shape[-1]))
    return out


def _fft_kernel(w_ref, m_ref, perm_ref, b_ref, v_ref, o_ref, z_ref):
    w = w_ref[...]
    perm = perm_ref[...]
    R = v_ref.shape[2]
    grp = BF16_ROWS
    for bg in range(R // grp):
        zs = []
        for j in range(grp):
            b = bg * grp + j
            rhs = jnp.concatenate([v_ref[0, b], v_ref[1, b]], axis=0).astype(BF16)
            zs.append(jnp.dot(w, rhs, preferred_element_type=F32).astype(BF16))
        for plane in range(2):
            groups = _swap_16x16([z[plane * R:(plane + 1) * R] for z in zs], perm)
            for g, blk in enumerate(groups):
                z_ref[g * grp:(g + 1) * grp, plane, bg * grp:(bg + 1) * grp, :] = blk
    for kg in range(R // grp):
        xs = []
        for j in range(grp):
            k1 = kg * grp + j
            zc = z_ref[k1].reshape(2 * R, z_ref.shape[3])
            xs.append((jnp.dot(m_ref[k1], zc, preferred_element_type=F32) + b_ref[...]).astype(BF16))
        for g, blk in enumerate(_swap_16x16(xs, perm)):
            o_ref[g * grp:(g + 1) * grp, kg, :, :] = blk


def _fourier(pq, fourier_b, perm):
    batch, _, R, _, fw = pq.shape
    seq = R * R
    assert R == FFT_RADIX and R % BF16_ROWS == 0 and MAX_DIL == BF16_ROWS
    i = np.arange(R)
    ang1 = 2.0 * np.pi * np.outer(i, i) / R
    c1, s1 = np.cos(ang1), np.sin(ang1)
    w_cat = jnp.asarray(np.block([[c1, -s1], [-s1, -c1]]), BF16)
    k_all = i[:, None, None] + R * i[None, :, None]
    ang2 = 2.0 * np.pi * ((k_all * i[None, None, :]) % seq) / seq
    m_cat = jnp.asarray(np.concatenate([np.cos(ang2), np.sin(ang2)], axis=-1), BF16)
    grp = BF16_ROWS
    const = lambda shape: pl.BlockSpec(shape, lambda b: (0,) * len(shape), pipeline_mode=pl.Buffered(1))
    out = pl.pallas_call(
        _fft_kernel,
        grid=(batch,),
        in_specs=[const((2 * R, 2 * R)), const(m_cat.shape), const(perm.shape), const((1, fw)),
                  pl.BlockSpec((None, 2, R, R, fw), lambda b: (b, 0, 0, 0, 0))],
        out_specs=pl.BlockSpec((None, R, R // grp, grp, fw), lambda b: (b, 0, 0, 0, 0)),
        out_shape=jax.ShapeDtypeStruct((batch, R, R // grp, grp, fw), BF16),
        scratch_shapes=[pltpu.VMEM((R, 2, R, fw), BF16)],
        compiler_params=_cparams(1),
        name="fft_positions",
    )(w_cat, m_cat, perm, fourier_b.reshape(1, fw).astype(F32), pq)
    return out.reshape(batch, seq, fw)


def _mix_ffn_kernel(ap_ref, a_ref, an_ref, fp_ref, f_ref, fn_ref, xp_ref, x_ref, xn_ref,
                    ga_ref, gf_ref, wo_ref, g_ref, wg_ref, wv_ref, cw_ref, cb_ref, wd_ref, gfin_ref,
                    o_ref, act_ref, *, tiles_per_seq, fc, final_norm, n_hp):
    i = pl.program_id(0)
    tm = x_ref.shape[0]
    n_ext = tm + 2 * HALO
    ext = lambda prev, cur, nxt: jnp.concatenate([prev, cur, nxt], axis=0)

    parts = [ext(ap_ref[j], a_ref[j], an_ref[j]).astype(F32) for j in range(n_hp)]
    ssq = functools.reduce(lambda a, b: a + b,
                           [jnp.sum(p * p, axis=-1, keepdims=True) for p in parts])
    inv = lax.rsqrt(ssq / (n_hp * LANES) + EPS)
    ga = ga_ref[...]
    cols = [(p * inv * ga[:, j * LANES:(j + 1) * LANES]).astype(BF16) for j, p in enumerate(parts)]
    cols.append(_rms(ext(fp_ref[...], f_ref[...], fn_ref[...]).astype(F32), gf_ref[...]).astype(BF16))
    mixed = jnp.concatenate(cols, axis=1)
    x_ext = ext(xp_ref[...], x_ref[...], xn_ref[...]) + jnp.dot(mixed, wo_ref[...],
                                                                 preferred_element_type=F32)
    x = x_ext[HALO:HALO + tm]

    keep_prev = (i % tiles_per_seq != 0).astype(F32)
    keep_next = (i % tiles_per_seq != tiles_per_seq - 1).astype(F32)
    row = lax.broadcasted_iota(jnp.int32, (n_ext, 1), 0)
    keep = jnp.where(row < HALO, keep_prev, jnp.where(row >= HALO + tm, keep_next, 1.0))
    h_ext = (_rms(x_ext, g_ref[...]) * keep).astype(BF16)
    h = h_ext[HALO:HALO + tm]
    d_ff = wg_ref.shape[1]

    def up(c0):
        cs = slice(c0, c0 + fc)
        return (jnp.dot(h_ext, wg_ref[:, cs], preferred_element_type=F32),
                jnp.dot(h, wv_ref[:, cs], preferred_element_type=F32))

    nxt = up(0)
    for c0 in range(0, d_ff, fc):
        cs = slice(c0, c0 + fc)
        g, val = nxt
        if c0 + fc < d_ff:
            nxt = up(c0 + fc)
        g_prev = pltpu.roll(g, 1, axis=0)[HALO:HALO + tm]
        g_next = pltpu.roll(g, n_ext - 1, axis=0)[HALO:HALO + tm]
        cw = cw_ref[:, cs]
        conv = cw[0:1] * g_prev + cw[1:2] * g[HALO:HALO + tm] + cw[2:3] * g_next + cb_ref[:, cs]
        act_ref[:, cs] = (conv * (1.0 / (1.0 + jnp.exp(-conv))) * val).astype(BF16)
    y = x + jnp.dot(act_ref[...], wd_ref[...], preferred_element_type=F32)
    o_ref[...] = _rms(y, gfin_ref[...]) if final_norm else y


def _mix_ffn(attn, four, x2d, ga, gf, w_out, gain, wg, wv, conv_w, conv_b, wd, gfin, tm, fc, final_norm):
    batch, n_hp, seq, _ = attn.shape
    fw = four.shape[-1]
    n_tok, d_model = x2d.shape
    d_ff = wg.shape[1]
    nt = seq // tm
    per_tile = tm // HALO
    last = seq // HALO - 1
    prev_blk = lambda i: jnp.maximum((i % nt) * per_tile - 1, 0)
    next_blk = lambda i: jnp.minimum((i % nt + 1) * per_tile, last)
    seq_blk = lambda i: (i // nt) * (last + 1)
    const = lambda shape: pl.BlockSpec(shape, lambda i: (0,) * len(shape), pipeline_mode=pl.Buffered(1))

    def triple(cur_shape, halo_shape, index):
        return [pl.BlockSpec(halo_shape, lambda i: index(i // nt, prev_blk(i))),
                pl.BlockSpec(cur_shape, lambda i: index(i // nt, i % nt)),
                pl.BlockSpec(halo_shape, lambda i: index(i // nt, next_blk(i)))]

    in_specs = (triple((None, n_hp, tm, LANES), (None, n_hp, HALO, LANES), lambda b, t: (b, 0, t, 0))
                + triple((None, tm, fw), (None, HALO, fw), lambda b, t: (b, t, 0))
                + [pl.BlockSpec((HALO, d_model), lambda i: (seq_blk(i) + prev_blk(i), 0)),
                   pl.BlockSpec((tm, d_model), lambda i: (i, 0)),
                   pl.BlockSpec((HALO, d_model), lambda i: (seq_blk(i) + next_blk(i), 0))]
                + [const((1, n_hp * LANES)), const((1, fw)), const(w_out.shape), const((1, d_model)),
                   const(wg.shape), const(wv.shape), const(conv_w.shape), const((1, d_ff)), const(wd.shape),
                   const((1, d_model))])
    return pl.pallas_call(
        functools.partial(_mix_ffn_kernel, tiles_per_seq=nt, fc=fc, final_norm=final_norm, n_hp=n_hp),
        grid=(n_tok // tm,),
        in_specs=in_specs,
        out_specs=pl.BlockSpec((tm, d_model), lambda i: (i, 0)),
        out_shape=jax.ShapeDtypeStruct((n_tok, d_model), F32),
        scratch_shapes=[pltpu.VMEM((tm, d_ff), BF16)],
        compiler_params=_cparams(1),
        name="outproj_convglu_ffn",
    )(attn, attn, attn, four, four, four, x2d, x2d, x2d, ga, gf, w_out, gain, wg, wv, conv_w, conv_b, wd, gfin)


def kernel(x, norm_mix_gain, w_in, attn_out_gain, rel_bias_table, fourier_w, fourier_b, fourier_out_gain, w_out, norm_ffn_gain, w_gate, w_val, conv_w, conv_b, w_down, final_norm_gain):
    batch, seq, d_model = x.shape
    depth = w_in.shape[0]
    n_heads = rel_bias_table.shape[1]
    attn_w = n_heads * HEAD_DIM
    n_hp = attn_w // LANES
    fw = fourier_w.shape[1] * fourier_w.shape[2]
    assert all(w // (2 * d) == HALF_WINDOW for w, d in DILATED_PATTERNS)
    dilations = sorted((d for _, d in DILATED_PATTERNS), reverse=True)
    tm = 512
    row = lambda g: g.reshape(1, -1).astype(F32)
    perm = _group_permutation()
    bias = _bias_tiles(rel_bias_table, dilations)

    x2d = x.reshape(batch * seq, d_model)
    for layer in range(depth):
        w_l = w_in[layer]
        ab = _fold_fourier_weights(fourier_w[layer], seq).astype(BF16)
        w_all = jnp.concatenate([w_l[:, :attn_w] * (HEAD_DIM ** -0.5 * LOG2E), w_l[:, attn_w:]],
                                axis=1).astype(BF16)
        q, k, v, qr, kr, vr, pq = _inproj(x2d, row(norm_mix_gain[layer]), w_all, perm, ab, batch, seq, n_hp,
                                          fw, tm)
        attn = _attention(q, k, v, qr, kr, vr, bias, perm, dilations)
        four = _fourier(pq, fourier_b[layer].reshape(-1), perm)
        x2d = _mix_ffn(attn, four, x2d, row(attn_out_gain[layer]), row(fourier_out_gain[layer]),
                       w_out[layer].astype(BF16), row(norm_ffn_gain[layer]), w_gate[layer].astype(BF16),
                       w_val[layer].astype(BF16), conv_w[layer].astype(F32), row(conv_b[layer]),
                       w_down[layer].astype(BF16), row(final_norm_gain), 2 * tm, 256, layer == depth - 1)
    return x2d.reshape(batch, seq, d_model)
```

```python
import functools
import math

import numpy as np
import jax
import jax.numpy as jnp
from jax import lax
from jax.experimental import pallas as pl
from jax.experimental.pallas import tpu as pltpu

EPS = 1e-6
NEG_INF = -1e30
LOG2E = math.log2(math.e)
HEAD_DIM = 64
DILATED_PATTERNS = ((128, 1), (512, 4), (2048, 16))
N_REL_BUCKETS = 32
REL_MAX_DISTANCE = 1024

LANES = 128
BF16_ROWS = 16
HALF_WINDOW = 64
TQ = 2 * HALF_WINDOW
TK = 4 * HALF_WINDOW
MAX_DIL = max(d for _, d in DILATED_PATTERNS)
GROUP = MAX_DIL * BF16_ROWS
SLABS_PER_STEP = 2
FFT_RADIX = 64
HALO = BF16_ROWS
VMEM_LIMIT = 56 * 1024 * 1024

BF16 = jnp.bfloat16
F32 = jnp.float32


def _cparams(n_axes):
    return pltpu.CompilerParams(dimension_semantics=("arbitrary",) * n_axes,
                                vmem_limit_bytes=VMEM_LIMIT)


def _rms(x, gain):
    ms = jnp.mean(x * x, axis=-1, keepdims=True)
    return x * lax.rsqrt(ms + EPS) * gain


def _group_permutation():
    t = np.arange(GROUP)
    swapped = (t % MAX_DIL) * BF16_ROWS + t // MAX_DIL
    perm = np.zeros((GROUP, GROUP), np.float32)
    perm[swapped, t] = 1.0
    assert MAX_DIL == BF16_ROWS and np.array_equal(perm, perm.T)
    return jnp.asarray(perm, BF16)


def _prep_kernel(fw_ref, cc_ref, sc_ref, ab_ref, *, groups):
    hi = lax.Precision.HIGHEST
    g = pl.program_id(0)
    fw = fw_ref[...]
    a = jnp.dot(cc_ref[...], fw, precision=hi, preferred_element_type=F32)
    b = jnp.dot(sc_ref[...], fw, precision=hi, preferred_element_type=F32)
    gd = fw.shape[0]
    wide = jnp.concatenate([a] * groups + [b] * groups, axis=1)
    col_group = (lax.broadcasted_iota(jnp.int32, wide.shape, 1) // gd) % groups
    ab_ref[...] = jnp.where(col_group == g, wide, 0.0)


def _fold_fourier_weights(fourier_w, seq):
    groups, gd, _ = fourier_w.shape
    ang = 2.0 * np.pi * np.outer(np.arange(gd), np.arange(gd)) / gd
    scale = 1.0 / math.sqrt(seq * gd)
    cc = jnp.asarray(np.cos(ang) * scale, F32)
    sc = jnp.asarray(np.sin(ang) * scale, F32)
    return pl.pallas_call(
        functools.partial(_prep_kernel, groups=groups),
        grid=(groups,),
        in_specs=[pl.BlockSpec((None, gd, gd), lambda g: (g, 0, 0)),
                  pl.BlockSpec((gd, gd), lambda g: (0, 0)),
                  pl.BlockSpec((gd, gd), lambda g: (0, 0))],
        out_specs=pl.BlockSpec((gd, 2 * groups * gd), lambda g: (g, 0)),
        out_shape=jax.ShapeDtypeStruct((groups * gd, 2 * groups * gd), F32),
        compiler_params=_cparams(1),
        name="fourier_weight_fold",
    )(fourier_w, cc, sc)


def _inproj_kernel(x_ref, g_ref, w_ref, perm_ref, fperm_ref, ab_ref, q_ref, k_ref, v_ref, qr_ref, kr_ref,
                   vr_ref, pq_ref, u_ref, *, n_hp, fw):
    h = _rms(x_ref[...], g_ref[...]).astype(BF16)
    tm = h.shape[0]
    perm = perm_ref[...]

    def proj(c0, n):
        return jnp.dot(h, w_ref[:, c0:c0 + n], preferred_element_type=F32)

    aw = n_hp * LANES

    def project(t, tok_ref):
        for c in range(n_hp // 2):
            res = proj(t * aw + c * 2 * LANES, 2 * LANES).astype(BF16)
            tok_ref[2 * c] = res[:, :LANES]
            tok_ref[2 * c + 1] = res[:, LANES:]

    def permute(tok_ref, res_ref):
        for c in range(n_hp // 2):
            for g in range(tm // GROUP):
                tok = jnp.concatenate([tok_ref[2 * c, g * GROUP:(g + 1) * GROUP, :],
                                       tok_ref[2 * c + 1, g * GROUP:(g + 1) * GROUP, :]], axis=1)
                rows = jnp.dot(perm, tok, preferred_element_type=F32).astype(BF16)
                for half in range(2):
                    blk = rows[:, half * LANES:(half + 1) * LANES].reshape(MAX_DIL, BF16_ROWS, LANES)
                    res_ref[2 * c + half, :, g * BF16_ROWS:(g + 1) * BF16_ROWS, :] = blk

    u_ref[...] = proj(3 * aw, fw).astype(BF16)
    project(0, q_ref)
    project(1, k_ref)
    permute(q_ref, qr_ref)
    u_rows = jnp.dot(fperm_ref[...], u_ref[...], preferred_element_type=F32).astype(BF16)
    project(2, v_ref)
    permute(k_ref, kr_ref)
    planes = jnp.dot(u_rows, ab_ref[...], preferred_element_type=F32)
    planes = planes.reshape(FFT_RADIX, tm // FFT_RADIX, 2 * fw)
    for t in range(2):
        pq_ref[t] = planes[:, :, t * fw:(t + 1) * fw]
    permute(v_ref, vr_ref)


def _inproj(x2d, gain, w_all, perm, ab, batch, seq, n_hp, fw, tm):
    n_tok, d_model = x2d.shape
    nt = seq // tm
    tok_map = lambda i: (i // nt, 0, i % nt, 0)
    res_map = lambda i: (i // nt, 0, 0, i % nt, 0)
    tok_shape = jax.ShapeDtypeStruct((batch, n_hp, seq, LANES), BF16)
    res_shape = jax.ShapeDtypeStruct((batch, n_hp, MAX_DIL, seq // MAX_DIL, LANES), BF16)
    R = FFT_RADIX
    t = np.arange(tm)
    fperm = np.zeros((tm, tm), np.float32)
    fperm[(t % R) * (tm // R) + t // R, t] = 1.0
    fperm = jnp.asarray(fperm, BF16)
    const = lambda shape: pl.BlockSpec(shape, lambda i: (0,) * len(shape), pipeline_mode=pl.Buffered(1))
    return pl.pallas_call(
        functools.partial(_inproj_kernel, n_hp=n_hp, fw=fw),
        grid=(n_tok // tm,),
        in_specs=[pl.BlockSpec((tm, d_model), lambda i: (i, 0)),
                  const((1, d_model)), const(w_all.shape), const(perm.shape), const(fperm.shape),
                  const(ab.shape)],
        out_specs=[pl.BlockSpec((None, n_hp, tm, LANES), tok_map)] * 3
        + [pl.BlockSpec((None, n_hp, MAX_DIL, tm // MAX_DIL, LANES), res_map)] * 3
        + [pl.BlockSpec((None, 2, R, tm // R, fw), res_map)],
        out_shape=[tok_shape] * 3 + [res_shape] * 3
        + [jax.ShapeDtypeStruct((batch, 2, R, seq // R, fw), F32)],
        scratch_shapes=[pltpu.VMEM((tm, fw), BF16)],
        compiler_params=_cparams(1),
        name="rmsnorm_inproj",
    )(x2d, gain, w_all, perm, fperm, ab)


def _t5_bucket_static(rel, dtype):
    nb = N_REL_BUCKETS // 2
    max_exact = nb // 2
    n = np.abs(rel)
    nf = np.maximum(n, 1).astype(dtype)
    large = max_exact + (np.log(nf / dtype(max_exact)) / dtype(math.log(REL_MAX_DISTANCE / max_exact))
                         * dtype(nb - max_exact)).astype(np.int32)
    large = np.minimum(large, nb - 1)
    return np.where(rel > 0, nb, 0) + np.where(n < max_exact, n, large)


def _bucket_tiles(dilations):
    qi = np.arange(TQ)[:, None]
    kc = np.arange(TK)[None, :]
    offsets = np.array([0, HALF_WINDOW, 2 * HALF_WINDOW])
    rel = kc[None] - offsets[:, None, None] - qi[None]
    tiles = []
    for d in dilations:
        bkt = _t5_bucket_static(rel * d, np.float32)
        assert np.array_equal(bkt, _t5_bucket_static(rel * d, np.float64))
        tile = np.where(np.abs(rel) <= HALF_WINDOW, bkt, -1)
        pieces = MAX_DIL // d if 1 < d < MAX_DIL else 1
        q_order = (np.arange(TQ) % (TQ // pieces)) * pieces + np.arange(TQ) // (TQ // pieces)
        k_order = (np.arange(TK) % (TK // pieces)) * pieces + np.arange(TK) // (TK // pieces)
        tiles.append(tile[:, q_order][:, :, k_order])
    return np.concatenate(tiles, axis=0).astype(np.int32)


def _bias_kernel(table_ref, bkt_ref, out_ref, *, n_heads):
    bkt = bkt_ref[...]
    for h in range(n_heads):
        acc = jnp.full(bkt.shape, NEG_INF, F32)
        for b in range(N_REL_BUCKETS):
            acc = jnp.where(bkt == b, table_ref[b, h], acc)
        out_ref[h] = acc * LOG2E


def _bias_tiles(rel_table, dilations):
    n_heads = rel_table.shape[1]
    bkt = jnp.asarray(_bucket_tiles(dilations))
    return pl.pallas_call(
        functools.partial(_bias_kernel, n_heads=n_heads),
        grid=(bkt.shape[0],),
        in_specs=[pl.BlockSpec(memory_space=pltpu.SMEM),
                  pl.BlockSpec((None, TQ, TK), lambda t: (t, 0, 0))],
        out_specs=pl.BlockSpec((None, n_heads, TQ, TK), lambda t: (t, 0, 0, 0)),
        out_shape=jax.ShapeDtypeStruct((bkt.shape[0], n_heads, TQ, TK), F32),
        compiler_params=_cparams(1),
        name="rel_bias_tiles",
    )(rel_table.astype(F32), bkt)


def _block_attention(qb, kb, vb, bias2, first_head):
    nt_dims = (((1,), (1,)), ((), ()))
    zero = jnp.zeros_like(qb)
    q2 = jnp.concatenate([jnp.where(first_head, qb, zero), jnp.where(first_head, zero, qb)], axis=0)
    s = lax.dot_general(q2, kb, nt_dims, preferred_element_type=F32) + bias2
    m = jnp.max(s, axis=-1, keepdims=True)
    p = jnp.exp2(s - m)
    l = jnp.sum(p, axis=-1, keepdims=True)
    pv = jnp.dot(p.astype(BF16), vb, preferred_element_type=F32)
    num = jnp.where(first_head, pv[:TQ], pv[TQ:])
    den = jnp.where(first_head, l[:TQ], l[TQ:])
    top = jnp.where(first_head, m[:TQ], m[TQ:])
    return num, top, den


def _merge(a, b):
    num_a, top_a, den_a = a
    num_b, top_b, den_b = b
    top = jnp.maximum(top_a, top_b)
    w_a = jnp.exp2(top_a - top)
    w_b = jnp.exp2(top_b - top)
    den = (w_a if den_a is None else w_a * den_a) + (w_b if den_b is None else w_b * den_b)
    return w_a * num_a + w_b * num_b, top, den


def _split_f32(x):
    hi = x.astype(BF16)
    return hi, (x - hi.astype(F32)).astype(BF16)


def _attn_kernel(*refs, dilation, merge_in, out):
    q_ref, k_ref, v_ref, bias_ref = refs[:4]
    n_prev = {None: 0, 'raw': 3, 'norm': 2}[merge_in]
    prev_refs = refs[4:4 + n_prev]
    out_refs = refs[4 + n_prev:]
    n_slabs = q_ref.shape[0]
    pieces = MAX_DIL // dilation if dilation > 1 else 1
    sub_len = q_ref.shape[-2] * pieces
    n_blk = sub_len // TQ
    tq_p, tk_p = TQ // pieces, TK // pieces
    first_head = lax.broadcasted_iota(jnp.int32, (TQ, LANES), 1) < HEAD_DIM

    def load(ref, hh, r, row, n_rows):
        if dilation == 1:
            return ref[hh, row:row + n_rows, :]
        parts = [ref[hh, a * dilation + r, row:row + n_rows, :] for a in range(pieces)]
        return parts[0] if pieces == 1 else jnp.concatenate(parts, axis=0)

    def store(ref, hh, r, row, n_rows, val):
        if dilation == 1:
            ref[hh, row:row + n_rows, :] = val
        else:
            for a in range(pieces):
                ref[hh, a * dilation + r, row:row + n_rows, :] = val[a * n_rows:(a + 1) * n_rows]

    for n in range(n_blk):
        q_row = n * tq_p
        k_row = min(max(q_row - tk_p // 4, 0), sub_len // pieces - tk_p)
        edge = (1 if n > 0 else 0) + (1 if n == n_blk - 1 else 0)
        for r in range(dilation):
            for hh in range(n_slabs):
                qb = load(q_ref, hh, r, q_row, tq_p)
                kb = load(k_ref, hh, r, k_row, tk_p)
                vb = load(v_ref, hh, r, k_row, tk_p)
                bias2 = bias_ref[edge, 2 * hh:2 * hh + 2].reshape(2 * TQ, TK)
                part = _block_attention(qb, kb, vb, bias2, first_head)
                if merge_in:
                    prev = [load(ref, hh, r, q_row, tq_p).astype(F32) for ref in prev_refs]
                    part = _merge(part, prev if merge_in == 'raw' else prev + [None])
                num, top, den = part
                if out == 'raw':
                    vals = (num.astype(BF16), top, den)
                else:
                    o = (num * (1.0 / den)).astype(BF16)
                    vals = (o,) + _split_f32(top + jnp.log2(den)) if out == 'split' else (o,)
                for ref, val in zip(out_refs, vals):
                    store(ref, hh, r, q_row, tq_p, val)


def _attn_residue_kernel(*refs, dilations):
    n = len(dilations)
    q_ref, k_ref, v_ref = refs[:3]
    bias_refs = refs[3:3 + n]
    out_refs = refs[3 + n:6 + n]
    part_refs = refs[6 + n:]
    for i, d in enumerate(dilations):
        last = i == n - 1
        _attn_kernel(q_ref, k_ref, v_ref, bias_refs[i], *(part_refs if i else ()),
                     *(out_refs if last else part_refs),
                     dilation=d, merge_in='raw' if i else None, out='split' if last else 'raw')


def _attn_token_kernel(q_ref, k_ref, v_ref, bias_ref, perm_ref, po_ref, hi_ref, lo_ref, o_ref,
                       o_tok, lse_tok):
    perm = perm_ref[...]
    for hh in range(po_ref.shape[0]):
        for g in range(po_ref.shape[2] // BF16_ROWS):
            rows = slice(g * BF16_ROWS, (g + 1) * BF16_ROWS)
            o_hi = jnp.concatenate([po_ref[hh, :, rows, :].reshape(GROUP, LANES),
                                    hi_ref[hh, :, rows, :].reshape(GROUP, LANES)], axis=1)
            tok = jnp.dot(perm, o_hi, preferred_element_type=F32)
            lo = jnp.dot(perm, lo_ref[hh, :, rows, :].reshape(GROUP, LANES), preferred_element_type=F32)
            o_tok[hh, g * GROUP:(g + 1) * GROUP, :] = tok[:, :LANES].astype(BF16)
            lse_tok[hh, g * GROUP:(g + 1) * GROUP, :] = tok[:, LANES:] + lo
    _attn_kernel(q_ref, k_ref, v_ref, bias_ref, o_tok, lse_tok, o_ref,
                 dilation=1, merge_in='norm', out='final')


def _attention(q, k, v, qr, kr, vr, bias, perm, dilations):
    batch, n_hp, seq, _ = q.shape
    assert dilations[0] == MAX_DIL and dilations[-1] == 1 and n_hp % SLABS_PER_STEP == 0
    ns = SLABS_PER_STEP
    res_slab = pl.BlockSpec((None, ns, MAX_DIL, seq // MAX_DIL, LANES), lambda b, h: (b, h, 0, 0, 0))
    tok_slab = pl.BlockSpec((None, ns, seq, LANES), lambda b, h: (b, h, 0, 0))
    res_shape = lambda dt: jax.ShapeDtypeStruct((batch, n_hp, MAX_DIL, seq // MAX_DIL, LANES), dt)
    bias_spec = lambda i: pl.BlockSpec((3, 2 * ns, TQ, TK), lambda b, h: (i, h, 0, 0))

    res_dils = dilations[:-1]
    merged = pl.pallas_call(
        functools.partial(_attn_residue_kernel, dilations=res_dils),
        grid=(batch, n_hp // ns),
        in_specs=[res_slab] * 3 + [bias_spec(i) for i in range(len(res_dils))],
        out_specs=[res_slab] * 3,
        out_shape=[res_shape(BF16)] * 3,
        scratch_shapes=[pltpu.VMEM((ns, MAX_DIL, seq // MAX_DIL, LANES), dt) for dt in (BF16, F32, F32)],
        compiler_params=_cparams(2),
        name="dilated_attn_residue_major",
    )(qr, kr, vr, *([bias] * len(res_dils)))

    return pl.pallas_call(
        _attn_token_kernel,
        grid=(batch, n_hp // ns),
        in_specs=[tok_slab] * 3 + [bias_spec(len(dilations) - 1), pl.BlockSpec(perm.shape, lambda b, h: (0, 0))]
        + [res_slab] * 3,
        out_specs=tok_slab,
        out_shape=jax.ShapeDtypeStruct((batch, n_hp, seq, LANES), BF16),
        scratch_shapes=[pltpu.VMEM((ns, seq, LANES), BF16), pltpu.VMEM((ns, seq, LANES), F32)],
        compiler_params=_cparams(2),
        name="dilated_attn_d1",
    )(q, k, v, bias, perm, *merged)


def _swap_16x16(mats, perm):
    out = []
    for g in range(mats[0].shape[0] // BF16_ROWS):
        rows = jnp.concatenate([m[g * BF16_ROWS:(g + 1) * BF16_ROWS] for m in mats], axis=0)
        swapped = jnp.dot(perm, rows, preferred_element_type=F32).astype(BF16)
        out.append(swapped.reshape(BF16_ROWS, MAX_DIL, rows.shape[-1]))
    return out


def _fft_kernel(w_ref, m_ref, perm_ref, b_ref, v_ref, o_ref, z_ref):
    w = w_ref[...]
    perm = perm_ref[...]
    R = v_ref.shape[2]
    grp = BF16_ROWS
    for bg in range(R // grp):
        zs = []
        for j in range(grp):
            b = bg * grp + j
            rhs = jnp.concatenate([v_ref[0, b], v_ref[1, b]], axis=0).astype(BF16)
            zs.append(jnp.dot(w, rhs, preferred_element_type=F32).astype(BF16))
        for plane in range(2):
            groups = _swap_16x16([z[plane * R:(plane + 1) * R] for z in zs], perm)
            for g, blk in enumerate(groups):
                z_ref[g * grp:(g + 1) * grp, plane, bg * grp:(bg + 1) * grp, :] = blk
    for kg in range(R // grp):
        xs = []
        for j in range(grp):
            k1 = kg * grp + j
            zc = z_ref[k1].reshape(2 * R, z_ref.shape[3])
            xs.append((jnp.dot(m_ref[k1], zc, preferred_element_type=F32) + b_ref[...]).astype(BF16))
        for g, blk in enumerate(_swap_16x16(xs, perm)):
            o_ref[g * grp:(g + 1) * grp, kg, :, :] = blk


def _fourier(pq, fourier_b, perm):
    batch, _, R, _, fw = pq.shape
    seq = R * R
    assert R == FFT_RADIX and R % BF16_ROWS == 0 and MAX_DIL == BF16_ROWS
    i = np.arange(R)
    ang1 = 2.0 * np.pi * np.outer(i, i) / R
    c1, s1 = np.cos(ang1), np.sin(ang1)
    w_cat = jnp.asarray(np.block([[c1, -s1], [-s1, -c1]]), BF16)
    k_all = i[:, None, None] + R * i[None, :, None]
    ang2 = 2.0 * np.pi * ((k_all * i[None, None, :]) % seq) / seq
    m_cat = jnp.asarray(np.concatenate([np.cos(ang2), np.sin(ang2)], axis=-1), BF16)
    grp = BF16_ROWS
    const = lambda shape: pl.BlockSpec(shape, lambda b: (0,) * len(shape), pipeline_mode=pl.Buffered(1))
    out = pl.pallas_call(
        _fft_kernel,
        grid=(batch,),
        in_specs=[const((2 * R, 2 * R)), const(m_cat.shape), const(perm.shape), const((1, fw)),
                  pl.BlockSpec((None, 2, R, R, fw), lambda b: (b, 0, 0, 0, 0))],
        out_specs=pl.BlockSpec((None, R, R // grp, grp, fw), lambda b: (b, 0, 0, 0, 0)),
        out_shape=jax.ShapeDtypeStruct((batch, R, R // grp, grp, fw), BF16),
        scratch_shapes=[pltpu.VMEM((R, 2, R, fw), BF16)],
        compiler_params=_cparams(1),
        name="fft_positions",
    )(w_cat, m_cat, perm, fourier_b.reshape(1, fw).astype(F32), pq)
    return out.reshape(batch, seq, fw)


def _mix_ffn_kernel(ap_ref, a_ref, an_ref, fp_ref, f_ref, fn_ref, xp_ref, x_ref, xn_ref,
                    ga_ref, gf_ref, wo_ref, g_ref, wg_ref, wv_ref, cw_ref, cb_ref, wd_ref, gfin_ref,
                    o_ref, act_ref, *, tiles_per_seq, fc, final_norm, n_hp):
    i = pl.program_id(0)
    tm = x_ref.shape[0]
    n_ext = tm + 2 * HALO
    ext = lambda prev, cur, nxt: jnp.concatenate([prev, cur, nxt], axis=0)

    parts = [ext(ap_ref[j], a_ref[j], an_ref[j]).astype(F32) for j in range(n_hp)]
    ssq = functools.reduce(lambda a, b: a + b,
                           [jnp.sum(p * p, axis=-1, keepdims=True) for p in parts])
    inv = lax.rsqrt(ssq / (n_hp * LANES) + EPS)
    ga = ga_ref[...]
    cols = [(p * inv * ga[:, j * LANES:(j + 1) * LANES]).astype(BF16) for j, p in enumerate(parts)]
    cols.append(_rms(ext(fp_ref[...], f_ref[...], fn_ref[...]).astype(F32), gf_ref[...]).astype(BF16))
    mixed = jnp.concatenate(cols, axis=1)
    x_ext = ext(xp_ref[...], x_ref[...], xn_ref[...]) + jnp.dot(mixed, wo_ref[...],
                                                                 preferred_element_type=F32)
    x = x_ext[HALO:HALO + tm]

    keep_prev = (i % tiles_per_seq != 0).astype(F32)
    keep_next = (i % tiles_per_seq != tiles_per_seq - 1).astype(F32)
    row = lax.broadcasted_iota(jnp.int32, (n_ext, 1), 0)
    keep = jnp.where(row < HALO, keep_prev, jnp.where(row >= HALO + tm, keep_next, 1.0))
    h_ext = (_rms(x_ext, g_ref[...]) * keep).astype(BF16)
    h = h_ext[HALO:HALO + tm]
    d_ff = wg_ref.shape[1]

    def up(c0):
        cs = slice(c0, c0 + fc)
        return (jnp.dot(h_ext, wg_ref[:, cs], preferred_element_type=F32),
                jnp.dot(h, wv_ref[:, cs], preferred_element_type=F32))

    nxt = up(0)
    for c0 in range(0, d_ff, fc):
        cs = slice(c0, c0 + fc)
        g, val = nxt
        if c0 + fc < d_ff:
            nxt = up(c0 + fc)
        g_prev = pltpu.roll(g, 1, axis=0)[HALO:HALO + tm]
        g_next = pltpu.roll(g, n_ext - 1, axis=0)[HALO:HALO + tm]
        cw = cw_ref[:, cs]
        conv = cw[0:1] * g_prev + cw[1:2] * g[HALO:HALO + tm] + cw[2:3] * g_next + cb_ref[:, cs]
        act_ref[:, cs] = (conv * (1.0 / (1.0 + jnp.exp(-conv))) * val).astype(BF16)
    y = x + jnp.dot(act_ref[...], wd_ref[...], preferred_element_type=F32)
    o_ref[...] = _rms(y, gfin_ref[...]) if final_norm else y


def _mix_ffn(attn, four, x2d, ga, gf, w_out, gain, wg, wv, conv_w, conv_b, wd, gfin, tm, fc, final_norm):
    batch, n_hp, seq, _ = attn.shape
    fw = four.shape[-1]
    n_tok, d_model = x2d.shape
    d_ff = wg.shape[1]
    nt = seq // tm
    per_tile = tm // HALO
    last = seq // HALO - 1
    prev_blk = lambda i: jnp.maximum((i % nt) * per_tile - 1, 0)
    next_blk = lambda i: jnp.minimum((i % nt + 1) * per_tile, last)
    seq_blk = lambda i: (i // nt) * (last + 1)
    const = lambda shape: pl.BlockSpec(shape, lambda i: (0,) * len(shape), pipeline_mode=pl.Buffered(1))

    def triple(cur_shape, halo_shape, index):
        return [pl.BlockSpec(halo_shape, lambda i: index(i // nt, prev_blk(i))),
                pl.BlockSpec(cur_shape, lambda i: index(i // nt, i % nt)),
                pl.BlockSpec(halo_shape, lambda i: index(i // nt, next_blk(i)))]

    in_specs = (triple((None, n_hp, tm, LANES), (None, n_hp, HALO, LANES), lambda b, t: (b, 0, t, 0))
                + triple((None, tm, fw), (None, HALO, fw), lambda b, t: (b, t, 0))
                + [pl.BlockSpec((HALO, d_model), lambda i: (seq_blk(i) + prev_blk(i), 0)),
                   pl.BlockSpec((tm, d_model), lambda i: (i, 0)),
                   pl.BlockSpec((HALO, d_model), lambda i: (seq_blk(i) + next_blk(i), 0))]
                + [const((1, n_hp * LANES)), const((1, fw)), const(w_out.shape), const((1, d_model)),
                   const(wg.shape), const(wv.shape), const(conv_w.shape), const((1, d_ff)), const(wd.shape),
                   const((1, d_model))])
    return pl.pallas_call(
        functools.partial(_mix_ffn_kernel, tiles_per_seq=nt, fc=fc, final_norm=final_norm, n_hp=n_hp),
        grid=(n_tok // tm,),
        in_specs=in_specs,
        out_specs=pl.BlockSpec((tm, d_model), lambda i: (i, 0)),
        out_shape=jax.ShapeDtypeStruct((n_tok, d_model), F32),
        scratch_shapes=[pltpu.VMEM((tm, d_ff), BF16)],
        compiler_params=_cparams(1),
        name="outproj_convglu_ffn",
    )(attn, attn, attn, four, four, four, x2d, x2d, x2d, ga, gf, w_out, gain, wg, wv, conv_w, conv_b, wd, gfin)


def kernel(x, norm_mix_gain, w_in, attn_out_gain, rel_bias_table, fourier_w, fourier_b, fourier_out_gain, w_out, norm_ffn_gain, w_gate, w_val, conv_w, conv_b, w_down, final_norm_gain):
    batch, seq, d_model = x.shape
    depth = w_in.shape[0]
    n_heads = rel_bias_table.shape[1]
    attn_w = n_heads * HEAD_DIM
    n_hp = attn_w // LANES
    fw = fourier_w.shape[1] * fourier_w.shape[2]
    assert all(w // (2 * d) == HALF_WINDOW for w, d in DILATED_PATTERNS)
    dilations = sorted((d for _, d in DILATED_PATTERNS), reverse=True)
    tm = 512
    row = lambda g: g.reshape(1, -1).astype(F32)
    perm = _group_permutation()
    bias = _bias_tiles(rel_bias_table, dilations)

    x2d = x.reshape(batch * seq, d_model)
    for layer in range(depth):
        w_l = w_in[layer]
        ab = _fold_fourier_weights(fourier_w[layer], seq).astype(BF16)
        w_all = jnp.concatenate([w_l[:, :attn_w] * (HEAD_DIM ** -0.5 * LOG2E), w_l[:, attn_w:]],
                                axis=1).astype(BF16)
        q, k, v, qr, kr, vr, pq = _inproj(x2d, row(norm_mix_gain[layer]), w_all, perm, ab, batch, seq, n_hp,
                                          fw, tm)
        attn = _attention(q, k, v, qr, kr, vr, bias, perm, dilations)
        four = _fourier(pq, fourier_b[layer].reshape(-1), perm)
        x2d = _mix_ffn(attn, four, x2d, row(attn_out_gain[layer]), row(fourier_out_gain[layer]),
                       w_out[layer].astype(BF16), row(norm_ffn_gain[layer]), w_gate[layer].astype(BF16),
                       w_val[layer].astype(BF16), conv_w[layer].astype(F32), row(conv_b[layer]),
                       w_down[layer].astype(BF16), row(final_norm_gain), 2 * tm, 256, layer == depth - 1)
    return x2d.reshape(batch, seq, d_model)
```

```python
import functools
import math

import numpy as np
import jax
import jax.numpy as jnp
from jax import lax
from jax.experimental import pallas as pl
from jax.experimental.pallas import tpu as pltpu

EPS = 1e-6
NEG_INF = -1e30
LOG2E = math.log2(math.e)
HEAD_DIM = 64
DILATED_PATTERNS = ((128, 1), (512, 4), (2048, 16))
N_REL_BUCKETS = 32
REL_MAX_DISTANCE = 1024

LANES = 128
BF16_ROWS = 16
HALF_WINDOW = 64
TQ = 2 * HALF_WINDOW
TK = 4 * HALF_WINDOW
MAX_DIL = max(d for _, d in DILATED_PATTERNS)
GROUP = MAX_DIL * BF16_ROWS
SLABS_PER_STEP = 2
FFT_RADIX = 64
FPERM_ROWS = 512
HALO = BF16_ROWS
VMEM_LIMIT = 56 * 1024 * 1024

BF16 = jnp.bfloat16
F32 = jnp.float32


def _cparams(n_axes):
    return pltpu.CompilerParams(dimension_semantics=("arbitrary",) * n_axes,
                                vmem_limit_bytes=VMEM_LIMIT)


def _rms(x, gain):
    ms = jnp.mean(x * x, axis=-1, keepdims=True)
    return x * lax.rsqrt(ms + EPS) * gain


def _group_permutation():
    t = np.arange(GROUP)
    swapped = (t % MAX_DIL) * BF16_ROWS + t // MAX_DIL
    perm = np.zeros((GROUP, GROUP), np.float32)
    perm[swapped, t] = 1.0
    assert MAX_DIL == BF16_ROWS and np.array_equal(perm, perm.T)
    return jnp.asarray(perm, BF16)


def _prep_kernel(fw_ref, cc_ref, sc_ref, ab_ref, *, groups):
    hi = lax.Precision.HIGHEST
    g = pl.program_id(0)
    fw = fw_ref[...]
    a = jnp.dot(cc_ref[...], fw, precision=hi, preferred_element_type=F32)
    b = jnp.dot(sc_ref[...], fw, precision=hi, preferred_element_type=F32)
    gd = fw.shape[0]
    wide = jnp.concatenate([a] * groups + [b] * groups, axis=1)
    col_group = (lax.broadcasted_iota(jnp.int32, wide.shape, 1) // gd) % groups
    ab_ref[...] = jnp.where(col_group == g, wide, 0.0)


def _fold_fourier_weights(fourier_w, seq):
    groups, gd, _ = fourier_w.shape
    ang = 2.0 * np.pi * np.outer(np.arange(gd), np.arange(gd)) / gd
    scale = 1.0 / math.sqrt(seq * gd)
    cc = jnp.asarray(np.cos(ang) * scale, F32)
    sc = jnp.asarray(np.sin(ang) * scale, F32)
    return pl.pallas_call(
        functools.partial(_prep_kernel, groups=groups),
        grid=(groups,),
        in_specs=[pl.BlockSpec((None, gd, gd), lambda g: (g, 0, 0)),
                  pl.BlockSpec((gd, gd), lambda g: (0, 0)),
                  pl.BlockSpec((gd, gd), lambda g: (0, 0))],
        out_specs=pl.BlockSpec((gd, 2 * groups * gd), lambda g: (g, 0)),
        out_shape=jax.ShapeDtypeStruct((groups * gd, 2 * groups * gd), F32),
        compiler_params=_cparams(1),
        name="fourier_weight_fold",
    )(fourier_w, cc, sc)


def _inproj_kernel(x_ref, g_ref, w_ref, perm_ref, fperm_ref, ab_ref, q_ref, k_ref, v_ref, qr_ref, kr_ref,
                   vr_ref, pq_ref, u_ref, *, n_hp, fw):
    h = _rms(x_ref[...], g_ref[...]).astype(BF16)
    tm = h.shape[0]
    perm = perm_ref[...]

    def proj(c0, n):
        return jnp.dot(h, w_ref[:, c0:c0 + n], preferred_element_type=F32)

    aw = n_hp * LANES

    def project(t, tok_ref):
        for c in range(n_hp // 2):
            res = proj(t * aw + c * 2 * LANES, 2 * LANES).astype(BF16)
            tok_ref[2 * c] = res[:, :LANES]
            tok_ref[2 * c + 1] = res[:, LANES:]

    def permute(tok_ref, res_ref):
        for c in range(n_hp // 2):
            for g in range(tm // GROUP):
                tok = jnp.concatenate([tok_ref[2 * c, g * GROUP:(g + 1) * GROUP, :],
                                       tok_ref[2 * c + 1, g * GROUP:(g + 1) * GROUP, :]], axis=1)
                rows = jnp.dot(perm, tok, preferred_element_type=F32).astype(BF16)
                for half in range(2):
                    blk = rows[:, half * LANES:(half + 1) * LANES].reshape(MAX_DIL, BF16_ROWS, LANES)
                    res_ref[2 * c + half, :, g * BF16_ROWS:(g + 1) * BF16_ROWS, :] = blk

    u_ref[...] = proj(3 * aw, fw).astype(BF16)
    project(0, q_ref)
    project(1, k_ref)
    permute(q_ref, qr_ref)
    fp = fperm_ref.shape[0]
    a_per = fp // FFT_RADIX
    u_rows = [jnp.dot(fperm_ref[...], u_ref[s * fp:(s + 1) * fp, :], preferred_element_type=F32).astype(BF16)
              for s in range(tm // fp)]
    project(2, v_ref)
    permute(k_ref, kr_ref)
    for s, rows in enumerate(u_rows):
        planes = jnp.dot(rows, ab_ref[...], preferred_element_type=F32).reshape(FFT_RADIX, a_per, 2 * fw)
        for t in range(2):
            pq_ref[t, :, s * a_per:(s + 1) * a_per, :] = planes[:, :, t * fw:(t + 1) * fw]
    permute(v_ref, vr_ref)


def _inproj(x2d, gain, w_all, perm, ab, batch, seq, n_hp, fw, tm):
    n_tok, d_model = x2d.shape
    nt = seq // tm
    tok_map = lambda i: (i // nt, 0, i % nt, 0)
    res_map = lambda i: (i // nt, 0, 0, i % nt, 0)
    tok_shape = jax.ShapeDtypeStruct((batch, n_hp, seq, LANES), BF16)
    res_shape = jax.ShapeDtypeStruct((batch, n_hp, MAX_DIL, seq // MAX_DIL, LANES), BF16)
    R = FFT_RADIX
    fp = min(tm, FPERM_ROWS)
    t = np.arange(fp)
    fperm = np.zeros((fp, fp), np.float32)
    fperm[(t % R) * (fp // R) + t // R, t] = 1.0
    fperm = jnp.asarray(fperm, BF16)
    const = lambda shape: pl.BlockSpec(shape, lambda i: (0,) * len(shape), pipeline_mode=pl.Buffered(1))
    return pl.pallas_call(
        functools.partial(_inproj_kernel, n_hp=n_hp, fw=fw),
        grid=(n_tok // tm,),
        in_specs=[pl.BlockSpec((tm, d_model), lambda i: (i, 0)),
                  const((1, d_model)), const(w_all.shape), const(perm.shape), const(fperm.shape),
                  const(ab.shape)],
        out_specs=[pl.BlockSpec((None, n_hp, tm, LANES), tok_map)] * 3
        + [pl.BlockSpec((None, n_hp, MAX_DIL, tm // MAX_DIL, LANES), res_map)] * 3
        + [pl.BlockSpec((None, 2, R, tm // R, fw), res_map)],
        out_shape=[tok_shape] * 3 + [res_shape] * 3
        + [jax.ShapeDtypeStruct((batch, 2, R, seq // R, fw), F32)],
        scratch_shapes=[pltpu.VMEM((tm, fw), BF16)],
        compiler_params=_cparams(1),
        name="rmsnorm_inproj",
    )(x2d, gain, w_all, perm, fperm, ab)


def _t5_bucket_static(rel, dtype):
    nb = N_REL_BUCKETS // 2
    max_exact = nb // 2
    n = np.abs(rel)
    nf = np.maximum(n, 1).astype(dtype)
    large = max_exact + (np.log(nf / dtype(max_exact)) / dtype(math.log(REL_MAX_DISTANCE / max_exact))
                         * dtype(nb - max_exact)).astype(np.int32)
    large = np.minimum(large, nb - 1)
    return np.where(rel > 0, nb, 0) + np.where(n < max_exact, n, large)


def _bucket_tiles(dilations):
    qi = np.arange(TQ)[:, None]
    kc = np.arange(TK)[None, :]
    offsets = np.array([0, HALF_WINDOW, 2 * HALF_WINDOW])
    rel = kc[None] - offsets[:, None, None] - qi[None]
    tiles = []
    for d in dilations:
        bkt = _t5_bucket_static(rel * d, np.float32)
        assert np.array_equal(bkt, _t5_bucket_static(rel * d, np.float64))
        tile = np.where(np.abs(rel) <= HALF_WINDOW, bkt, -1)
        pieces = MAX_DIL // d if 1 < d < MAX_DIL else 1
        q_order = (np.arange(TQ) % (TQ // pieces)) * pieces + np.arange(TQ) // (TQ // pieces)
        k_order = (np.arange(TK) % (TK // pieces)) * pieces + np.arange(TK) // (TK // pieces)
        tiles.append(tile[:, q_order][:, :, k_order])
    return np.concatenate(tiles, axis=0).astype(np.int32)


def _bias_kernel(table_ref, bkt_ref, out_ref, *, n_heads):
    bkt = bkt_ref[...]
    for h in range(n_heads):
        acc = jnp.full(bkt.shape, NEG_INF, F32)
        for b in range(N_REL_BUCKETS):
            acc = jnp.where(bkt == b, table_ref[b, h], acc)
        out_ref[h] = acc * LOG2E


def _bias_tiles(rel_table, dilations):
    n_heads = rel_table.shape[1]
    bkt = jnp.asarray(_bucket_tiles(dilations))
    return pl.pallas_call(
        functools.partial(_bias_kernel, n_heads=n_heads),
        grid=(bkt.shape[0],),
        in_specs=[pl.BlockSpec(memory_space=pltpu.SMEM),
                  pl.BlockSpec((None, TQ, TK), lambda t: (t, 0, 0))],
        out_specs=pl.BlockSpec((None, n_heads, TQ, TK), lambda t: (t, 0, 0, 0)),
        out_shape=jax.ShapeDtypeStruct((bkt.shape[0], n_heads, TQ, TK), F32),
        compiler_params=_cparams(1),
        name="rel_bias_tiles",
    )(rel_table.astype(F32), bkt)


def _block_attention(qb, kb, vb, bias2, first_head):
    nt_dims = (((1,), (1,)), ((), ()))
    zero = jnp.zeros_like(qb)
    q2 = jnp.concatenate([jnp.where(first_head, qb, zero), jnp.where(first_head, zero, qb)], axis=0)
    s = lax.dot_general(q2, kb, nt_dims, preferred_element_type=F32) + bias2
    m = jnp.max(s, axis=-1, keepdims=True)
    p = jnp.exp2(s - m)
    l = jnp.sum(p, axis=-1, keepdims=True)
    pv = jnp.dot(p.astype(BF16), vb, preferred_element_type=F32)
    num = jnp.where(first_head, pv[:TQ], pv[TQ:])
    den = jnp.where(first_head, l[:TQ], l[TQ:])
    top = jnp.where(first_head, m[:TQ], m[TQ:])
    return num, top, den


def _merge(a, b):
    num_a, top_a, den_a = a
    num_b, top_b, den_b = b
    top = jnp.maximum(top_a, top_b)
    w_a = jnp.exp2(top_a - top)
    w_b = jnp.exp2(top_b - top)
    den = (w_a if den_a is None else w_a * den_a) + (w_b if den_b is None else w_b * den_b)
    return w_a * num_a + w_b * num_b, top, den


def _split_f32(x):
    hi = x.astype(BF16)
    return hi, (x - hi.astype(F32)).astype(BF16)


def _attn_kernel(*refs, dilation, merge_in, out):
    q_ref, k_ref, v_ref, bias_ref = refs[:4]
    n_prev = {None: 0, 'raw': 3, 'norm': 2}[merge_in]
    prev_refs = refs[4:4 + n_prev]
    out_refs = refs[4 + n_prev:]
    n_slabs = q_ref.shape[0]
    pieces = MAX_DIL // dilation if dilation > 1 else 1
    sub_len = q_ref.shape[-2] * pieces
    n_blk = sub_len // TQ
    tq_p, tk_p = TQ // pieces, TK // pieces
    first_head = lax.broadcasted_iota(jnp.int32, (TQ, LANES), 1) < HEAD_DIM

    def load(ref, hh, r, row, n_rows):
        if dilation == 1:
            return ref[hh, row:row + n_rows, :]
        parts = [ref[hh, a * dilation + r, row:row + n_rows, :] for a in range(pieces)]
        return parts[0] if pieces == 1 else jnp.concatenate(parts, axis=0)

    def store(ref, hh, r, row, n_rows, val):
        if dilation == 1:
            ref[hh, row:row + n_rows, :] = val
        else:
            for a in range(pieces):
                ref[hh, a * dilation + r, row:row + n_rows, :] = val[a * n_rows:(a + 1) * n_rows]

    for n in range(n_blk):
        q_row = n * tq_p
        k_row = min(max(q_row - tk_p // 4, 0), sub_len // pieces - tk_p)
        edge = (1 if n > 0 else 0) + (1 if n == n_blk - 1 else 0)
        for r in range(dilation):
            for hh in range(n_slabs):
                qb = load(q_ref, hh, r, q_row, tq_p)
                kb = load(k_ref, hh, r, k_row, tk_p)
                vb = load(v_ref, hh, r, k_row, tk_p)
                bias2 = bias_ref[edge, 2 * hh:2 * hh + 2].reshape(2 * TQ, TK)
                part = _block_attention(qb, kb, vb, bias2, first_head)
                if merge_in:
                    prev = [load(ref, hh, r, q_row, tq_p).astype(F32) for ref in prev_refs]
                    part = _merge(part, prev if merge_in == 'raw' else prev + [None])
                num, top, den = part
                if out == 'raw':
                    vals = (num.astype(BF16), top, den)
                else:
                    o = (num * (1.0 / den)).astype(BF16)
                    vals = (o,) + _split_f32(top + jnp.log2(den)) if out == 'split' else (o,)
                for ref, val in zip(out_refs, vals):
                    store(ref, hh, r, q_row, tq_p, val)


def _attn_residue_kernel(*refs, dilations):
    n = len(dilations)
    q_ref, k_ref, v_ref = refs[:3]
    bias_refs = refs[3:3 + n]
    out_refs = refs[3 + n:6 + n]
    part_refs = refs[6 + n:]
    for i, d in enumerate(dilations):
        last = i == n - 1
        _attn_kernel(q_ref, k_ref, v_ref, bias_refs[i], *(part_refs if i else ()),
                     *(out_refs if last else part_refs),
                     dilation=d, merge_in='raw' if i else None, out='split' if last else 'raw')


def _attn_token_kernel(q_ref, k_ref, v_ref, bias_ref, perm_ref, po_ref, hi_ref, lo_ref, o_ref,
                       o_tok, lse_tok):
    perm = perm_ref[...]
    for hh in range(po_ref.shape[0]):
        for g in range(po_ref.shape[2] // BF16_ROWS):
            rows = slice(g * BF16_ROWS, (g + 1) * BF16_ROWS)
            o_hi = jnp.concatenate([po_ref[hh, :, rows, :].reshape(GROUP, LANES),
                                    hi_ref[hh, :, rows, :].reshape(GROUP, LANES)], axis=1)
            tok = jnp.dot(perm, o_hi, preferred_element_type=F32)
            lo = jnp.dot(perm, lo_ref[hh, :, rows, :].reshape(GROUP, LANES), preferred_element_type=F32)
            o_tok[hh, g * GROUP:(g + 1) * GROUP, :] = tok[:, :LANES].astype(BF16)
            lse_tok[hh, g * GROUP:(g + 1) * GROUP, :] = tok[:, LANES:] + lo
    _attn_kernel(q_ref, k_ref, v_ref, bias_ref, o_tok, lse_tok, o_ref,
                 dilation=1, merge_in='norm', out='final')


def _attention(q, k, v, qr, kr, vr, bias, perm, dilations):
    batch, n_hp, seq, _ = q.shape
    assert dilations[0] == MAX_DIL and dilations[-1] == 1 and n_hp % SLABS_PER_STEP == 0
    ns = SLABS_PER_STEP
    res_slab = pl.BlockSpec((None, ns, MAX_DIL, seq // MAX_DIL, LANES), lambda b, h: (b, h, 0, 0, 0))
    tok_slab = pl.BlockSpec((None, ns, seq, LANES), lambda b, h: (b, h, 0, 0))
    res_shape = lambda dt: jax.ShapeDtypeStruct((batch, n_hp, MAX_DIL, seq // MAX_DIL, LANES), dt)
    bias_spec = lambda i: pl.BlockSpec((3, 2 * ns, TQ, TK), lambda b, h: (i, h, 0, 0))

    res_dils = dilations[:-1]
    merged = pl.pallas_call(
        functools.partial(_attn_residue_kernel, dilations=res_dils),
        grid=(batch, n_hp // ns),
        in_specs=[res_slab] * 3 + [bias_spec(i) for i in range(len(res_dils))],
        out_specs=[res_slab] * 3,
        out_shape=[res_shape(BF16)] * 3,
        scratch_shapes=[pltpu.VMEM((ns, MAX_DIL, seq // MAX_DIL, LANES), dt) for dt in (BF16, F32, F32)],
        compiler_params=_cparams(2),
        name="dilated_attn_residue_major",
    )(qr, kr, vr, *([bias] * len(res_dils)))

    return pl.pallas_call(
        _attn_token_kernel,
        grid=(batch, n_hp // ns),
        in_specs=[tok_slab] * 3 + [bias_spec(len(dilations) - 1), pl.BlockSpec(perm.shape, lambda b, h: (0, 0))]
        + [res_slab] * 3,
        out_specs=tok_slab,
        out_shape=jax.ShapeDtypeStruct((batch, n_hp, seq, LANES), BF16),
        scratch_shapes=[pltpu.VMEM((ns, seq, LANES), BF16), pltpu.VMEM((ns, seq, LANES), F32)],
        compiler_params=_cparams(2),
        name="dilated_attn_d1",
    )(q, k, v, bias, perm, *merged)


def _swap_16x16(mats, perm):
    out = []
    for g in range(mats[0].shape[0] // BF16_ROWS):
        rows = jnp.concatenate([m[g * BF16_ROWS:(g + 1) * BF16_ROWS] for m in mats], axis=0)
        swapped = jnp.dot(perm, rows, preferred_element_type=F32).astype(BF16)
        out.append(swapped.reshape(BF16_ROWS, MAX_DIL, rows.shape[-1]))
    return out


def _fft_kernel(w_ref, m_ref, perm_ref, b_ref, v_ref, o_ref, z_ref):
    w = w_ref[...]
    perm = perm_ref[...]
    R = v_ref.shape[2]
    grp = BF16_ROWS
    for bg in range(R // grp):
        zs = []
        for j in range(grp):
            b = bg * grp + j
            rhs = jnp.concatenate([v_ref[0, b], v_ref[1, b]], axis=0).astype(BF16)
            zs.append(jnp.dot(w, rhs, preferred_element_type=F32).astype(BF16))
        for plane in range(2):
            groups = _swap_16x16([z[plane * R:(plane + 1) * R] for z in zs], perm)
            for g, blk in enumerate(groups):
                z_ref[g * grp:(g + 1) * grp, plane, bg * grp:(bg + 1) * grp, :] = blk
    for kg in range(R // grp):
        xs = []
        for j in range(grp):
            k1 = kg * grp + j
            zc = z_ref[k1].reshape(2 * R, z_ref.shape[3])
            xs.append((jnp.dot(m_ref[k1], zc, preferred_element_type=F32) + b_ref[...]).astype(BF16))
        for g, blk in enumerate(_swap_16x16(xs, perm)):
            o_ref[g * grp:(g + 1) * grp, kg, :, :] = blk


def _fourier(pq, fourier_b, perm):
    batch, _, R, _, fw = pq.shape
    seq = R * R
    assert R == FFT_RADIX and R % BF16_ROWS == 0 and MAX_DIL == BF16_ROWS
    i = np.arange(R)
    ang1 = 2.0 * np.pi * np.outer(i, i) / R
    c1, s1 = np.cos(ang1), np.sin(ang1)
    w_cat = jnp.asarray(np.block([[c1, -s1], [-s1, -c1]]), BF16)
    k_all = i[:, None, None] + R * i[None, :, None]
    ang2 = 2.0 * np.pi * ((k_all * i[None, None, :]) % seq) / seq
    m_cat = jnp.asarray(np.concatenate([np.cos(ang2), np.sin(ang2)], axis=-1), BF16)
    grp = BF16_ROWS
    const = lambda shape: pl.BlockSpec(shape, lambda b: (0,) * len(shape), pipeline_mode=pl.Buffered(1))
    out = pl.pallas_call(
        _fft_kernel,
        grid=(batch,),
        in_specs=[const((2 * R, 2 * R)), const(m_cat.shape), const(perm.shape), const((1, fw)),
                  pl.BlockSpec((None, 2, R, R, fw), lambda b: (b, 0, 0, 0, 0))],
        out_specs=pl.BlockSpec((None, R, R // grp, grp, fw), lambda b: (b, 0, 0, 0, 0)),
        out_shape=jax.ShapeDtypeStruct((batch, R, R // grp, grp, fw), BF16),
        scratch_shapes=[pltpu.VMEM((R, 2, R, fw), BF16)],
        compiler_params=_cparams(1),
        name="fft_positions",
    )(w_cat, m_cat, perm, fourier_b.reshape(1, fw).astype(F32), pq)
    return out.reshape(batch, seq, fw)


def _mix_ffn_kernel(ap_ref, a_ref, an_ref, fp_ref, f_ref, fn_ref, xp_ref, x_ref, xn_ref,
                    ga_ref, gf_ref, wo_ref, g_ref, wg_ref, wv_ref, cw_ref, cb_ref, wd_ref, gfin_ref,
                    o_ref, act_ref, *, tiles_per_seq, fc, final_norm, n_hp):
    i = pl.program_id(0)
    tm = x_ref.shape[0]
    n_ext = tm + 2 * HALO
    ext = lambda prev, cur, nxt: jnp.concatenate([prev, cur, nxt], axis=0)

    parts = [ext(ap_ref[j], a_ref[j], an_ref[j]).astype(F32) for j in range(n_hp)]
    ssq = functools.reduce(lambda a, b: a + b,
                           [jnp.sum(p * p, axis=-1, keepdims=True) for p in parts])
    inv = lax.rsqrt(ssq / (n_hp * LANES) + EPS)
    ga = ga_ref[...]
    cols = [(p * inv * ga[:, j * LANES:(j + 1) * LANES]).astype(BF16) for j, p in enumerate(parts)]
    cols.append(_rms(ext(fp_ref[...], f_ref[...], fn_ref[...]).astype(F32), gf_ref[...]).astype(BF16))
    mixed = jnp.concatenate(cols, axis=1)
    x_ext = ext(xp_ref[...], x_ref[...], xn_ref[...]) + jnp.dot(mixed, wo_ref[...],
                                                                 preferred_element_type=F32)
    x = x_ext[HALO:HALO + tm]

    keep_prev = (i % tiles_per_seq != 0).astype(F32)
    keep_next = (i % tiles_per_seq != tiles_per_seq - 1).astype(F32)
    row = lax.broadcasted_iota(jnp.int32, (n_ext, 1), 0)
    keep = jnp.where(row < HALO, keep_prev, jnp.where(row >= HALO + tm, keep_next, 1.0))
    h_ext = (_rms(x_ext, g_ref[...]) * keep).astype(BF16)
    h = h_ext[HALO:HALO + tm]
    d_ff = wg_ref.shape[1]

    def up(c0):
        cs = slice(c0, c0 + fc)
        return (jnp.dot(h_ext, wg_ref[:, cs], preferred_element_type=F32),
                jnp.dot(h, wv_ref[:, cs], preferred_element_type=F32))

    nxt = up(0)
    for c0 in range(0, d_ff, fc):
        cs = slice(c0, c0 + fc)
        g, val = nxt
        if c0 + fc < d_ff:
            nxt = up(c0 + fc)
        g_prev = pltpu.roll(g, 1, axis=0)[HALO:HALO + tm]
        g_next = pltpu.roll(g, n_ext - 1, axis=0)[HALO:HALO + tm]
        cw = cw_ref[:, cs]
        conv = cw[0:1] * g_prev + cw[1:2] * g[HALO:HALO + tm] + cw[2:3] * g_next + cb_ref[:, cs]
        act_ref[:, cs] = (conv * (1.0 / (1.0 + jnp.exp(-conv))) * val).astype(BF16)
    y = x + jnp.dot(act_ref[...], wd_ref[...], preferred_element_type=F32)
    o_ref[...] = _rms(y, gfin_ref[...]) if final_norm else y


def _mix_ffn(attn, four, x2d, ga, gf, w_out, gain, wg, wv, conv_w, conv_b, wd, gfin, tm, fc, final_norm):
    batch, n_hp, seq, _ = attn.shape
    fw = four.shape[-1]
    n_tok, d_model = x2d.shape
    d_ff = wg.shape[1]
    nt = seq // tm
    per_tile = tm // HALO
    last = seq // HALO - 1
    prev_blk = lambda i: jnp.maximum((i % nt) * per_tile - 1, 0)
    next_blk = lambda i: jnp.minimum((i % nt + 1) * per_tile, last)
    seq_blk = lambda i: (i // nt) * (last + 1)
    const = lambda shape: pl.BlockSpec(shape, lambda i: (0,) * len(shape), pipeline_mode=pl.Buffered(1))

    def triple(cur_shape, halo_shape, index):
        return [pl.BlockSpec(halo_shape, lambda i: index(i // nt, prev_blk(i))),
                pl.BlockSpec(cur_shape, lambda i: index(i // nt, i % nt)),
                pl.BlockSpec(halo_shape, lambda i: index(i // nt, next_blk(i)))]

    in_specs = (triple((None, n_hp, tm, LANES), (None, n_hp, HALO, LANES), lambda b, t: (b, 0, t, 0))
                + triple((None, tm, fw), (None, HALO, fw), lambda b, t: (b, t, 0))
                + [pl.BlockSpec((HALO, d_model), lambda i: (seq_blk(i) + prev_blk(i), 0)),
                   pl.BlockSpec((tm, d_model), lambda i: (i, 0)),
                   pl.BlockSpec((HALO, d_model), lambda i: (seq_blk(i) + next_blk(i), 0))]
                + [const((1, n_hp * LANES)), const((1, fw)), const(w_out.shape), const((1, d_model)),
                   const(wg.shape), const(wv.shape), const(conv_w.shape), const((1, d_ff)), const(wd.shape),
                   const((1, d_model))])
    return pl.pallas_call(
        functools.partial(_mix_ffn_kernel, tiles_per_seq=nt, fc=fc, final_norm=final_norm, n_hp=n_hp),
        grid=(n_tok // tm,),
        in_specs=in_specs,
        out_specs=pl.BlockSpec((tm, d_model), lambda i: (i, 0)),
        out_shape=jax.ShapeDtypeStruct((n_tok, d_model), F32),
        scratch_shapes=[pltpu.VMEM((tm, d_ff), BF16)],
        compiler_params=_cparams(1),
        name="outproj_convglu_ffn",
    )(attn, attn, attn, four, four, four, x2d, x2d, x2d, ga, gf, w_out, gain, wg, wv, conv_w, conv_b, wd, gfin)


def kernel(x, norm_mix_gain, w_in, attn_out_gain, rel_bias_table, fourier_w, fourier_b, fourier_out_gain, w_out, norm_ffn_gain, w_gate, w_val, conv_w, conv_b, w_down, final_norm_gain):
    batch, seq, d_model = x.shape
    depth = w_in.shape[0]
    n_heads = rel_bias_table.shape[1]
    attn_w = n_heads * HEAD_DIM
    n_hp = attn_w // LANES
    fw = fourier_w.shape[1] * fourier_w.shape[2]
    assert all(w // (2 * d) == HALF_WINDOW for w, d in DILATED_PATTERNS)
    dilations = sorted((d for _, d in DILATED_PATTERNS), reverse=True)
    tm = 1024
    row = lambda g: g.reshape(1, -1).astype(F32)
    perm = _group_permutation()
    bias = _bias_tiles(rel_bias_table, dilations)

    x2d = x.reshape(batch * seq, d_model)
    for layer in range(depth):
        w_l = w_in[layer]
        ab = _fold_fourier_weights(fourier_w[layer], seq).astype(BF16)
        w_all = jnp.concatenate([w_l[:, :attn_w] * (HEAD_DIM ** -0.5 * LOG2E), w_l[:, attn_w:]],
                                axis=1).astype(BF16)
        q, k, v, qr, kr, vr, pq = _inproj(x2d, row(norm_mix_gain[layer]), w_all, perm, ab, batch, seq, n_hp,
                                          fw, tm)
        attn = _attention(q, k, v, qr, kr, vr, bias, perm, dilations)
        four = _fourier(pq, fourier_b[layer].reshape(-1), perm)
        x2d = _mix_ffn(attn, four, x2d, row(attn_out_gain[layer]), row(fourier_out_gain[layer]),
                       w_out[layer].astype(BF16), row(norm_ffn_gain[layer]), w_gate[layer].astype(BF16),
                       w_val[layer].astype(BF16), conv_w[layer].astype(F32), row(conv_b[layer]),
                       w_down[layer].astype(BF16), row(final_norm_gain), tm, 256, layer == depth - 1)
    return x2d.reshape(batch, seq, d_model)
```

```python
import functools
import math

import numpy as np
import jax
import jax.numpy as jnp
from jax import lax
from jax.experimental import pallas as pl
from jax.experimental.pallas import tpu as pltpu

EPS = 1e-6
NEG_INF = -1e30
LOG2E = math.log2(math.e)
HEAD_DIM = 64
DILATED_PATTERNS = ((128, 1), (512, 4), (2048, 16))
N_REL_BUCKETS = 32
REL_MAX_DISTANCE = 1024

LANES = 128
BF16_ROWS = 16
HALF_WINDOW = 64
TQ = 2 * HALF_WINDOW
TK = 4 * HALF_WINDOW
MAX_DIL = max(d for _, d in DILATED_PATTERNS)
GROUP = MAX_DIL * BF16_ROWS
SLABS_PER_STEP = 2
FFT_RADIX = 64
FPERM_ROWS = 512
HALO = BF16_ROWS
VMEM_LIMIT = 56 * 1024 * 1024

BF16 = jnp.bfloat16
F32 = jnp.float32


def _cparams(n_axes):
    return pltpu.CompilerParams(dimension_semantics=("arbitrary",) * n_axes,
                                vmem_limit_bytes=VMEM_LIMIT)


def _rms(x, gain):
    ms = jnp.mean(x * x, axis=-1, keepdims=True)
    return x * lax.rsqrt(ms + EPS) * gain


def _group_permutation():
    t = np.arange(GROUP)
    swapped = (t % MAX_DIL) * BF16_ROWS + t // MAX_DIL
    perm = np.zeros((GROUP, GROUP), np.float32)
    perm[swapped, t] = 1.0
    assert MAX_DIL == BF16_ROWS and np.array_equal(perm, perm.T)
    return jnp.asarray(perm, BF16)


def _prep_kernel(fw_ref, cc_ref, sc_ref, ab_ref, *, groups):
    hi = lax.Precision.HIGHEST
    g = pl.program_id(0)
    fw = fw_ref[...]
    a = jnp.dot(cc_ref[...], fw, precision=hi, preferred_element_type=F32)
    b = jnp.dot(sc_ref[...], fw, precision=hi, preferred_element_type=F32)
    gd = fw.shape[0]
    wide = jnp.concatenate([a] * groups + [b] * groups, axis=1)
    col_group = (lax.broadcasted_iota(jnp.int32, wide.shape, 1) // gd) % groups
    ab_ref[...] = jnp.where(col_group == g, wide, 0.0)


def _fold_fourier_weights(fourier_w, seq):
    groups, gd, _ = fourier_w.shape
    ang = 2.0 * np.pi * np.outer(np.arange(gd), np.arange(gd)) / gd
    scale = 1.0 / math.sqrt(seq * gd)
    cc = jnp.asarray(np.cos(ang) * scale, F32)
    sc = jnp.asarray(np.sin(ang) * scale, F32)
    return pl.pallas_call(
        functools.partial(_prep_kernel, groups=groups),
        grid=(groups,),
        in_specs=[pl.BlockSpec((None, gd, gd), lambda g: (g, 0, 0)),
                  pl.BlockSpec((gd, gd), lambda g: (0, 0)),
                  pl.BlockSpec((gd, gd), lambda g: (0, 0))],
        out_specs=pl.BlockSpec((gd, 2 * groups * gd), lambda g: (g, 0)),
        out_shape=jax.ShapeDtypeStruct((groups * gd, 2 * groups * gd), F32),
        compiler_params=_cparams(1),
        name="fourier_weight_fold",
    )(fourier_w, cc, sc)


def _inproj_kernel(x_ref, g_ref, w_ref, perm_ref, fperm_ref, ab_ref, q_ref, k_ref, v_ref, qr_ref, kr_ref,
                   vr_ref, pq_ref, u_ref, *, n_hp, fw):
    h = _rms(x_ref[...], g_ref[...]).astype(BF16)
    tm = h.shape[0]
    perm = perm_ref[...]

    def proj(c0, n):
        return jnp.dot(h, w_ref[:, c0:c0 + n], preferred_element_type=F32)

    aw = n_hp * LANES

    def project(t, tok_ref):
        for c in range(n_hp // 2):
            res = proj(t * aw + c * 2 * LANES, 2 * LANES).astype(BF16)
            tok_ref[2 * c] = res[:, :LANES]
            tok_ref[2 * c + 1] = res[:, LANES:]

    def permute(tok_ref, res_ref):
        for c in range(n_hp // 2):
            for g in range(tm // GROUP):
                tok = jnp.concatenate([tok_ref[2 * c, g * GROUP:(g + 1) * GROUP, :],
                                       tok_ref[2 * c + 1, g * GROUP:(g + 1) * GROUP, :]], axis=1)
                rows = jnp.dot(perm, tok, preferred_element_type=F32).astype(BF16)
                for half in range(2):
                    blk = rows[:, half * LANES:(half + 1) * LANES].reshape(MAX_DIL, BF16_ROWS, LANES)
                    res_ref[2 * c + half, :, g * BF16_ROWS:(g + 1) * BF16_ROWS, :] = blk

    u_ref[...] = proj(3 * aw, fw).astype(BF16)
    project(0, q_ref)
    project(1, k_ref)
    permute(q_ref, qr_ref)
    fp = fperm_ref.shape[0]
    a_per = fp // FFT_RADIX
    u_rows = [jnp.dot(fperm_ref[...], u_ref[s * fp:(s + 1) * fp, :], preferred_element_type=F32).astype(BF16)
              for s in range(tm // fp)]
    project(2, v_ref)
    permute(k_ref, kr_ref)
    planes = jnp.concatenate(
        [jnp.dot(rows, ab_ref[...], preferred_element_type=F32).reshape(FFT_RADIX, a_per, 2 * fw)
         for rows in u_rows], axis=1).astype(BF16)
    for t in range(2):
        pq_ref[t] = planes[:, :, t * fw:(t + 1) * fw]
    permute(v_ref, vr_ref)


def _inproj(x2d, gain, w_all, perm, ab, batch, seq, n_hp, fw, tm):
    n_tok, d_model = x2d.shape
    nt = seq // tm
    tok_map = lambda i: (i // nt, 0, i % nt, 0)
    res_map = lambda i: (i // nt, 0, 0, i % nt, 0)
    tok_shape = jax.ShapeDtypeStruct((batch, n_hp, seq, LANES), BF16)
    res_shape = jax.ShapeDtypeStruct((batch, n_hp, MAX_DIL, seq // MAX_DIL, LANES), BF16)
    R = FFT_RADIX
    fp = min(tm, FPERM_ROWS)
    t = np.arange(fp)
    fperm = np.zeros((fp, fp), np.float32)
    fperm[(t % R) * (fp // R) + t // R, t] = 1.0
    fperm = jnp.asarray(fperm, BF16)
    const = lambda shape: pl.BlockSpec(shape, lambda i: (0,) * len(shape), pipeline_mode=pl.Buffered(1))
    return pl.pallas_call(
        functools.partial(_inproj_kernel, n_hp=n_hp, fw=fw),
        grid=(n_tok // tm,),
        in_specs=[pl.BlockSpec((tm, d_model), lambda i: (i, 0)),
                  const((1, d_model)), const(w_all.shape), const(perm.shape), const(fperm.shape),
                  const(ab.shape)],
        out_specs=[pl.BlockSpec((None, n_hp, tm, LANES), tok_map)] * 3
        + [pl.BlockSpec((None, n_hp, MAX_DIL, tm // MAX_DIL, LANES), res_map)] * 3
        + [pl.BlockSpec((None, 2, R, tm // R, fw), res_map)],
        out_shape=[tok_shape] * 3 + [res_shape] * 3
        + [jax.ShapeDtypeStruct((batch, 2, R, seq // R, fw), BF16)],
        scratch_shapes=[pltpu.VMEM((tm, fw), BF16)],
        compiler_params=_cparams(1),
        name="rmsnorm_inproj",
    )(x2d, gain, w_all, perm, fperm, ab)


def _t5_bucket_static(rel, dtype):
    nb = N_REL_BUCKETS // 2
    max_exact = nb // 2
    n = np.abs(rel)
    nf = np.maximum(n, 1).astype(dtype)
    large = max_exact + (np.log(nf / dtype(max_exact)) / dtype(math.log(REL_MAX_DISTANCE / max_exact))
                         * dtype(nb - max_exact)).astype(np.int32)
    large = np.minimum(large, nb - 1)
    return np.where(rel > 0, nb, 0) + np.where(n < max_exact, n, large)


def _bucket_tiles(dilations):
    qi = np.arange(TQ)[:, None]
    kc = np.arange(TK)[None, :]
    offsets = np.array([0, HALF_WINDOW, 2 * HALF_WINDOW])
    rel = kc[None] - offsets[:, None, None] - qi[None]
    tiles = []
    for d in dilations:
        bkt = _t5_bucket_static(rel * d, np.float32)
        assert np.array_equal(bkt, _t5_bucket_static(rel * d, np.float64))
        tile = np.where(np.abs(rel) <= HALF_WINDOW, bkt, -1)
        pieces = MAX_DIL // d if 1 < d < MAX_DIL else 1
        q_order = (np.arange(TQ) % (TQ // pieces)) * pieces + np.arange(TQ) // (TQ // pieces)
        k_order = (np.arange(TK) % (TK // pieces)) * pieces + np.arange(TK) // (TK // pieces)
        tiles.append(tile[:, q_order][:, :, k_order])
    return np.concatenate(tiles, axis=0).astype(np.int32)


def _bias_kernel(table_ref, bkt_ref, out_ref, *, n_heads):
    bkt = bkt_ref[...]
    for h in range(n_heads):
        acc = jnp.full(bkt.shape, NEG_INF, F32)
        for b in range(N_REL_BUCKETS):
            acc = jnp.where(bkt == b, table_ref[b, h], acc)
        out_ref[h] = acc * LOG2E


def _bias_tiles(rel_table, dilations):
    n_heads = rel_table.shape[1]
    bkt = jnp.asarray(_bucket_tiles(dilations))
    return pl.pallas_call(
        functools.partial(_bias_kernel, n_heads=n_heads),
        grid=(bkt.shape[0],),
        in_specs=[pl.BlockSpec(memory_space=pltpu.SMEM),
                  pl.BlockSpec((None, TQ, TK), lambda t: (t, 0, 0))],
        out_specs=pl.BlockSpec((None, n_heads, TQ, TK), lambda t: (t, 0, 0, 0)),
        out_shape=jax.ShapeDtypeStruct((bkt.shape[0], n_heads, TQ, TK), F32),
        compiler_params=_cparams(1),
        name="rel_bias_tiles",
    )(rel_table.astype(F32), bkt)


def _block_attention(qb, kb, vb, bias2, first_head):
    nt_dims = (((1,), (1,)), ((), ()))
    zero = jnp.zeros_like(qb)
    q2 = jnp.concatenate([jnp.where(first_head, qb, zero), jnp.where(first_head, zero, qb)], axis=0)
    s = lax.dot_general(q2, kb, nt_dims, preferred_element_type=F32) + bias2
    m = jnp.max(s, axis=-1, keepdims=True)
    p = jnp.exp2(s - m)
    l = jnp.sum(p, axis=-1, keepdims=True)
    pv = jnp.dot(p.astype(BF16), vb, preferred_element_type=F32)
    num = jnp.where(first_head, pv[:TQ], pv[TQ:])
    den = jnp.where(first_head, l[:TQ], l[TQ:])
    top = jnp.where(first_head, m[:TQ], m[TQ:])
    return num, top, den


def _merge(a, b):
    num_a, top_a, den_a = a
    num_b, top_b, den_b = b
    top = jnp.maximum(top_a, top_b)
    w_a = jnp.exp2(top_a - top)
    w_b = jnp.exp2(top_b - top)
    den = (w_a if den_a is None else w_a * den_a) + (w_b if den_b is None else w_b * den_b)
    return w_a * num_a + w_b * num_b, top, den


def _split_f32(x):
    hi = x.astype(BF16)
    return hi, (x - hi.astype(F32)).astype(BF16)


def _attn_kernel(*refs, dilation, merge_in, out):
    q_ref, k_ref, v_ref, bias_ref = refs[:4]
    n_prev = {None: 0, 'raw': 3, 'norm': 2}[merge_in]
    prev_refs = refs[4:4 + n_prev]
    out_refs = refs[4 + n_prev:]
    n_slabs = q_ref.shape[0]
    pieces = MAX_DIL // dilation if dilation > 1 else 1
    sub_len = q_ref.shape[-2] * pieces
    n_blk = sub_len // TQ
    tq_p, tk_p = TQ // pieces, TK // pieces
    first_head = lax.broadcasted_iota(jnp.int32, (TQ, LANES), 1) < HEAD_DIM

    def load(ref, hh, r, row, n_rows):
        if dilation == 1:
            return ref[hh, row:row + n_rows, :]
        parts = [ref[hh, a * dilation + r, row:row + n_rows, :] for a in range(pieces)]
        return parts[0] if pieces == 1 else jnp.concatenate(parts, axis=0)

    def store(ref, hh, r, row, n_rows, val):
        if dilation == 1:
            ref[hh, row:row + n_rows, :] = val
        else:
            for a in range(pieces):
                ref[hh, a * dilation + r, row:row + n_rows, :] = val[a * n_rows:(a + 1) * n_rows]

    for n in range(n_blk):
        q_row = n * tq_p
        k_row = min(max(q_row - tk_p // 4, 0), sub_len // pieces - tk_p)
        edge = (1 if n > 0 else 0) + (1 if n == n_blk - 1 else 0)
        for r in range(dilation):
            for hh in range(n_slabs):
                qb = load(q_ref, hh, r, q_row, tq_p)
                kb = load(k_ref, hh, r, k_row, tk_p)
                vb = load(v_ref, hh, r, k_row, tk_p)
                bias2 = bias_ref[edge, 2 * hh:2 * hh + 2].reshape(2 * TQ, TK)
                part = _block_attention(qb, kb, vb, bias2, first_head)
                if merge_in:
                    prev = [load(ref, hh, r, q_row, tq_p).astype(F32) for ref in prev_refs]
                    part = _merge(part, prev if merge_in == 'raw' else prev + [None])
                num, top, den = part
                if out == 'raw':
                    vals = (num.astype(BF16), top, den)
                else:
                    o = (num * (1.0 / den)).astype(BF16)
                    vals = (o,) + _split_f32(top + jnp.log2(den)) if out == 'split' else (o,)
                for ref, val in zip(out_refs, vals):
                    store(ref, hh, r, q_row, tq_p, val)


def _attn_residue_kernel(*refs, dilations):
    n = len(dilations)
    q_ref, k_ref, v_ref = refs[:3]
    bias_refs = refs[3:3 + n]
    out_refs = refs[3 + n:6 + n]
    part_refs = refs[6 + n:]
    for i, d in enumerate(dilations):
        last = i == n - 1
        _attn_kernel(q_ref, k_ref, v_ref, bias_refs[i], *(part_refs if i else ()),
                     *(out_refs if last else part_refs),
                     dilation=d, merge_in='raw' if i else None, out='split' if last else 'raw')


def _attn_token_kernel(q_ref, k_ref, v_ref, bias_ref, perm_ref, po_ref, hi_ref, lo_ref, o_ref,
                       o_tok, lse_tok):
    perm = perm_ref[...]
    for hh in range(po_ref.shape[0]):
        for g in range(po_ref.shape[2] // BF16_ROWS):
            rows = slice(g * BF16_ROWS, (g + 1) * BF16_ROWS)
            o_hi = jnp.concatenate([po_ref[hh, :, rows, :].reshape(GROUP, LANES),
                                    hi_ref[hh, :, rows, :].reshape(GROUP, LANES)], axis=1)
            tok = jnp.dot(perm, o_hi, preferred_element_type=F32)
            lo = jnp.dot(perm, lo_ref[hh, :, rows, :].reshape(GROUP, LANES), preferred_element_type=F32)
            o_tok[hh, g * GROUP:(g + 1) * GROUP, :] = tok[:, :LANES].astype(BF16)
            lse_tok[hh, g * GROUP:(g + 1) * GROUP, :] = tok[:, LANES:] + lo
    _attn_kernel(q_ref, k_ref, v_ref, bias_ref, o_tok, lse_tok, o_ref,
                 dilation=1, merge_in='norm', out='final')


def _attention(q, k, v, qr, kr, vr, bias, perm, dilations):
    batch, n_hp, seq, _ = q.shape
    assert dilations[0] == MAX_DIL and dilations[-1] == 1 and n_hp % SLABS_PER_STEP == 0
    ns = SLABS_PER_STEP
    res_slab = pl.BlockSpec((None, ns, MAX_DIL, seq // MAX_DIL, LANES), lambda b, h: (b, h, 0, 0, 0))
    tok_slab = pl.BlockSpec((None, ns, seq, LANES), lambda b, h: (b, h, 0, 0))
    res_shape = lambda dt: jax.ShapeDtypeStruct((batch, n_hp, MAX_DIL, seq // MAX_DIL, LANES), dt)
    bias_spec = lambda i: pl.BlockSpec((3, 2 * ns, TQ, TK), lambda b, h: (i, h, 0, 0))

    res_dils = dilations[:-1]
    merged = pl.pallas_call(
        functools.partial(_attn_residue_kernel, dilations=res_dils),
        grid=(batch, n_hp // ns),
        in_specs=[res_slab] * 3 + [bias_spec(i) for i in range(len(res_dils))],
        out_specs=[res_slab] * 3,
        out_shape=[res_shape(BF16)] * 3,
        scratch_shapes=[pltpu.VMEM((ns, MAX_DIL, seq // MAX_DIL, LANES), dt) for dt in (BF16, F32, F32)],
        compiler_params=_cparams(2),
        name="dilated_attn_residue_major",
    )(qr, kr, vr, *([bias] * len(res_dils)))

    return pl.pallas_call(
        _attn_token_kernel,
        grid=(batch, n_hp // ns),
        in_specs=[tok_slab] * 3 + [bias_spec(len(dilations) - 1), pl.BlockSpec(perm.shape, lambda b, h: (0, 0))]
        + [res_slab] * 3,
        out_specs=tok_slab,
        out_shape=jax.ShapeDtypeStruct((batch, n_hp, seq, LANES), BF16),
        scratch_shapes=[pltpu.VMEM((ns, seq, LANES), BF16), pltpu.VMEM((ns, seq, LANES), F32)],
        compiler_params=_cparams(2),
        name="dilated_attn_d1",
    )(q, k, v, bias, perm, *merged)


def _swap_16x16(mats, perm):
    out = []
    for g in range(mats[0].shape[0] // BF16_ROWS):
        rows = jnp.concatenate([m[g * BF16_ROWS:(g + 1) * BF16_ROWS] for m in mats], axis=0)
        swapped = jnp.dot(perm, rows, preferred_element_type=F32).astype(BF16)
        out.append(swapped.reshape(BF16_ROWS, MAX_DIL, rows.shape[-1]))
    return out


def _fft_kernel(w_ref, m_ref, perm_ref, b_ref, v_ref, o_ref, z_ref):
    w = w_ref[...]
    perm = perm_ref[...]
    R = v_ref.shape[2]
    grp = BF16_ROWS
    for bg in range(R // grp):
        zs = []
        for j in range(grp):
            b = bg * grp + j
            rhs = jnp.concatenate([v_ref[0, b], v_ref[1, b]], axis=0)
            zs.append(jnp.dot(w, rhs, preferred_element_type=F32).astype(BF16))
        for plane in range(2):
            groups = _swap_16x16([z[plane * R:(plane + 1) * R] for z in zs], perm)
            for g, blk in enumerate(groups):
                z_ref[g * grp:(g + 1) * grp, plane, bg * grp:(bg + 1) * grp, :] = blk
    for kg in range(R // grp):
        xs = []
        for j in range(grp):
            k1 = kg * grp + j
            zc = z_ref[k1].reshape(2 * R, z_ref.shape[3])
            xs.append((jnp.dot(m_ref[k1], zc, preferred_element_type=F32) + b_ref[...]).astype(BF16))
        for g, blk in enumerate(_swap_16x16(xs, perm)):
            o_ref[g * grp:(g + 1) * grp, kg, :, :] = blk


def _fourier(pq, fourier_b, perm):
    batch, _, R, _, fw = pq.shape
    seq = R * R
    assert R == FFT_RADIX and R % BF16_ROWS == 0 and MAX_DIL == BF16_ROWS
    i = np.arange(R)
    ang1 = 2.0 * np.pi * np.outer(i, i) / R
    c1, s1 = np.cos(ang1), np.sin(ang1)
    w_cat = jnp.asarray(np.block([[c1, -s1], [-s1, -c1]]), BF16)
    k_all = i[:, None, None] + R * i[None, :, None]
    ang2 = 2.0 * np.pi * ((k_all * i[None, None, :]) % seq) / seq
    m_cat = jnp.asarray(np.concatenate([np.cos(ang2), np.sin(ang2)], axis=-1), BF16)
    grp = BF16_ROWS
    const = lambda shape: pl.BlockSpec(shape, lambda b: (0,) * len(shape), pipeline_mode=pl.Buffered(1))
    out = pl.pallas_call(
        _fft_kernel,
        grid=(batch,),
        in_specs=[const((2 * R, 2 * R)), const(m_cat.shape), const(perm.shape), const((1, fw)),
                  pl.BlockSpec((None, 2, R, R, fw), lambda b: (b, 0, 0, 0, 0))],
        out_specs=pl.BlockSpec((None, R, R // grp, grp, fw), lambda b: (b, 0, 0, 0, 0)),
        out_shape=jax.ShapeDtypeStruct((batch, R, R // grp, grp, fw), BF16),
        scratch_shapes=[pltpu.VMEM((R, 2, R, fw), BF16)],
        compiler_params=_cparams(1),
        name="fft_positions",
    )(w_cat, m_cat, perm, fourier_b.reshape(1, fw).astype(F32), pq)
    return out.reshape(batch, seq, fw)


def _mix_ffn_kernel(ap_ref, a_ref, an_ref, fp_ref, f_ref, fn_ref, xp_ref, x_ref, xn_ref,
                    ga_ref, gf_ref, wo_ref, g_ref, wg_ref, wv_ref, cw_ref, cb_ref, wd_ref, gfin_ref,
                    o_ref, act_ref, *, tiles_per_seq, fc, final_norm, n_hp):
    i = pl.program_id(0)
    tm = x_ref.shape[0]
    n_ext = tm + 2 * HALO
    ext = lambda prev, cur, nxt: jnp.concatenate([prev, cur, nxt], axis=0)

    parts = [ext(ap_ref[j], a_ref[j], an_ref[j]).astype(F32) for j in range(n_hp)]
    ssq = functools.reduce(lambda a, b: a + b,
                           [jnp.sum(p * p, axis=-1, keepdims=True) for p in parts])
    inv = lax.rsqrt(ssq / (n_hp * LANES) + EPS)
    ga = ga_ref[...]
    cols = [(p * inv * ga[:, j * LANES:(j + 1) * LANES]).astype(BF16) for j, p in enumerate(parts)]
    cols.append(_rms(ext(fp_ref[...], f_ref[...], fn_ref[...]).astype(F32), gf_ref[...]).astype(BF16))
    mixed = jnp.concatenate(cols, axis=1)
    x_ext = ext(xp_ref[...], x_ref[...], xn_ref[...]) + jnp.dot(mixed, wo_ref[...],
                                                                 preferred_element_type=F32)
    x = x_ext[HALO:HALO + tm]

    keep_prev = (i % tiles_per_seq != 0).astype(F32)
    keep_next = (i % tiles_per_seq != tiles_per_seq - 1).astype(F32)
    row = lax.broadcasted_iota(jnp.int32, (n_ext, 1), 0)
    keep = jnp.where(row < HALO, keep_prev, jnp.where(row >= HALO + tm, keep_next, 1.0))
    h_ext = (_rms(x_ext, g_ref[...]) * keep).astype(BF16)
    h = h_ext[HALO:HALO + tm]
    d_ff = wg_ref.shape[1]

    def up(c0):
        cs = slice(c0, c0 + fc)
        return (jnp.dot(h_ext, wg_ref[:, cs], preferred_element_type=F32),
                jnp.dot(h, wv_ref[:, cs], preferred_element_type=F32))

    nxt = up(0)
    for c0 in range(0, d_ff, fc):
        cs = slice(c0, c0 + fc)
        g, val = nxt
        if c0 + fc < d_ff:
            nxt = up(c0 + fc)
        g_prev = pltpu.roll(g, 1, axis=0)[HALO:HALO + tm]
        g_next = pltpu.roll(g, n_ext - 1, axis=0)[HALO:HALO + tm]
        cw = cw_ref[:, cs]
        conv = cw[0:1] * g_prev + cw[1:2] * g[HALO:HALO + tm] + cw[2:3] * g_next + cb_ref[:, cs]
        act_ref[:, cs] = (conv * (1.0 / (1.0 + jnp.exp(-conv))) * val).astype(BF16)
    y = x + jnp.dot(act_ref[...], wd_ref[...], preferred_element_type=F32)
    o_ref[...] = _rms(y, gfin_ref[...]) if final_norm else y


def _mix_ffn(attn, four, x2d, ga, gf, w_out, gain, wg, wv, conv_w, conv_b, wd, gfin, tm, fc, final_norm):
    batch, n_hp, seq, _ = attn.shape
    fw = four.shape[-1]
    n_tok, d_model = x2d.shape
    d_ff = wg.shape[1]
    nt = seq // tm
    per_tile = tm // HALO
    last = seq // HALO - 1
    prev_blk = lambda i: jnp.maximum((i % nt) * per_tile - 1, 0)
    next_blk = lambda i: jnp.minimum((i % nt + 1) * per_tile, last)
    seq_blk = lambda i: (i // nt) * (last + 1)
    const = lambda shape: pl.BlockSpec(shape, lambda i: (0,) * len(shape), pipeline_mode=pl.Buffered(1))

    def triple(cur_shape, halo_shape, index):
        return [pl.BlockSpec(halo_shape, lambda i: index(i // nt, prev_blk(i))),
                pl.BlockSpec(cur_shape, lambda i: index(i // nt, i % nt)),
                pl.BlockSpec(halo_shape, lambda i: index(i // nt, next_blk(i)))]

    in_specs = (triple((None, n_hp, tm, LANES), (None, n_hp, HALO, LANES), lambda b, t: (b, 0, t, 0))
                + triple((None, tm, fw), (None, HALO, fw), lambda b, t: (b, t, 0))
                + [pl.BlockSpec((HALO, d_model), lambda i: (seq_blk(i) + prev_blk(i), 0)),
                   pl.BlockSpec((tm, d_model), lambda i: (i, 0)),
                   pl.BlockSpec((HALO, d_model), lambda i: (seq_blk(i) + next_blk(i), 0))]
                + [const((1, n_hp * LANES)), const((1, fw)), const(w_out.shape), const((1, d_model)),
                   const(wg.shape), const(wv.shape), const(conv_w.shape), const((1, d_ff)), const(wd.shape),
                   const((1, d_model))])
    return pl.pallas_call(
        functools.partial(_mix_ffn_kernel, tiles_per_seq=nt, fc=fc, final_norm=final_norm, n_hp=n_hp),
        grid=(n_tok // tm,),
        in_specs=in_specs,
        out_specs=pl.BlockSpec((tm, d_model), lambda i: (i, 0)),
        out_shape=jax.ShapeDtypeStruct((n_tok, d_model), F32),
        scratch_shapes=[pltpu.VMEM((tm, d_ff), BF16)],
        compiler_params=_cparams(1),
        name="outproj_convglu_ffn",
    )(attn, attn, attn, four, four, four, x2d, x2d, x2d, ga, gf, w_out, gain, wg, wv, conv_w, conv_b, wd, gfin)


def kernel(x, norm_mix_gain, w_in, attn_out_gain, rel_bias_table, fourier_w, fourier_b, fourier_out_gain, w_out, norm_ffn_gain, w_gate, w_val, conv_w, conv_b, w_down, final_norm_gain):
    batch, seq, d_model = x.shape
    depth = w_in.shape[0]
    n_heads = rel_bias_table.shape[1]
    attn_w = n_heads * HEAD_DIM
    n_hp = attn_w // LANES
    fw = fourier_w.shape[1] * fourier_w.shape[2]
    assert all(w // (2 * d) == HALF_WINDOW for w, d in DILATED_PATTERNS)
    dilations = sorted((d for _, d in DILATED_PATTERNS), reverse=True)
    tm = 1024
    row = lambda g: g.reshape(1, -1).astype(F32)
    perm = _group_permutation()
    bias = _bias_tiles(rel_bias_table, dilations)

    x2d = x.reshape(batch * seq, d_model)
    for layer in range(depth):
        w_l = w_in[layer]
        ab = _fold_fourier_weights(fourier_w[layer], seq).astype(BF16)
        w_all = jnp.concatenate([w_l[:, :attn_w] * (HEAD_DIM ** -0.5 * LOG2E), w_l[:, attn_w:]],
                                axis=1).astype(BF16)
        q, k, v, qr, kr, vr, pq = _inproj(x2d, row(norm_mix_gain[layer]), w_all, perm, ab, batch, seq, n_hp,
                                          fw, tm)
        attn = _attention(q, k, v, qr, kr, vr, bias, perm, dilations)
        four = _fourier(pq, fourier_b[layer].reshape(-1), perm)
        x2d = _mix_ffn(attn, four, x2d, row(attn_out_gain[layer]), row(fourier_out_gain[layer]),
                       w_out[layer].astype(BF16), row(norm_ffn_gain[layer]), w_gate[layer].astype(BF16),
                       w_val[layer].astype(BF16), conv_w[layer].astype(F32), row(conv_b[layer]),
                       w_down[layer].astype(BF16), row(final_norm_gain), tm, 256, layer == depth - 1)
    return x2d.reshape(batch, seq, d_model)
```

```python
import functools
import math

import numpy as np
import jax
import jax.numpy as jnp
from jax import lax
from jax.experimental import pallas as pl
from jax.experimental.pallas import tpu as pltpu

EPS = 1e-6
NEG_INF = -1e30
LOG2E = math.log2(math.e)
HEAD_DIM = 64
DILATED_PATTERNS = ((128, 1), (512, 4), (2048, 16))
N_REL_BUCKETS = 32
REL_MAX_DISTANCE = 1024

LANES = 128
BF16_ROWS = 16
HALF_WINDOW = 64
TQ = 2 * HALF_WINDOW
TK = 4 * HALF_WINDOW
MAX_DIL = max(d for _, d in DILATED_PATTERNS)
GROUP = MAX_DIL * BF16_ROWS
SLABS_PER_STEP = 2
FFT_RADIX = 64
FPERM_ROWS = 512
HALO = BF16_ROWS
TOKEN_TILE = 1024
FF_CHUNK = 256
VMEM_LIMIT = 56 * 1024 * 1024

BF16 = jnp.bfloat16
F32 = jnp.float32


def _cparams(n_axes):
    return pltpu.CompilerParams(dimension_semantics=("arbitrary",) * n_axes,
                                vmem_limit_bytes=VMEM_LIMIT)


def _rms(x, gain):
    ms = jnp.mean(x * x, axis=-1, keepdims=True)
    return x * lax.rsqrt(ms + EPS) * gain


def _group_permutation():
    t = np.arange(GROUP)
    swapped = (t % MAX_DIL) * BF16_ROWS + t // MAX_DIL
    perm = np.zeros((GROUP, GROUP), np.float32)
    perm[swapped, t] = 1.0
    assert MAX_DIL == BF16_ROWS and np.array_equal(perm, perm.T)
    return jnp.asarray(perm, BF16)


def _prep_kernel(fw_ref, cc_ref, sc_ref, ab_ref, *, groups):
    hi = lax.Precision.HIGHEST
    g = pl.program_id(0)
    fw = fw_ref[...]
    a = jnp.dot(cc_ref[...], fw, precision=hi, preferred_element_type=F32)
    b = jnp.dot(sc_ref[...], fw, precision=hi, preferred_element_type=F32)
    gd = fw.shape[0]
    wide = jnp.concatenate([a] * groups + [b] * groups, axis=1)
    col_group = (lax.broadcasted_iota(jnp.int32, wide.shape, 1) // gd) % groups
    ab_ref[...] = jnp.where(col_group == g, wide, 0.0)


def _fold_fourier_weights(fourier_w, seq):
    groups, gd, _ = fourier_w.shape
    ang = 2.0 * np.pi * np.outer(np.arange(gd), np.arange(gd)) / gd
    scale = 1.0 / math.sqrt(seq * gd)
    cc = jnp.asarray(np.cos(ang) * scale, F32)
    sc = jnp.asarray(np.sin(ang) * scale, F32)
    return pl.pallas_call(
        functools.partial(_prep_kernel, groups=groups),
        grid=(groups,),
        in_specs=[pl.BlockSpec((None, gd, gd), lambda g: (g, 0, 0)),
                  pl.BlockSpec((gd, gd), lambda g: (0, 0)),
                  pl.BlockSpec((gd, gd), lambda g: (0, 0))],
        out_specs=pl.BlockSpec((gd, 2 * groups * gd), lambda g: (g, 0)),
        out_shape=jax.ShapeDtypeStruct((groups * gd, 2 * groups * gd), F32),
        compiler_params=_cparams(1),
        name="fourier_weight_fold",
    )(fourier_w, cc, sc)


def _inproj_kernel(x_ref, g_ref, w_ref, perm_ref, fperm_ref, ab_ref, q_ref, k_ref, v_ref, qr_ref, kr_ref,
                   vr_ref, pq_ref, u_ref, *, n_hp, fw):
    h = _rms(x_ref[...], g_ref[...]).astype(BF16)
    tm = h.shape[0]
    perm = perm_ref[...]

    def proj(c0, n):
        return jnp.dot(h, w_ref[:, c0:c0 + n], preferred_element_type=F32)

    aw = n_hp * LANES

    def project(t, tok_ref):
        for c in range(n_hp // 2):
            res = proj(t * aw + c * 2 * LANES, 2 * LANES).astype(BF16)
            tok_ref[2 * c] = res[:, :LANES]
            tok_ref[2 * c + 1] = res[:, LANES:]

    def permute(tok_ref, res_ref):
        for c in range(n_hp // 2):
            for g in range(tm // GROUP):
                tok = jnp.concatenate([tok_ref[2 * c, g * GROUP:(g + 1) * GROUP, :],
                                       tok_ref[2 * c + 1, g * GROUP:(g + 1) * GROUP, :]], axis=1)
                rows = jnp.dot(perm, tok, preferred_element_type=F32).astype(BF16)
                for half in range(2):
                    blk = rows[:, half * LANES:(half + 1) * LANES].reshape(MAX_DIL, BF16_ROWS, LANES)
                    res_ref[2 * c + half, :, g * BF16_ROWS:(g + 1) * BF16_ROWS, :] = blk

    u_ref[...] = proj(3 * aw, fw).astype(BF16)
    project(0, q_ref)
    project(1, k_ref)
    permute(q_ref, qr_ref)
    fp = fperm_ref.shape[0]
    a_per = fp // FFT_RADIX
    u_rows = [jnp.dot(fperm_ref[...], u_ref[s * fp:(s + 1) * fp, :], preferred_element_type=F32).astype(BF16)
              for s in range(tm // fp)]
    project(2, v_ref)
    permute(k_ref, kr_ref)
    planes = jnp.concatenate(
        [jnp.dot(rows, ab_ref[...], preferred_element_type=F32).reshape(FFT_RADIX, a_per, 2 * fw)
         for rows in u_rows], axis=1).astype(BF16)
    for t in range(2):
        pq_ref[t] = planes[:, :, t * fw:(t + 1) * fw]
    permute(v_ref, vr_ref)


def _inproj(x2d, gain, w_all, perm, ab, batch, seq, n_hp, fw, tm):
    n_tok, d_model = x2d.shape
    nt = seq // tm
    tok_map = lambda i: (i // nt, 0, i % nt, 0)
    res_map = lambda i: (i // nt, 0, 0, i % nt, 0)
    tok_shape = jax.ShapeDtypeStruct((batch, n_hp, seq, LANES), BF16)
    res_shape = jax.ShapeDtypeStruct((batch, n_hp, MAX_DIL, seq // MAX_DIL, LANES), BF16)
    R = FFT_RADIX
    fp = min(tm, FPERM_ROWS)
    t = np.arange(fp)
    fperm = np.zeros((fp, fp), np.float32)
    fperm[(t % R) * (fp // R) + t // R, t] = 1.0
    fperm = jnp.asarray(fperm, BF16)
    const = lambda shape: pl.BlockSpec(shape, lambda i: (0,) * len(shape), pipeline_mode=pl.Buffered(1))
    return pl.pallas_call(
        functools.partial(_inproj_kernel, n_hp=n_hp, fw=fw),
        grid=(n_tok // tm,),
        in_specs=[pl.BlockSpec((tm, d_model), lambda i: (i, 0)),
                  const((1, d_model)), const(w_all.shape), const(perm.shape), const(fperm.shape),
                  const(ab.shape)],
        out_specs=[pl.BlockSpec((None, n_hp, tm, LANES), tok_map)] * 3
        + [pl.BlockSpec((None, n_hp, MAX_DIL, tm // MAX_DIL, LANES), res_map)] * 3
        + [pl.BlockSpec((None, 2, R, tm // R, fw), res_map)],
        out_shape=[tok_shape] * 3 + [res_shape] * 3
        + [jax.ShapeDtypeStruct((batch, 2, R, seq // R, fw), BF16)],
        scratch_shapes=[pltpu.VMEM((tm, fw), BF16)],
        compiler_params=_cparams(1),
        name="rmsnorm_inproj",
    )(x2d, gain, w_all, perm, fperm, ab)


def _t5_bucket_static(rel, dtype):
    nb = N_REL_BUCKETS // 2
    max_exact = nb // 2
    n = np.abs(rel)
    nf = np.maximum(n, 1).astype(dtype)
    large = max_exact + (np.log(nf / dtype(max_exact)) / dtype(math.log(REL_MAX_DISTANCE / max_exact))
                         * dtype(nb - max_exact)).astype(np.int32)
    large = np.minimum(large, nb - 1)
    return np.where(rel > 0, nb, 0) + np.where(n < max_exact, n, large)


def _bucket_tiles(dilations):
    qi = np.arange(TQ)[:, None]
    kc = np.arange(TK)[None, :]
    offsets = np.array([0, HALF_WINDOW, 2 * HALF_WINDOW])
    rel = kc[None] - offsets[:, None, None] - qi[None]
    tiles = []
    for d in dilations:
        bkt = _t5_bucket_static(rel * d, np.float32)
        assert np.array_equal(bkt, _t5_bucket_static(rel * d, np.float64))
        tile = np.where(np.abs(rel) <= HALF_WINDOW, bkt, -1)
        pieces = MAX_DIL // d if 1 < d < MAX_DIL else 1
        q_order = (np.arange(TQ) % (TQ // pieces)) * pieces + np.arange(TQ) // (TQ // pieces)
        k_order = (np.arange(TK) % (TK // pieces)) * pieces + np.arange(TK) // (TK // pieces)
        tiles.append(tile[:, q_order][:, :, k_order])
    return np.concatenate(tiles, axis=0).astype(np.int32)


def _bias_kernel(table_ref, bkt_ref, out_ref, *, n_heads):
    bkt = bkt_ref[...]
    for h in range(n_heads):
        acc = jnp.full(bkt.shape, NEG_INF, F32)
        for b in range(N_REL_BUCKETS):
            acc = jnp.where(bkt == b, table_ref[b, h], acc)
        out_ref[h] = acc * LOG2E


def _bias_tiles(rel_table, dilations):
    n_heads = rel_table.shape[1]
    bkt = jnp.asarray(_bucket_tiles(dilations))
    return pl.pallas_call(
        functools.partial(_bias_kernel, n_heads=n_heads),
        grid=(bkt.shape[0],),
        in_specs=[pl.BlockSpec(memory_space=pltpu.SMEM),
                  pl.BlockSpec((None, TQ, TK), lambda t: (t, 0, 0))],
        out_specs=pl.BlockSpec((None, n_heads, TQ, TK), lambda t: (t, 0, 0, 0)),
        out_shape=jax.ShapeDtypeStruct((bkt.shape[0], n_heads, TQ, TK), F32),
        compiler_params=_cparams(1),
        name="rel_bias_tiles",
    )(rel_table.astype(F32), bkt)


def _block_attention(qb, kb, vb, bias2, first_head):
    nt_dims = (((1,), (1,)), ((), ()))
    zero = jnp.zeros_like(qb)
    q2 = jnp.concatenate([jnp.where(first_head, qb, zero), jnp.where(first_head, zero, qb)], axis=0)
    s = lax.dot_general(q2, kb, nt_dims, preferred_element_type=F32) + bias2
    m = jnp.max(s, axis=-1, keepdims=True)
    p = jnp.exp2(s - m).astype(BF16)
    pv = jnp.dot(p, jnp.concatenate([vb, jnp.ones_like(vb)], axis=1), preferred_element_type=F32)
    num = jnp.where(first_head, pv[:TQ, :LANES], pv[TQ:, :LANES])
    den = jnp.where(first_head, pv[:TQ, LANES:], pv[TQ:, LANES:])
    top = jnp.where(first_head, m[:TQ], m[TQ:])
    return num, top, den


def _merge(a, b):
    num_a, top_a, den_a = a
    num_b, top_b, den_b = b
    top = jnp.maximum(top_a, top_b)
    w_a = jnp.exp2(top_a - top)
    w_b = jnp.exp2(top_b - top)
    den = (w_a if den_a is None else w_a * den_a) + (w_b if den_b is None else w_b * den_b)
    return w_a * num_a + w_b * num_b, top, den


def _split_f32(x):
    hi = x.astype(BF16)
    return hi, (x - hi.astype(F32)).astype(BF16)


def _attn_kernel(*refs, dilation, merge_in, out):
    q_ref, k_ref, v_ref, bias_ref = refs[:4]
    n_prev = {None: 0, 'raw': 3, 'norm': 2}[merge_in]
    prev_refs = refs[4:4 + n_prev]
    out_refs = refs[4 + n_prev:]
    n_slabs = q_ref.shape[0]
    pieces = MAX_DIL // dilation if dilation > 1 else 1
    sub_len = q_ref.shape[-2] * pieces
    n_blk = sub_len // TQ
    tq_p, tk_p = TQ // pieces, TK // pieces
    first_head = lax.broadcasted_iota(jnp.int32, (TQ, LANES), 1) < HEAD_DIM

    def load(ref, hh, r, row, n_rows):
        if dilation == 1:
            return ref[hh, row:row + n_rows, :]
        parts = [ref[hh, a * dilation + r, row:row + n_rows, :] for a in range(pieces)]
        return parts[0] if pieces == 1 else jnp.concatenate(parts, axis=0)

    def store(ref, hh, r, row, n_rows, val):
        if dilation == 1:
            ref[hh, row:row + n_rows, :] = val
        else:
            for a in range(pieces):
                ref[hh, a * dilation + r, row:row + n_rows, :] = val[a * n_rows:(a + 1) * n_rows]

    for n in range(n_blk):
        q_row = n * tq_p
        k_row = min(max(q_row - tk_p // 4, 0), sub_len // pieces - tk_p)
        edge = (1 if n > 0 else 0) + (1 if n == n_blk - 1 else 0)
        for r in range(dilation):
            for hh in range(n_slabs):
                qb = load(q_ref, hh, r, q_row, tq_p)
                kb = load(k_ref, hh, r, k_row, tk_p)
                vb = load(v_ref, hh, r, k_row, tk_p)
                bias2 = bias_ref[edge, 2 * hh:2 * hh + 2].reshape(2 * TQ, TK)
                part = _block_attention(qb, kb, vb, bias2, first_head)
                if merge_in:
                    prev = [load(ref, hh, r, q_row, tq_p).astype(F32) for ref in prev_refs]
                    part = _merge(part, prev if merge_in == 'raw' else prev + [None])
                num, top, den = part
                if out == 'raw':
                    vals = (num.astype(BF16), top, den)
                else:
                    o = (num * (1.0 / den)).astype(BF16)
                    vals = (o,) + _split_f32(top + jnp.log2(den)) if out == 'split' else (o,)
                for ref, val in zip(out_refs, vals):
                    store(ref, hh, r, q_row, tq_p, val)


def _attn_residue_kernel(*refs, dilations):
    n = len(dilations)
    q_ref, k_ref, v_ref = refs[:3]
    bias_refs = refs[3:3 + n]
    out_refs = refs[3 + n:6 + n]
    part_refs = refs[6 + n:]
    for i, d in enumerate(dilations):
        last = i == n - 1
        _attn_kernel(q_ref, k_ref, v_ref, bias_refs[i], *(part_refs if i else ()),
                     *(out_refs if last else part_refs),
                     dilation=d, merge_in='raw' if i else None, out='split' if last else 'raw')


def _attn_token_kernel(q_ref, k_ref, v_ref, bias_ref, perm_ref, po_ref, hi_ref, lo_ref, o_ref,
                       o_tok, lse_tok):
    perm = perm_ref[...]
    group_rows = lambda ref, hh, g: ref[hh, :, g * BF16_ROWS:(g + 1) * BF16_ROWS, :].reshape(GROUP, LANES)
    for hh in range(po_ref.shape[0]):
        for g in range(0, po_ref.shape[2] // BF16_ROWS, 2):
            lo = jnp.dot(perm, jnp.concatenate([group_rows(lo_ref, hh, g), group_rows(lo_ref, hh, g + 1)], axis=1),
                         preferred_element_type=F32)
            for j in range(2):
                o_hi = jnp.concatenate([group_rows(po_ref, hh, g + j), group_rows(hi_ref, hh, g + j)], axis=1)
                tok = jnp.dot(perm, o_hi, preferred_element_type=F32)
                rows = slice((g + j) * GROUP, (g + j + 1) * GROUP)
                o_tok[hh, rows, :] = tok[:, :LANES].astype(BF16)
                lse_tok[hh, rows, :] = tok[:, LANES:] + lo[:, j * LANES:(j + 1) * LANES]
    _attn_kernel(q_ref, k_ref, v_ref, bias_ref, o_tok, lse_tok, o_ref,
                 dilation=1, merge_in='norm', out='final')


def _attention(q, k, v, qr, kr, vr, bias, perm, dilations):
    batch, n_hp, seq, _ = q.shape
    assert dilations[0] == MAX_DIL and dilations[-1] == 1 and n_hp % SLABS_PER_STEP == 0
    ns = SLABS_PER_STEP
    res_slab = pl.BlockSpec((None, ns, MAX_DIL, seq // MAX_DIL, LANES), lambda b, h: (b, h, 0, 0, 0))
    tok_slab = pl.BlockSpec((None, ns, seq, LANES), lambda b, h: (b, h, 0, 0))
    res_shape = lambda dt: jax.ShapeDtypeStruct((batch, n_hp, MAX_DIL, seq // MAX_DIL, LANES), dt)
    bias_spec = lambda i: pl.BlockSpec((3, 2 * ns, TQ, TK), lambda b, h: (i, h, 0, 0))

    res_dils = dilations[:-1]
    merged = pl.pallas_call(
        functools.partial(_attn_residue_kernel, dilations=res_dils),
        grid=(batch, n_hp // ns),
        in_specs=[res_slab] * 3 + [bias_spec(i) for i in range(len(res_dils))],
        out_specs=[res_slab] * 3,
        out_shape=[res_shape(BF16)] * 3,
        scratch_shapes=[pltpu.VMEM((ns, MAX_DIL, seq // MAX_DIL, LANES), dt) for dt in (BF16, F32, F32)],
        compiler_params=_cparams(2),
        name="dilated_attn_residue_major",
    )(qr, kr, vr, *([bias] * len(res_dils)))

    return pl.pallas_call(
        _attn_token_kernel,
        grid=(batch, n_hp // ns),
        in_specs=[tok_slab] * 3 + [bias_spec(len(dilations) - 1), pl.BlockSpec(perm.shape, lambda b, h: (0, 0))]
        + [res_slab] * 3,
        out_specs=tok_slab,
        out_shape=jax.ShapeDtypeStruct((batch, n_hp, seq, LANES), BF16),
        scratch_shapes=[pltpu.VMEM((ns, seq, LANES), BF16), pltpu.VMEM((ns, seq, LANES), F32)],
        compiler_params=_cparams(2),
        name="dilated_attn_d1",
    )(q, k, v, bias, perm, *merged)


def _swap_16x16(mats, perm):
    out = []
    for g in range(mats[0].shape[0] // BF16_ROWS):
        rows = jnp.concatenate([m[g * BF16_ROWS:(g + 1) * BF16_ROWS] for m in mats], axis=0)
        swapped = jnp.dot(perm, rows, preferred_element_type=F32).astype(BF16)
        out.append(swapped.reshape(BF16_ROWS, MAX_DIL, rows.shape[-1]))
    return out


def _fft_kernel(w_ref, m_ref, perm_ref, b_ref, v_ref, o_ref, z_ref):
    w = w_ref[...]
    perm = perm_ref[...]
    R = v_ref.shape[2]
    grp = BF16_ROWS
    for bg in range(R // grp):
        zs = []
        for j in range(grp):
            b = bg * grp + j
            rhs = jnp.concatenate([v_ref[0, b], v_ref[1, b]], axis=0)
            zs.append(jnp.dot(w, rhs, preferred_element_type=F32).astype(BF16))
        for plane in range(2):
            groups = _swap_16x16([z[plane * R:(plane + 1) * R] for z in zs], perm)
            for g, blk in enumerate(groups):
                z_ref[g * grp:(g + 1) * grp, plane, bg * grp:(bg + 1) * grp, :] = blk
    for kg in range(R // grp):
        xs = []
        for j in range(grp):
            k1 = kg * grp + j
            zc = z_ref[k1].reshape(2 * R, z_ref.shape[3])
            xs.append((jnp.dot(m_ref[k1], zc, preferred_element_type=F32) + b_ref[...]).astype(BF16))
        for g, blk in enumerate(_swap_16x16(xs, perm)):
            o_ref[g * grp:(g + 1) * grp, kg, :, :] = blk


def _fourier(pq, fourier_b, perm):
    batch, _, R, _, fw = pq.shape
    seq = R * R
    assert R == FFT_RADIX and R % BF16_ROWS == 0 and MAX_DIL == BF16_ROWS
    i = np.arange(R)
    ang1 = 2.0 * np.pi * np.outer(i, i) / R
    c1, s1 = np.cos(ang1), np.sin(ang1)
    w_cat = jnp.asarray(np.block([[c1, -s1], [-s1, -c1]]), BF16)
    k_all = i[:, None, None] + R * i[None, :, None]
    ang2 = 2.0 * np.pi * ((k_all * i[None, None, :]) % seq) / seq
    m_cat = jnp.asarray(np.concatenate([np.cos(ang2), np.sin(ang2)], axis=-1), BF16)
    grp = BF16_ROWS
    const = lambda shape: pl.BlockSpec(shape, lambda b: (0,) * len(shape), pipeline_mode=pl.Buffered(1))
    out = pl.pallas_call(
        _fft_kernel,
        grid=(batch,),
        in_specs=[const((2 * R, 2 * R)), const(m_cat.shape), const(perm.shape), const((1, fw)),
                  pl.BlockSpec((None, 2, R, R, fw), lambda b: (b, 0, 0, 0, 0))],
        out_specs=pl.BlockSpec((None, R, R // grp, grp, fw), lambda b: (b, 0, 0, 0, 0)),
        out_shape=jax.ShapeDtypeStruct((batch, R, R // grp, grp, fw), BF16),
        scratch_shapes=[pltpu.VMEM((R, 2, R, fw), BF16)],
        compiler_params=_cparams(1),
        name="fft_positions",
    )(w_cat, m_cat, perm, fourier_b.reshape(1, fw).astype(F32), pq)
    return out.reshape(batch, seq, fw)


def _mix_ffn_kernel(ap_ref, a_ref, an_ref, fp_ref, f_ref, fn_ref, xp_ref, x_ref, xn_ref,
                    ga_ref, gf_ref, wo_ref, g_ref, wg_ref, wv_ref, cw_ref, cb_ref, wd_ref, gfin_ref,
                    o_ref, act_ref, *, tiles_per_seq, fc, final_norm, n_hp):
    i = pl.program_id(0)
    tm = x_ref.shape[0]
    n_ext = tm + 2 * HALO
    ext = lambda prev, cur, nxt: jnp.concatenate([prev, cur, nxt], axis=0)

    parts = [ext(ap_ref[j], a_ref[j], an_ref[j]).astype(F32) for j in range(n_hp)]
    ssq = functools.reduce(lambda a, b: a + b,
                           [jnp.sum(p * p, axis=-1, keepdims=True) for p in parts])
    inv = lax.rsqrt(ssq / (n_hp * LANES) + EPS)
    ga = ga_ref[...]
    cols = [(p * inv * ga[:, j * LANES:(j + 1) * LANES]).astype(BF16) for j, p in enumerate(parts)]
    cols.append(_rms(ext(fp_ref[...], f_ref[...], fn_ref[...]).astype(F32), gf_ref[...]).astype(BF16))
    mixed = jnp.concatenate(cols, axis=1)
    x_ext = ext(xp_ref[...], x_ref[...], xn_ref[...]) + jnp.dot(mixed, wo_ref[...],
                                                                 preferred_element_type=F32)
    x = x_ext[HALO:HALO + tm]

    keep_prev = (i % tiles_per_seq != 0).astype(F32)
    keep_next = (i % tiles_per_seq != tiles_per_seq - 1).astype(F32)
    row = lax.broadcasted_iota(jnp.int32, (n_ext, 1), 0)
    keep = jnp.where(row < HALO, keep_prev, jnp.where(row >= HALO + tm, keep_next, 1.0))
    h_ext = (_rms(x_ext, g_ref[...]) * keep).astype(BF16)
    h = h_ext[HALO:HALO + tm]
    d_ff = wg_ref.shape[1]

    def up(c0):
        cs = slice(c0, c0 + fc)
        return (jnp.dot(h_ext, wg_ref[:, cs], preferred_element_type=F32),
                jnp.dot(h, wv_ref[:, cs], preferred_element_type=F32))

    nxt = up(0)
    for c0 in range(0, d_ff, fc):
        cs = slice(c0, c0 + fc)
        g, val = nxt
        if c0 + fc < d_ff:
            nxt = up(c0 + fc)
        g_prev = pltpu.roll(g, 1, axis=0)[HALO:HALO + tm]
        g_next = pltpu.roll(g, n_ext - 1, axis=0)[HALO:HALO + tm]
        cw = cw_ref[:, cs]
        conv = cw[0:1] * g_prev + cw[1:2] * g[HALO:HALO + tm] + cw[2:3] * g_next + cb_ref[:, cs]
        act_ref[:, cs] = (conv * (1.0 / (1.0 + jnp.exp(-conv))) * val).astype(BF16)
    y = x + jnp.dot(act_ref[...], wd_ref[...], preferred_element_type=F32)
    o_ref[...] = _rms(y, gfin_ref[...]) if final_norm else y


def _mix_ffn(attn, four, x2d, ga, gf, w_out, gain, wg, wv, conv_w, conv_b, wd, gfin, tm, fc, final_norm):
    batch, n_hp, seq, _ = attn.shape
    fw = four.shape[-1]
    n_tok, d_model = x2d.shape
    d_ff = wg.shape[1]
    nt = seq // tm
    per_tile = tm // HALO
    last = seq // HALO - 1
    prev_blk = lambda i: jnp.maximum((i % nt) * per_tile - 1, 0)
    next_blk = lambda i: jnp.minimum((i % nt + 1) * per_tile, last)
    seq_blk = lambda i: (i // nt) * (last + 1)
    const = lambda shape: pl.BlockSpec(shape, lambda i: (0,) * len(shape), pipeline_mode=pl.Buffered(1))

    def triple(cur_shape, halo_shape, index):
        return [pl.BlockSpec(halo_shape, lambda i: index(i // nt, prev_blk(i))),
                pl.BlockSpec(cur_shape, lambda i: index(i // nt, i % nt)),
                pl.BlockSpec(halo_shape, lambda i: index(i // nt, next_blk(i)))]

    in_specs = (triple((None, n_hp, tm, LANES), (None, n_hp, HALO, LANES), lambda b, t: (b, 0, t, 0))
                + triple((None, tm, fw), (None, HALO, fw), lambda b, t: (b, t, 0))
                + [pl.BlockSpec((HALO, d_model), lambda i: (seq_blk(i) + prev_blk(i), 0)),
                   pl.BlockSpec((tm, d_model), lambda i: (i, 0)),
                   pl.BlockSpec((HALO, d_model), lambda i: (seq_blk(i) + next_blk(i), 0))]
                + [const((1, n_hp * LANES)), const((1, fw)), const(w_out.shape), const((1, d_model)),
                   const(wg.shape), const(wv.shape), const(conv_w.shape), const((1, d_ff)), const(wd.shape),
                   const((1, d_model))])
    return pl.pallas_call(
        functools.partial(_mix_ffn_kernel, tiles_per_seq=nt, fc=fc, final_norm=final_norm, n_hp=n_hp),
        grid=(n_tok // tm,),
        in_specs=in_specs,
        out_specs=pl.BlockSpec((tm, d_model), lambda i: (i, 0)),
        out_shape=jax.ShapeDtypeStruct((n_tok, d_model), F32),
        scratch_shapes=[pltpu.VMEM((tm, d_ff), BF16)],
        compiler_params=_cparams(1),
        name="outproj_convglu_ffn",
    )(attn, attn, attn, four, four, four, x2d, x2d, x2d, ga, gf, w_out, gain, wg, wv, conv_w, conv_b, wd, gfin)


def kernel(x, norm_mix_gain, w_in, attn_out_gain, rel_bias_table, fourier_w, fourier_b, fourier_out_gain, w_out, norm_ffn_gain, w_gate, w_val, conv_w, conv_b, w_down, final_norm_gain):
    batch, seq, d_model = x.shape
    depth = w_in.shape[0]
    n_heads = rel_bias_table.shape[1]
    attn_w = n_heads * HEAD_DIM
    n_hp = attn_w // LANES
    fw = fourier_w.shape[1] * fourier_w.shape[2]
    assert all(w // (2 * d) == HALF_WINDOW for w, d in DILATED_PATTERNS)
    assert seq == FFT_RADIX ** 2 and seq % TOKEN_TILE == 0 and seq % (MAX_DIL * TK) == 0
    assert TOKEN_TILE % FPERM_ROWS == 0 and (TOKEN_TILE // FFT_RADIX) % BF16_ROWS == 0
    assert n_hp % 2 == 0 and n_hp % SLABS_PER_STEP == 0 and w_gate.shape[-1] % FF_CHUNK == 0
    dilations = sorted((d for _, d in DILATED_PATTERNS), reverse=True)
    row = lambda g: g.reshape(1, -1).astype(F32)
    perm = _group_permutation()
    bias = _bias_tiles(rel_bias_table, dilations)

    x2d = x.reshape(batch * seq, d_model)
    for layer in range(depth):
        w_l = w_in[layer]
        ab = _fold_fourier_weights(fourier_w[layer], seq).astype(BF16)
        w_all = jnp.concatenate([w_l[:, :attn_w] * (HEAD_DIM ** -0.5 * LOG2E), w_l[:, attn_w:]],
                                axis=1).astype(BF16)
        q, k, v, qr, kr, vr, pq = _inproj(x2d, row(norm_mix_gain[layer]), w_all, perm, ab, batch, seq, n_hp,
                                          fw, TOKEN_TILE)
        attn = _attention(q, k, v, qr, kr, vr, bias, perm, dilations)
        four = _fourier(pq, fourier_b[layer].reshape(-1), perm)
        x2d = _mix_ffn(attn, four, x2d, row(attn_out_gain[layer]), row(fourier_out_gain[layer]),
                       w_out[layer].astype(BF16), row(norm_ffn_gain[layer]), w_gate[layer].astype(BF16),
                       w_val[layer].astype(BF16), conv_w[layer].astype(F32), row(conv_b[layer]),
                       w_down[layer].astype(BF16), row(final_norm_gain), TOKEN_TILE, FF_CHUNK,
                       layer == depth - 1)
    return x2d.reshape(batch, seq, d_model)
```

```python
import functools
import math

import numpy as np
import jax
import jax.numpy as jnp
from jax import lax
from jax.experimental import pallas as pl
from jax.experimental.pallas import tpu as pltpu

EPS = 1e-6
NEG_INF = -1e30
LOG2E = math.log2(math.e)
HEAD_DIM = 64
DILATED_PATTERNS = ((128, 1), (512, 4), (2048, 16))
N_REL_BUCKETS = 32
REL_MAX_DISTANCE = 1024

LANES = 128
BF16_ROWS = 16
HALF_WINDOW = 64
TQ = 2 * HALF_WINDOW
TK = 4 * HALF_WINDOW
MAX_DIL = max(d for _, d in DILATED_PATTERNS)
GROUP = MAX_DIL * BF16_ROWS
SLABS_PER_STEP = 2
FFT_RADIX = 64
FPERM_ROWS = 512
HALO = BF16_ROWS
TOKEN_TILE = 1024
FF_CHUNK = 256
VMEM_LIMIT = 56 * 1024 * 1024

BF16 = jnp.bfloat16
F32 = jnp.float32


def _cparams(n_axes):
    return pltpu.CompilerParams(dimension_semantics=("arbitrary",) * n_axes,
                                vmem_limit_bytes=VMEM_LIMIT)


def _rms(x, gain):
    ms = jnp.mean(x * x, axis=-1, keepdims=True)
    return x * lax.rsqrt(ms + EPS) * gain


def _group_permutation():
    t = np.arange(GROUP)
    swapped = (t % MAX_DIL) * BF16_ROWS + t // MAX_DIL
    perm = np.zeros((GROUP, GROUP), np.float32)
    perm[swapped, t] = 1.0
    assert MAX_DIL == BF16_ROWS and np.array_equal(perm, perm.T)
    return jnp.asarray(perm, BF16)


def _prep_kernel(fw_ref, cc_ref, sc_ref, ab_ref, *, groups):
    hi = lax.Precision.HIGHEST
    g = pl.program_id(0)
    fw = fw_ref[...]
    a = jnp.dot(cc_ref[...], fw, precision=hi, preferred_element_type=F32)
    b = jnp.dot(sc_ref[...], fw, precision=hi, preferred_element_type=F32)
    gd = fw.shape[0]
    wide = jnp.concatenate([a] * groups + [b] * groups, axis=1)
    col_group = (lax.broadcasted_iota(jnp.int32, wide.shape, 1) // gd) % groups
    ab_ref[...] = jnp.where(col_group == g, wide, 0.0)


def _fold_fourier_weights(fourier_w, seq):
    groups, gd, _ = fourier_w.shape
    ang = 2.0 * np.pi * np.outer(np.arange(gd), np.arange(gd)) / gd
    scale = 1.0 / math.sqrt(seq * gd)
    cc = jnp.asarray(np.cos(ang) * scale, F32)
    sc = jnp.asarray(np.sin(ang) * scale, F32)
    return pl.pallas_call(
        functools.partial(_prep_kernel, groups=groups),
        grid=(groups,),
        in_specs=[pl.BlockSpec((None, gd, gd), lambda g: (g, 0, 0)),
                  pl.BlockSpec((gd, gd), lambda g: (0, 0)),
                  pl.BlockSpec((gd, gd), lambda g: (0, 0))],
        out_specs=pl.BlockSpec((gd, 2 * groups * gd), lambda g: (g, 0)),
        out_shape=jax.ShapeDtypeStruct((groups * gd, 2 * groups * gd), F32),
        compiler_params=_cparams(1),
        name="fourier_weight_fold",
    )(fourier_w, cc, sc)


def _inproj_kernel(x_ref, g_ref, w_ref, perm_ref, fperm_ref, ab_ref, q_ref, k_ref, v_ref, qr_ref, kr_ref,
                   vr_ref, pq_ref, u_ref, *, n_hp, fw):
    h = _rms(x_ref[...], g_ref[...]).astype(BF16)
    tm = h.shape[0]
    perm = perm_ref[...]

    def proj(c0, n):
        return jnp.dot(h, w_ref[:, c0:c0 + n], preferred_element_type=F32)

    aw = n_hp * LANES

    def project(t, tok_ref):
        for c in range(n_hp // 2):
            res = proj(t * aw + c * 2 * LANES, 2 * LANES).astype(BF16)
            tok_ref[2 * c] = res[:, :LANES]
            tok_ref[2 * c + 1] = res[:, LANES:]

    def permute(tok_ref, res_ref):
        for c in range(n_hp // 2):
            for g in range(tm // GROUP):
                tok = jnp.concatenate([tok_ref[2 * c, g * GROUP:(g + 1) * GROUP, :],
                                       tok_ref[2 * c + 1, g * GROUP:(g + 1) * GROUP, :]], axis=1)
                rows = jnp.dot(perm, tok, preferred_element_type=F32).astype(BF16)
                for half in range(2):
                    blk = rows[:, half * LANES:(half + 1) * LANES].reshape(MAX_DIL, BF16_ROWS, LANES)
                    res_ref[2 * c + half, :, g * BF16_ROWS:(g + 1) * BF16_ROWS, :] = blk

    u_ref[...] = proj(3 * aw, fw).astype(BF16)
    project(0, q_ref)
    project(1, k_ref)
    permute(q_ref, qr_ref)
    fp = fperm_ref.shape[0]
    a_per = fp // FFT_RADIX
    u_rows = [jnp.dot(fperm_ref[...], u_ref[s * fp:(s + 1) * fp, :], preferred_element_type=F32).astype(BF16)
              for s in range(tm // fp)]
    project(2, v_ref)
    permute(k_ref, kr_ref)
    planes = jnp.concatenate(
        [jnp.dot(rows, ab_ref[...], preferred_element_type=F32).reshape(FFT_RADIX, a_per, 2 * fw)
         for rows in u_rows], axis=1).astype(BF16)
    for t in range(2):
        pq_ref[t] = planes[:, :, t * fw:(t + 1) * fw]
    permute(v_ref, vr_ref)


def _inproj(x2d, gain, w_all, perm, ab, batch, seq, n_hp, fw, tm):
    n_tok, d_model = x2d.shape
    nt = seq // tm
    tok_map = lambda i: (i // nt, 0, i % nt, 0)
    res_map = lambda i: (i // nt, 0, 0, i % nt, 0)
    tok_shape = jax.ShapeDtypeStruct((batch, n_hp, seq, LANES), BF16)
    res_shape = jax.ShapeDtypeStruct((batch, n_hp, MAX_DIL, seq // MAX_DIL, LANES), BF16)
    R = FFT_RADIX
    fp = min(tm, FPERM_ROWS)
    t = np.arange(fp)
    fperm = np.zeros((fp, fp), np.float32)
    fperm[(t % R) * (fp // R) + t // R, t] = 1.0
    fperm = jnp.asarray(fperm, BF16)
    const = lambda shape: pl.BlockSpec(shape, lambda i: (0,) * len(shape), pipeline_mode=pl.Buffered(1))
    return pl.pallas_call(
        functools.partial(_inproj_kernel, n_hp=n_hp, fw=fw),
        grid=(n_tok // tm,),
        in_specs=[pl.BlockSpec((tm, d_model), lambda i: (i, 0)),
                  const((1, d_model)), const(w_all.shape), const(perm.shape), const(fperm.shape),
                  const(ab.shape)],
        out_specs=[pl.BlockSpec((None, n_hp, tm, LANES), tok_map)] * 3
        + [pl.BlockSpec((None, n_hp, MAX_DIL, tm // MAX_DIL, LANES), res_map)] * 3
        + [pl.BlockSpec((None, 2, R, tm // R, fw), res_map)],
        out_shape=[tok_shape] * 3 + [res_shape] * 3
        + [jax.ShapeDtypeStruct((batch, 2, R, seq // R, fw), BF16)],
        scratch_shapes=[pltpu.VMEM((tm, fw), BF16)],
        compiler_params=_cparams(1),
        name="rmsnorm_inproj",
    )(x2d, gain, w_all, perm, fperm, ab)


def _t5_bucket_static(rel, dtype):
    nb = N_REL_BUCKETS // 2
    max_exact = nb // 2
    n = np.abs(rel)
    nf = np.maximum(n, 1).astype(dtype)
    large = max_exact + (np.log(nf / dtype(max_exact)) / dtype(math.log(REL_MAX_DISTANCE / max_exact))
                         * dtype(nb - max_exact)).astype(np.int32)
    large = np.minimum(large, nb - 1)
    return np.where(rel > 0, nb, 0) + np.where(n < max_exact, n, large)


def _bucket_tiles(dilations):
    qi = np.arange(TQ)[:, None]
    kc = np.arange(TK)[None, :]
    offsets = np.array([0, HALF_WINDOW, 2 * HALF_WINDOW])
    rel = kc[None] - offsets[:, None, None] - qi[None]
    tiles = []
    for d in dilations:
        bkt = _t5_bucket_static(rel * d, np.float32)
        assert np.array_equal(bkt, _t5_bucket_static(rel * d, np.float64))
        tile = np.where(np.abs(rel) <= HALF_WINDOW, bkt, -1)
        pieces = MAX_DIL // d if 1 < d < MAX_DIL else 1
        q_order = (np.arange(TQ) % (TQ // pieces)) * pieces + np.arange(TQ) // (TQ // pieces)
        k_order = (np.arange(TK) % (TK // pieces)) * pieces + np.arange(TK) // (TK // pieces)
        tiles.append(tile[:, q_order][:, :, k_order])
    return np.concatenate(tiles, axis=0).astype(np.int32)


DIAG_LANES = 512


def _diagonals(tiles):
    i = np.arange(TQ)[:, None]
    j = np.arange(TK)[None, :]
    diags = np.full((tiles.shape[0], 8, DIAG_LANES), -1, np.int32)
    constant = []
    for t, tile in enumerate(tiles):
        diag = np.concatenate([tile[::-1, 0], tile[0, 1:]])
        if np.array_equal(tile, diag[j - i + TQ - 1]):
            diags[t, :, :diag.size] = diag
            constant.append(t)
    return diags, constant


def _bias_kernel(table_ref, bkt_ref, diag_ref, out_ref, *, n_heads, constant):
    def lookup(bkt, h):
        acc = jnp.full(bkt.shape, NEG_INF, F32)
        for b in range(N_REL_BUCKETS):
            acc = jnp.where(bkt == b, table_ref[b, h], acc)
        return acc * LOG2E

    t = pl.program_id(0)
    by_diagonal = functools.reduce(jnp.logical_or, [t == c for c in constant], t < 0)

    @pl.when(by_diagonal)
    def _():
        diag = diag_ref[...]
        for h in range(n_heads):
            rows = jnp.tile(lookup(diag, h), (TQ // diag.shape[0], 1))
            rows = pltpu.roll(rows, DIAG_LANES - (TQ - 1), 1, stride=1, stride_axis=0)
            out_ref[h] = rows[:, :TK]

    @pl.when(jnp.logical_not(by_diagonal))
    def _():
        bkt = bkt_ref[...]
        for h in range(n_heads):
            out_ref[h] = lookup(bkt, h)


def _bias_tiles(rel_table, dilations):
    n_heads = rel_table.shape[1]
    tiles = _bucket_tiles(dilations)
    diags, constant = _diagonals(tiles)
    return pl.pallas_call(
        functools.partial(_bias_kernel, n_heads=n_heads, constant=constant),
        grid=(tiles.shape[0],),
        in_specs=[pl.BlockSpec(memory_space=pltpu.SMEM),
                  pl.BlockSpec((None, TQ, TK), lambda t: (t, 0, 0)),
                  pl.BlockSpec((None,) + diags.shape[1:], lambda t: (t, 0, 0))],
        out_specs=pl.BlockSpec((None, n_heads, TQ, TK), lambda t: (t, 0, 0, 0)),
        out_shape=jax.ShapeDtypeStruct((tiles.shape[0], n_heads, TQ, TK), F32),
        compiler_params=_cparams(1),
        name="rel_bias_tiles",
    )(rel_table.astype(F32), jnp.asarray(tiles), jnp.asarray(diags))


def _block_attention(qb, kb, vb, bias2, first_head):
    nt_dims = (((1,), (1,)), ((), ()))
    zero = jnp.zeros_like(qb)
    q2 = jnp.concatenate([jnp.where(first_head, qb, zero), jnp.where(first_head, zero, qb)], axis=0)
    s = lax.dot_general(q2, kb, nt_dims, preferred_element_type=F32) + bias2
    m = jnp.max(s, axis=-1, keepdims=True)
    p = jnp.exp2(s - m).astype(BF16)
    pv = jnp.dot(p, jnp.concatenate([vb, jnp.ones_like(vb)], axis=1), preferred_element_type=F32)
    num = jnp.where(first_head, pv[:TQ, :LANES], pv[TQ:, :LANES])
    den = jnp.where(first_head, pv[:TQ, LANES:], pv[TQ:, LANES:])
    top = jnp.where(first_head, m[:TQ], m[TQ:])
    return num, top, den


def _merge(a, b):
    num_a, top_a, den_a = a
    num_b, top_b, den_b = b
    top = jnp.maximum(top_a, top_b)
    w_a = jnp.exp2(top_a - top)
    w_b = jnp.exp2(top_b - top)
    den = (w_a if den_a is None else w_a * den_a) + (w_b if den_b is None else w_b * den_b)
    return w_a * num_a + w_b * num_b, top, den


def _split_f32(x):
    hi = x.astype(BF16)
    return hi, (x - hi.astype(F32)).astype(BF16)


def _attn_kernel(*refs, dilation, merge_in, out):
    q_ref, k_ref, v_ref, bias_ref = refs[:4]
    n_prev = {None: 0, 'raw': 3, 'norm': 2}[merge_in]
    prev_refs = refs[4:4 + n_prev]
    out_refs = refs[4 + n_prev:]
    n_slabs = q_ref.shape[0]
    pieces = MAX_DIL // dilation if dilation > 1 else 1
    sub_len = q_ref.shape[-2] * pieces
    n_blk = sub_len // TQ
    tq_p, tk_p = TQ // pieces, TK // pieces
    first_head = lax.broadcasted_iota(jnp.int32, (TQ, LANES), 1) < HEAD_DIM

    def load(ref, hh, r, row, n_rows):
        if dilation == 1:
            return ref[hh, row:row + n_rows, :]
        parts = [ref[hh, a * dilation + r, row:row + n_rows, :] for a in range(pieces)]
        return parts[0] if pieces == 1 else jnp.concatenate(parts, axis=0)

    def store(ref, hh, r, row, n_rows, val):
        if dilation == 1:
            ref[hh, row:row + n_rows, :] = val
        else:
            for a in range(pieces):
                ref[hh, a * dilation + r, row:row + n_rows, :] = val[a * n_rows:(a + 1) * n_rows]

    for n in range(n_blk):
        q_row = n * tq_p
        k_row = min(max(q_row - tk_p // 4, 0), sub_len // pieces - tk_p)
        edge = (1 if n > 0 else 0) + (1 if n == n_blk - 1 else 0)
        for r in range(dilation):
            for hh in range(n_slabs):
                qb = load(q_ref, hh, r, q_row, tq_p)
                kb = load(k_ref, hh, r, k_row, tk_p)
                vb = load(v_ref, hh, r, k_row, tk_p)
                bias2 = bias_ref[edge, 2 * hh:2 * hh + 2].reshape(2 * TQ, TK)
                part = _block_attention(qb, kb, vb, bias2, first_head)
                if merge_in:
                    prev = [load(ref, hh, r, q_row, tq_p).astype(F32) for ref in prev_refs]
                    part = _merge(part, prev if merge_in == 'raw' else prev + [None])
                num, top, den = part
                if out == 'raw':
                    vals = (num.astype(BF16), top, den)
                else:
                    o = (num * (1.0 / den)).astype(BF16)
                    vals = (o,) + _split_f32(top + jnp.log2(den)) if out == 'split' else (o,)
                for ref, val in zip(out_refs, vals):
                    store(ref, hh, r, q_row, tq_p, val)


def _attn_residue_kernel(*refs, dilations):
    n = len(dilations)
    q_ref, k_ref, v_ref = refs[:3]
    bias_refs = refs[3:3 + n]
    out_refs = refs[3 + n:6 + n]
    part_refs = refs[6 + n:]
    for i, d in enumerate(dilations):
        last = i == n - 1
        _attn_kernel(q_ref, k_ref, v_ref, bias_refs[i], *(part_refs if i else ()),
                     *(out_refs if last else part_refs),
                     dilation=d, merge_in='raw' if i else None, out='split' if last else 'raw')


def _attn_token_kernel(q_ref, k_ref, v_ref, bias_ref, perm_ref, po_ref, hi_ref, lo_ref, o_ref,
                       o_tok, lse_tok):
    perm = perm_ref[...]
    group_rows = lambda ref, hh, g: ref[hh, :, g * BF16_ROWS:(g + 1) * BF16_ROWS, :].reshape(GROUP, LANES)
    for hh in range(po_ref.shape[0]):
        for g in range(0, po_ref.shape[2] // BF16_ROWS, 2):
            lo = jnp.dot(perm, jnp.concatenate([group_rows(lo_ref, hh, g), group_rows(lo_ref, hh, g + 1)], axis=1),
                         preferred_element_type=F32)
            for j in range(2):
                o_hi = jnp.concatenate([group_rows(po_ref, hh, g + j), group_rows(hi_ref, hh, g + j)], axis=1)
                tok = jnp.dot(perm, o_hi, preferred_element_type=F32)
                rows = slice((g + j) * GROUP, (g + j + 1) * GROUP)
                o_tok[hh, rows, :] = tok[:, :LANES].astype(BF16)
                lse_tok[hh, rows, :] = tok[:, LANES:] + lo[:, j * LANES:(j + 1) * LANES]
    _attn_kernel(q_ref, k_ref, v_ref, bias_ref, o_tok, lse_tok, o_ref,
                 dilation=1, merge_in='norm', out='final')


def _attention(q, k, v, qr, kr, vr, bias, perm, dilations):
    batch, n_hp, seq, _ = q.shape
    assert dilations[0] == MAX_DIL and dilations[-1] == 1 and n_hp % SLABS_PER_STEP == 0
    ns = SLABS_PER_STEP
    res_slab = pl.BlockSpec((None, ns, MAX_DIL, seq // MAX_DIL, LANES), lambda b, h: (b, h, 0, 0, 0))
    tok_slab = pl.BlockSpec((None, ns, seq, LANES), lambda b, h: (b, h, 0, 0))
    res_shape = lambda dt: jax.ShapeDtypeStruct((batch, n_hp, MAX_DIL, seq // MAX_DIL, LANES), dt)
    bias_spec = lambda i: pl.BlockSpec((3, 2 * ns, TQ, TK), lambda b, h: (i, h, 0, 0))

    res_dils = dilations[:-1]
    merged = pl.pallas_call(
        functools.partial(_attn_residue_kernel, dilations=res_dils),
        grid=(batch, n_hp // ns),
        in_specs=[res_slab] * 3 + [bias_spec(i) for i in range(len(res_dils))],
        out_specs=[res_slab] * 3,
        out_shape=[res_shape(BF16)] * 3,
        scratch_shapes=[pltpu.VMEM((ns, MAX_DIL, seq // MAX_DIL, LANES), dt) for dt in (BF16, F32, F32)],
        compiler_params=_cparams(2),
        name="dilated_attn_residue_major",
    )(qr, kr, vr, *([bias] * len(res_dils)))

    return pl.pallas_call(
        _attn_token_kernel,
        grid=(batch, n_hp // ns),
        in_specs=[tok_slab] * 3 + [bias_spec(len(dilations) - 1), pl.BlockSpec(perm.shape, lambda b, h: (0, 0))]
        + [res_slab] * 3,
        out_specs=tok_slab,
        out_shape=jax.ShapeDtypeStruct((batch, n_hp, seq, LANES), BF16),
        scratch_shapes=[pltpu.VMEM((ns, seq, LANES), BF16), pltpu.VMEM((ns, seq, LANES), F32)],
        compiler_params=_cparams(2),
        name="dilated_attn_d1",
    )(q, k, v, bias, perm, *merged)


def _swap_16x16(mats, perm):
    out = []
    for g in range(mats[0].shape[0] // BF16_ROWS):
        rows = jnp.concatenate([m[g * BF16_ROWS:(g + 1) * BF16_ROWS] for m in mats], axis=0)
        swapped = jnp.dot(perm, rows, preferred_element_type=F32).astype(BF16)
        out.append(swapped.reshape(BF16_ROWS, MAX_DIL, rows.shape[-1]))
    return out


def _fft_kernel(w_ref, m_ref, perm_ref, b_ref, v_ref, o_ref, z_ref):
    w = w_ref[...]
    perm = perm_ref[...]
    R = v_ref.shape[2]
    grp = BF16_ROWS
    for bg in range(R // grp):
        zs = []
        for j in range(grp):
            b = bg * grp + j
            rhs = jnp.concatenate([v_ref[0, b], v_ref[1, b]], axis=0)
            zs.append(jnp.dot(w, rhs, preferred_element_type=F32).astype(BF16))
        for plane in range(2):
            groups = _swap_16x16([z[plane * R:(plane + 1) * R] for z in zs], perm)
            for g, blk in enumerate(groups):
                z_ref[g * grp:(g + 1) * grp, plane, bg * grp:(bg + 1) * grp, :] = blk
    for kg in range(R // grp):
        xs = []
        for j in range(grp):
            k1 = kg * grp + j
            zc = z_ref[k1].reshape(2 * R, z_ref.shape[3])
            xs.append((jnp.dot(m_ref[k1], zc, preferred_element_type=F32) + b_ref[...]).astype(BF16))
        for g, blk in enumerate(_swap_16x16(xs, perm)):
            o_ref[g * grp:(g + 1) * grp, kg, :, :] = blk


def _fourier(pq, fourier_b, perm):
    batch, _, R, _, fw = pq.shape
    seq = R * R
    assert R == FFT_RADIX and R % BF16_ROWS == 0 and MAX_DIL == BF16_ROWS
    i = np.arange(R)
    ang1 = 2.0 * np.pi * np.outer(i, i) / R
    c1, s1 = np.cos(ang1), np.sin(ang1)
    w_cat = jnp.asarray(np.block([[c1, -s1], [-s1, -c1]]), BF16)
    k_all = i[:, None, None] + R * i[None, :, None]
    ang2 = 2.0 * np.pi * ((k_all * i[None, None, :]) % seq) / seq
    m_cat = jnp.asarray(np.concatenate([np.cos(ang2), np.sin(ang2)], axis=-1), BF16)
    grp = BF16_ROWS
    const = lambda shape: pl.BlockSpec(shape, lambda b: (0,) * len(shape), pipeline_mode=pl.Buffered(1))
    out = pl.pallas_call(
        _fft_kernel,
        grid=(batch,),
        in_specs=[const((2 * R, 2 * R)), const(m_cat.shape), const(perm.shape), const((1, fw)),
                  pl.BlockSpec((None, 2, R, R, fw), lambda b: (b, 0, 0, 0, 0))],
        out_specs=pl.BlockSpec((None, R, R // grp, grp, fw), lambda b: (b, 0, 0, 0, 0)),
        out_shape=jax.ShapeDtypeStruct((batch, R, R // grp, grp, fw), BF16),
        scratch_shapes=[pltpu.VMEM((R, 2, R, fw), BF16)],
        compiler_params=_cparams(1),
        name="fft_positions",
    )(w_cat, m_cat, perm, fourier_b.reshape(1, fw).astype(F32), pq)
    return out.reshape(batch, seq, fw)


def _mix_ffn_kernel(ap_ref, a_ref, an_ref, fp_ref, f_ref, fn_ref, xp_ref, x_ref, xn_ref,
                    ga_ref, gf_ref, wo_ref, g_ref, wg_ref, wv_ref, cw_ref, cb_ref, wd_ref, gfin_ref,
                    o_ref, act_ref, *, tiles_per_seq, fc, final_norm, n_hp):
    i = pl.program_id(0)
    tm = x_ref.shape[0]
    n_ext = tm + 2 * HALO
    ext = lambda prev, cur, nxt: jnp.concatenate([prev, cur, nxt], axis=0)

    parts = [ext(ap_ref[j], a_ref[j], an_ref[j]).astype(F32) for j in range(n_hp)]
    ssq = functools.reduce(lambda a, b: a + b,
                           [jnp.sum(p * p, axis=-1, keepdims=True) for p in parts])
    inv = lax.rsqrt(ssq / (n_hp * LANES) + EPS)
    ga = ga_ref[...]
    cols = [(p * inv * ga[:, j * LANES:(j + 1) * LANES]).astype(BF16) for j, p in enumerate(parts)]
    cols.append(_rms(ext(fp_ref[...], f_ref[...], fn_ref[...]).astype(F32), gf_ref[...]).astype(BF16))
    mixed = jnp.concatenate(cols, axis=1)
    x_ext = ext(xp_ref[...], x_ref[...], xn_ref[...]) + jnp.dot(mixed, wo_ref[...],
                                                                 preferred_element_type=F32)
    x = x_ext[HALO:HALO + tm]

    keep_prev = (i % tiles_per_seq != 0).astype(F32)
    keep_next = (i % tiles_per_seq != tiles_per_seq - 1).astype(F32)
    row = lax.broadcasted_iota(jnp.int32, (n_ext, 1), 0)
    keep = jnp.where(row < HALO, keep_prev, jnp.where(row >= HALO + tm, keep_next, 1.0))
    h_ext = (_rms(x_ext, g_ref[...]) * keep).astype(BF16)
    h = h_ext[HALO:HALO + tm]
    d_ff = wg_ref.shape[1]

    def up(c0):
        cs = slice(c0, c0 + fc)
        return (jnp.dot(h_ext, wg_ref[:, cs], preferred_element_type=F32),
                jnp.dot(h, wv_ref[:, cs], preferred_element_type=F32))

    nxt = up(0)
    for c0 in range(0, d_ff, fc):
        cs = slice(c0, c0 + fc)
        g, val = nxt
        if c0 + fc < d_ff:
            nxt = up(c0 + fc)
        g_prev = pltpu.roll(g, 1, axis=0)[HALO:HALO + tm]
        g_next = pltpu.roll(g, n_ext - 1, axis=0)[HALO:HALO + tm]
        cw = cw_ref[:, cs]
        conv = cw[0:1] * g_prev + cw[1:2] * g[HALO:HALO + tm] + cw[2:3] * g_next + cb_ref[:, cs]
        act_ref[:, cs] = (conv * (1.0 / (1.0 + jnp.exp(-conv))) * val).astype(BF16)
    y = x + jnp.dot(act_ref[...], wd_ref[...], preferred_element_type=F32)
    o_ref[...] = _rms(y, gfin_ref[...]) if final_norm else y


def _mix_ffn(attn, four, x2d, ga, gf, w_out, gain, wg, wv, conv_w, conv_b, wd, gfin, tm, fc, final_norm):
    batch, n_hp, seq, _ = attn.shape
    fw = four.shape[-1]
    n_tok, d_model = x2d.shape
    d_ff = wg.shape[1]
    nt = seq // tm
    per_tile = tm // HALO
    last = seq // HALO - 1
    prev_blk = lambda i: jnp.maximum((i % nt) * per_tile - 1, 0)
    next_blk = lambda i: jnp.minimum((i % nt + 1) * per_tile, last)
    seq_blk = lambda i: (i // nt) * (last + 1)
    const = lambda shape: pl.BlockSpec(shape, lambda i: (0,) * len(shape), pipeline_mode=pl.Buffered(1))

    def triple(cur_shape, halo_shape, index):
        return [pl.BlockSpec(halo_shape, lambda i: index(i // nt, prev_blk(i))),
                pl.BlockSpec(cur_shape, lambda i: index(i // nt, i % nt)),
                pl.BlockSpec(halo_shape, lambda i: index(i // nt, next_blk(i)))]

    in_specs = (triple((None, n_hp, tm, LANES), (None, n_hp, HALO, LANES), lambda b, t: (b, 0, t, 0))
                + triple((None, tm, fw), (None, HALO, fw), lambda b, t: (b, t, 0))
                + [pl.BlockSpec((HALO, d_model), lambda i: (seq_blk(i) + prev_blk(i), 0)),
                   pl.BlockSpec((tm, d_model), lambda i: (i, 0)),
                   pl.BlockSpec((HALO, d_model), lambda i: (seq_blk(i) + next_blk(i), 0))]
                + [const((1, n_hp * LANES)), const((1, fw)), const(w_out.shape), const((1, d_model)),
                   const(wg.shape), const(wv.shape), const(conv_w.shape), const((1, d_ff)), const(wd.shape),
                   const((1, d_model))])
    return pl.pallas_call(
        functools.partial(_mix_ffn_kernel, tiles_per_seq=nt, fc=fc, final_norm=final_norm, n_hp=n_hp),
        grid=(n_tok // tm,),
        in_specs=in_specs,
        out_specs=pl.BlockSpec((tm, d_model), lambda i: (i, 0)),
        out_shape=jax.ShapeDtypeStruct((n_tok, d_model), F32),
        scratch_shapes=[pltpu.VMEM((tm, d_ff), BF16)],
        compiler_params=_cparams(1),
        name="outproj_convglu_ffn",
    )(attn, attn, attn, four, four, four, x2d, x2d, x2d, ga, gf, w_out, gain, wg, wv, conv_w, conv_b, wd, gfin)


def kernel(x, norm_mix_gain, w_in, attn_out_gain, rel_bias_table, fourier_w, fourier_b, fourier_out_gain, w_out, norm_ffn_gain, w_gate, w_val, conv_w, conv_b, w_down, final_norm_gain):
    batch, seq, d_model = x.shape
    depth = w_in.shape[0]
    n_heads = rel_bias_table.shape[1]
    attn_w = n_heads * HEAD_DIM
    n_hp = attn_w // LANES
    fw = fourier_w.shape[1] * fourier_w.shape[2]
    assert all(w // (2 * d) == HALF_WINDOW for w, d in DILATED_PATTERNS)
    assert seq == FFT_RADIX ** 2 and seq % TOKEN_TILE == 0 and seq % (MAX_DIL * TK) == 0
    assert TOKEN_TILE % FPERM_ROWS == 0 and (TOKEN_TILE // FFT_RADIX) % BF16_ROWS == 0
    assert n_hp % 2 == 0 and n_hp % SLABS_PER_STEP == 0 and w_gate.shape[-1] % FF_CHUNK == 0
    dilations = sorted((d for _, d in DILATED_PATTERNS), reverse=True)
    row = lambda g: g.reshape(1, -1).astype(F32)
    perm = _group_permutation()
    bias = _bias_tiles(rel_bias_table, dilations)

    x2d = x.reshape(batch * seq, d_model)
    for layer in range(depth):
        w_l = w_in[layer]
        ab = _fold_fourier_weights(fourier_w[layer], seq).astype(BF16)
        w_all = jnp.concatenate([w_l[:, :attn_w] * (HEAD_DIM ** -0.5 * LOG2E), w_l[:, attn_w:]],
                                axis=1).astype(BF16)
        q, k, v, qr, kr, vr, pq = _inproj(x2d, row(norm_mix_gain[layer]), w_all, perm, ab, batch, seq, n_hp,
                                          fw, TOKEN_TILE)
        attn = _attention(q, k, v, qr, kr, vr, bias, perm, dilations)
        four = _fourier(pq, fourier_b[layer].reshape(-1), perm)
        x2d = _mix_ffn(attn, four, x2d, row(attn_out_gain[layer]), row(fourier_out_gain[layer]),
                       w_out[layer].astype(BF16), row(norm_ffn_gain[layer]), w_gate[layer].astype(BF16),
                       w_val[layer].astype(BF16), conv_w[layer].astype(F32), row(conv_b[layer]),
                       w_down[layer].astype(BF16), row(final_norm_gain), TOKEN_TILE, FF_CHUNK,
                       layer == depth - 1)
    return x2d.reshape(batch, seq, d_model)
```
